```python
import math
import jax
import jax.numpy as jnp
from jax import lax
import numpy as np

D_MODEL = 1024
BATCH = 8
SEQ = 4096
DEPTH = 2

CTX_LEN = 256
GRID_W = 64
N_MOD = 6
EPS = 1e-6
DA_WIDTH = D_MODEL // 2
DA_HEAD_DIM = 64
DA_V_DIM = 2 * DA_HEAD_DIM
DA_HEADS = DA_WIDTH // DA_V_DIM
DA_COLS = 3 * DA_WIDTH
Q_BLOCK = 128
ROPE_BASE = 10000.0
ROPE_NFREQ = DA_HEAD_DIM // 4
RW_WIDTH = D_MODEL - DA_WIDTH
RW_HEAD_DIM = 64
RW_HEADS = RW_WIDTH // RW_HEAD_DIM
RW_DECAY_LORA = 64
RW_AAA_LORA = 64
RW_GATE_LORA = 128
RW_COLS = 3 * RW_WIDTH + 2 * RW_DECAY_LORA + 2 * RW_AAA_LORA + RW_GATE_LORA
RW_SPLITS = (RW_WIDTH, 2 * RW_WIDTH, 3 * RW_WIDTH, 3 * RW_WIDTH + RW_DECAY_LORA, 3 * RW_WIDTH + 2 * RW_DECAY_LORA, 3 * RW_WIDTH + 2 * RW_DECAY_LORA + RW_AAA_LORA, 3 * RW_WIDTH + 2 * RW_DECAY_LORA + 2 * RW_AAA_LORA)
RW_GN_EPS = 64e-5
EVEN_IN_COLS = DA_COLS + RW_COLS
M_INNER = 2 * D_MODEL
M_HEAD_DIM = 64
M_HEADS = M_INNER // M_HEAD_DIM
M_GROUPS = 4
M_STATE = 128
M_BC = M_GROUPS * M_STATE
M_CONV = 3
M_CONV_DIM = M_INNER + 2 * M_BC
M_CHUNK = 128
ODD_IN_COLS = M_INNER + M_CONV_DIM + 2 * M_HEADS
PEER_HEADS = 8
PEER_NKEYS = 128
PEER_EXPERTS = PEER_NKEYS * PEER_NKEYS
PEER_HALF = 128
PEER_QDIM = 2 * PEER_HALF
PEER_TOPK = 16
PEER_BLOCK = 128

kernel_name = 'hybrid_diffattn_rwkv7_ssd_peer'

F32 = jnp.float32


def rmsnorm(x, g):
    xf = x.astype(F32)
    y = xf * lax.rsqrt(jnp.mean(xf * xf, axis=-1, keepdims=True) + EPS)
    return (y * g.astype(F32)).astype(x.dtype)


def modulate(h, shift, scale):
    return h * (1.0 + scale) + shift


def ada_mod(cvec, w, b):
    m = jax.nn.silu(cvec) @ w + b
    return [t[:, None, :] for t in jnp.split(m, N_MOD, axis=-1)]


def centred_shift(u):
    up = jnp.pad(u, ((0, 0), (1, 1), (0, 0)))
    return 0.5 * (up[:, :-2] + up[:, 2:])


def dwconv_centred(u, w, b):
    pad = w.shape[0] // 2
    out = lax.conv_general_dilated(u, w[:, None, :].astype(u.dtype), window_strides=(1,), padding=((pad, pad),), dimension_numbers=('NWC', 'WIO', 'NWC'), feature_group_count=u.shape[-1])
    return out + b


def axial_rope_tables(rows):
    n = rows * GRID_W
    row = jnp.repeat(jnp.arange(rows, dtype=F32), GRID_W)
    col = (jnp.arange(n) % GRID_W).astype(F32)
    inv = ROPE_BASE ** (-jnp.arange(ROPE_NFREQ, dtype=F32) / ROPE_NFREQ)
    ang = jnp.stack([row[:, None] * inv, col[:, None] * inv], axis=1)
    return jnp.cos(ang), jnp.sin(ang)


def apply_axial_rope(t, cos, sin):
    shp = t.shape
    tt = t.reshape(shp[:-1] + (2, 2, ROPE_NFREQ)).astype(F32)
    t1, t2 = tt[..., 0, :], tt[..., 1, :]
    cb = cos[None, :, None, None]
    sb = sin[None, :, None, None]
    out = jnp.stack([t1 * cb - t2 * sb, t2 * cb + t1 * sb], axis=-2)
    return out.reshape(shp).astype(t.dtype)


def diff_attention(q, k, v, lam):
    s = jnp.einsum('bqhmd,bkhmd->bhmqk', q, k).astype(F32) * (DA_HEAD_DIM ** -0.5)
    p = jax.nn.softmax(s, axis=-1)
    a = p[:, :, 0] - lam * p[:, :, 1]
    return jnp.einsum('bhqk,bkhe->bqhe', a.astype(v.dtype), v)


def split_attn(p):
    bsz, L = p.shape[:2]
    q = p[..., :DA_WIDTH].reshape(bsz, L, DA_HEADS, 2, DA_HEAD_DIM)
    k = p[..., DA_WIDTH:2 * DA_WIDTH].reshape(bsz, L, DA_HEADS, 2, DA_HEAD_DIM)
    v = p[..., 2 * DA_WIDTH:DA_COLS].reshape(bsz, L, DA_HEADS, DA_V_DIM)
    return q, k, v


def rwkv7_scan(r, decay, k, v, a, b, s0, reverse):
    xs = tuple(jnp.moveaxis(t, 1, 0) for t in (r, decay, k, v, a, b))

    def step(S, inp):
        rt, wt, kt, vt, at, bt = inp
        sa = jnp.einsum('bhvk,bhk->bhv', S, at)
        S = S * wt[:, :, None, :] + sa[..., None] * bt[:, :, None, :] + vt[..., None] * kt[:, :, None, :]
        return S, jnp.einsum('bhvk,bhk->bhv', S, rt)

    s_final, ys = lax.scan(step, s0, xs, reverse=reverse)
    return jnp.moveaxis(ys, 0, 1), s_final


def rwkv_prepare(u, mu, w0, w2, a0, a2, g2, k_k, k_a):
    bsz, L, _ = u.shape
    u = u + mu * (centred_shift(u) - u)
    r, k, v, w_f, w_b, a_f, a_b, g_in = jnp.split(u, RW_SPLITS, axis=-1)
    heads = lambda t: t.reshape(bsz, L, RW_HEADS, RW_HEAD_DIM)
    g = jax.nn.sigmoid(g_in) @ g2
    kk = heads(k * k_k).astype(F32)
    kk = kk / jnp.maximum(jnp.linalg.norm(kk, axis=-1, keepdims=True), 1e-12)
    per_dir = []
    for d, (w_in, a_in) in enumerate(((w_f, a_f), (w_b, a_b))):
        w_log = -jax.nn.softplus(-(w0[d] + jnp.tanh(w_in) @ w2[d])) - 0.5
        decay = jnp.exp(-jnp.exp(w_log.astype(F32)))
        a = jax.nn.sigmoid(a0[d] + a_in @ a2[d])
        k_d = k * (1.0 + (a - 1.0) * k_a)
        per_dir.append((heads(decay), heads(k_d), kk * heads(a).astype(F32)))
    return heads(r), heads(v), g, kk, per_dir


def rwkv_mix(r, v, g, kk, per_dir, init_states, r_k, lnx_w, lnx_b):
    bsz, L = r.shape[:2]
    ys = []
    finals = []
    for d in range(2):
        decay, k_d, b_vec = per_dir[d]
        y, s_t = rwkv7_scan(r, decay, k_d, v, -kk, b_vec, init_states[d], reverse=(d == 1))
        ys.append(y)
        finals.append(s_t)
    y = ys[0] + ys[1]
    mean = jnp.mean(y, axis=-1, keepdims=True)
    var = jnp.mean(jnp.square(y - mean), axis=-1, keepdims=True)
    y = (y - mean) * lax.rsqrt(var + RW_GN_EPS) * lnx_w.reshape(RW_HEADS, RW_HEAD_DIM) + lnx_b.reshape(RW_HEADS, RW_HEAD_DIM)
    bonus = (jnp.sum(r * per_dir[0][1] * r_k, axis=-1, keepdims=True) + jnp.sum(r * per_dir[1][1] * r_k, axis=-1, keepdims=True)) * v
    out = (y + bonus).reshape(bsz, L, RW_WIDTH) * g
    return out.astype(g.dtype), finals


def even_mixer(p_lat, p_ctx, da_lambda, da_subln, rw_mu, rw_w0, rw_w2, rw_a0, rw_a2, rw_g2, rw_kk, rw_ka, rw_rk, rw_lnx_w, rw_lnx_b, lam_init, rope):
    bsz, L = p_lat.shape[:2]
    lamp = da_lambda.astype(F32)
    lam = jnp.exp(jnp.sum(lamp[0] * lamp[1])) - jnp.exp(jnp.sum(lamp[2] * lamp[3])) + lam_init
    q_l, k_l, v_l = split_attn(p_lat)
    q_c, k_c, v_c = split_attn(p_ctx)
    o_c = diff_attention(q_c, k_c, v_c, lam)
    cos, sin = rope
    q_l = apply_axial_rope(q_l, cos, sin)
    k_l = apply_axial_rope(k_l, cos, sin)
    k_all = jnp.concatenate([k_c, k_l], axis=1)
    v_all = jnp.concatenate([v_c, v_l], axis=1)
    nb = L // Q_BLOCK
    q_blocks = jnp.moveaxis(q_l.reshape(bsz, nb, Q_BLOCK, DA_HEADS, 2, DA_HEAD_DIM), 1, 0)
    o_l = lax.map(lambda qb: diff_attention(qb, k_all, v_all, lam), q_blocks)
    o_l = jnp.moveaxis(o_l, 0, 1).reshape(bsz, L, DA_HEADS, DA_V_DIM)
    post = lambda o: (rmsnorm(o, da_subln) * (1.0 - lam_init)).reshape(o.shape[0], o.shape[1], DA_WIDTH)
    rw_args = (rw_mu, rw_w0, rw_w2, rw_a0, rw_a2, rw_g2, rw_kk, rw_ka)
    zero = jnp.zeros((bsz, RW_HEADS, RW_HEAD_DIM, RW_HEAD_DIM), F32)
    r_c, vv_c, g_c, kk_c, dir_c = rwkv_prepare(p_ctx[..., DA_COLS:], *rw_args)
    or_c, ctx_states = rwkv_mix(r_c, vv_c, g_c, kk_c, dir_c, [zero, zero], rw_rk, rw_lnx_w, rw_lnx_b)
    r_l, vv_l, g_l, kk_l, dir_l = rwkv_prepare(p_lat[..., DA_COLS:], *rw_args)
    or_l, _ = rwkv_mix(r_l, vv_l, g_l, kk_l, dir_l, ctx_states, rw_rk, rw_lnx_w, rw_lnx_b)
    o_lat = jnp.concatenate([post(o_l), or_l], axis=-1)
    o_ctx = jnp.concatenate([post(o_c), or_c], axis=-1)
    return o_lat, o_ctx


def peer(h, wq, keys, U, V):
    bsz, L, D = h.shape
    q = (h @ wq).reshape(bsz, L, PEER_HEADS, 2, PEER_HALF)
    s = jnp.einsum('blhpd,hpkd->blhpk', q, keys).astype(F32)
    v1, i1 = lax.top_k(s[..., 0, :], PEER_TOPK)
    v2, i2 = lax.top_k(s[..., 1, :], PEER_TOPK)
    cand = (v1[..., :, None] + v2[..., None, :]).reshape(bsz, L, PEER_HEADS, PEER_TOPK * PEER_TOPK)
    cidx = (i1[..., :, None] * PEER_NKEYS + i2[..., None, :]).reshape(bsz, L, PEER_HEADS, PEER_TOPK * PEER_TOPK)
    top_s, pos = lax.top_k(cand, PEER_TOPK)
    idx = jnp.take_along_axis(cidx, pos, axis=-1)
    gates = jax.nn.softmax(top_s, axis=-1)
    T = bsz * L
    nblk = T // PEER_BLOCK
    hb = h.reshape(nblk, PEER_BLOCK, D)
    ib = idx.reshape(nblk, PEER_BLOCK, PEER_HEADS * PEER_TOPK)
    gb = gates.reshape(nblk, PEER_BLOCK, PEER_HEADS * PEER_TOPK).astype(h.dtype)

    def block(args):
        hx, ix, gx = args
        act = jax.nn.gelu(jnp.einsum('td,ted->te', hx, U[ix]), approximate=False)
        return jnp.einsum('te,ted->td', gx * act, V[ix])

    out = lax.map(block, (hb, ib, gb))
    return out.reshape(bsz, L, D)


def even_layer(x_lat, x_ctx, c, c_ctx, ada_w, ada_b, norm1, norm2, win, da_lambda, da_subln, rw_mu, rw_w0, rw_w2, rw_a0, rw_a2, rw_g2, rw_kk, rw_ka, rw_rk, rw_lnx_w, rw_lnx_b, wout, peer_q, peer_keys, peer_u, peer_v, lam_init, rope, last):
    ml = ada_mod(c, ada_w, ada_b)
    mc = ada_mod(c_ctx[None, :], ada_w, ada_b)
    p_lat = modulate(rmsnorm(x_lat, norm1), ml[0], ml[1]) @ win
    p_ctx = modulate(rmsnorm(x_ctx, norm1), mc[0], mc[1]) @ win
    o_lat, o_ctx = even_mixer(p_lat, p_ctx, da_lambda, da_subln, rw_mu, rw_w0, rw_w2, rw_a0, rw_a2, rw_g2, rw_kk, rw_ka, rw_rk, rw_lnx_w, rw_lnx_b, lam_init, rope)
    x_lat = x_lat + ml[2] * (o_lat @ wout)
    x_lat = x_lat + ml[5] * peer(modulate(rmsnorm(x_lat, norm2), ml[3], ml[4]), peer_q, peer_keys, peer_u, peer_v)
    if last:
        return x_lat, None
    x_ctx = x_ctx + mc[2] * (o_ctx @ wout)
    x_ctx = x_ctx + mc[5] * peer(modulate(rmsnorm(x_ctx, norm2), mc[3], mc[4]), peer_q, peer_keys, peer_u, peer_v)
    return x_lat, x_ctx


def mamba_pre(p, conv_w, conv_b, dt_bias):
    bsz, L, _ = p.shape
    xbc = jax.nn.silu(dwconv_centred(p[..., :M_CONV_DIM], conv_w, conv_b))
    xs = xbc[..., :M_INNER].reshape(bsz, L, M_HEADS, M_HEAD_DIM)
    Bm = xbc[..., M_INNER:M_INNER + M_BC].reshape(bsz, L, M_GROUPS, M_STATE)
    Cm = xbc[..., M_INNER + M_BC:].reshape(bsz, L, M_GROUPS, M_STATE)
    dt = jax.nn.softplus(p[..., M_CONV_DIM:].reshape(bsz, L, 2, M_HEADS).astype(F32) + dt_bias)
    return xs, Bm, Cm, dt


def ssd_chunked(xs, dt, A, Bm, Cm, h0, with_output):
    bsz, L, H, P = xs.shape
    G, N = Bm.shape[2], Bm.shape[3]
    hg = H // G
    nc = L // M_CHUNK
    dtf = dt.astype(F32)
    xdt = (xs * dtf[..., None]).reshape(bsz, nc, M_CHUNK, G, hg, P)
    a_cs = jnp.cumsum(dtf.reshape(bsz, nc, M_CHUNK, G, hg) * A.reshape(G, hg), axis=2)
    Bc = Bm.reshape(bsz, nc, M_CHUNK, G, N)
    a_last = a_cs[:, :, -1]
    states = jnp.einsum('bclgn,bclgh,bclghp->bcghpn', Bc, jnp.exp(a_last[:, :, None] - a_cs), xdt)

    def carry_step(h, inp):
        al, st = inp
        return h * jnp.exp(al)[..., None, None] + st, h

    h_final, h_prev = lax.scan(carry_step, h0, (jnp.moveaxis(a_last, 1, 0), jnp.moveaxis(states, 1, 0)))
    if not with_output:
        return None, h_final
    Cc = Cm.reshape(bsz, nc, M_CHUNK, G, N)
    h_prev = jnp.moveaxis(h_prev, 0, 1)
    seg = a_cs[:, :, :, None] - a_cs[:, :, None, :]
    lower = jnp.tril(jnp.ones((M_CHUNK, M_CHUNK), dtype=bool))
    lmat = jnp.exp(jnp.where(lower[:, :, None, None], seg, -jnp.inf))
    cb = jnp.einsum('bclgn,bcsgn->bclsg', Cc, Bc)
    y_diag = jnp.einsum('bclsg,bclsgh,bcsghp->bclghp', cb, lmat, xdt)
    y_off = jnp.einsum('bclgn,bcghpn,bclgh->bclghp', Cc, h_prev, jnp.exp(a_cs))
    return (y_diag + y_off).reshape(bsz, L, H, P), h_final


def ssd_bidir(xs, Bm, Cm, dt, A, h0_f, h0_b, with_output):
    flip = lambda t: jnp.flip(t, axis=1)
    y_f, h_f = ssd_chunked(xs, dt[:, :, 0], A[0], Bm, Cm, h0_f, with_output)
    y_b, h_b = ssd_chunked(flip(xs), flip(dt[:, :, 1]), A[1], flip(Bm), flip(Cm), h0_b, with_output)
    y = y_f + flip(y_b) if with_output else None
    return y, h_f, h_b


def mamba_out(y, xs, z, d_skip, gnorm, wout):
    bsz, L = z.shape[:2]
    y = y + d_skip.astype(F32)[:, None] * xs.astype(F32)
    y = y.reshape(bsz, L, M_INNER) * jax.nn.silu(z.astype(F32))
    yg = y.reshape(bsz, L, M_GROUPS, M_INNER // M_GROUPS)
    yg = yg * lax.rsqrt(jnp.mean(yg * yg, axis=-1, keepdims=True) + EPS)
    y = yg.reshape(bsz, L, M_INNER) * gnorm
    return y.astype(z.dtype) @ wout


def odd_layer(x_lat, x_ctx, c, c_ctx, ada_w, ada_b, norm1, norm2, win, conv_w, conv_b, dt_bias, a_log, d_skip, gnorm, wout, peer_q, peer_keys, peer_u, peer_v, last):
    bsz = x_lat.shape[0]
    ml = ada_mod(c, ada_w, ada_b)
    mc = ada_mod(c_ctx[None, :], ada_w, ada_b)
    A = -jnp.exp(a_log.astype(F32))
    p_lat = modulate(rmsnorm(x_lat, norm1), ml[0], ml[1]) @ win
    h_ctx = modulate(rmsnorm(x_ctx, norm1), mc[0], mc[1])
    p_ctx = h_ctx @ (win[:, M_INNER:] if last else win)
    rest_ctx = p_ctx if last else p_ctx[..., M_INNER:]
    xs_c, Bm_c, Cm_c, dt_c = mamba_pre(rest_ctx, conv_w, conv_b, dt_bias)
    h0 = jnp.zeros((bsz, M_GROUPS, M_HEADS // M_GROUPS, M_HEAD_DIM, M_STATE), F32)
    y_c, h_f, h_b = ssd_bidir(xs_c, Bm_c, Cm_c, dt_c, A, h0, h0, not last)
    xs_l, Bm_l, Cm_l, dt_l = mamba_pre(p_lat[..., M_INNER:], conv_w, conv_b, dt_bias)
    y_l, _, _ = ssd_bidir(xs_l, Bm_l, Cm_l, dt_l, A, h_f, h_b, True)
    x_lat = x_lat + ml[2] * mamba_out(y_l, xs_l, p_lat[..., :M_INNER], d_skip, gnorm, wout)
    x_lat = x_lat + ml[5] * peer(modulate(rmsnorm(x_lat, norm2), ml[3], ml[4]), peer_q, peer_keys, peer_u, peer_v)
    if last:
        return x_lat, None
    x_ctx = x_ctx + mc[2] * mamba_out(y_c, xs_c, p_ctx[..., :M_INNER], d_skip, gnorm, wout)
    x_ctx = x_ctx + mc[5] * peer(modulate(rmsnorm(x_ctx, norm2), mc[3], mc[4]), peer_q, peer_keys, peer_u, peer_v)
    return x_lat, x_ctx


def setup_inputs(seed: int = 0) -> dict:
    key = jax.random.key(seed)
    keys = list(jax.random.split(key, 80))

    def nk():
        return keys.pop()

    def nrm(shape, scale):
        return scale * jax.random.normal(nk(), shape, F32)

    def gain(n):
        return 1.0 + 0.02 * jax.random.normal(nk(), (n,), F32)

    D = D_MODEL
    inp = {}
    inp['x'] = nrm((BATCH, SEQ, D), 1.0)
    inp['c'] = nrm((BATCH, D), 1.0)
    inp['ctx'] = nrm((BATCH, CTX_LEN, D), 1.0)
    inp['c_ctx'] = nrm((D,), 1.0)
    inp['ada_w_0'] = nrm((D, N_MOD * D), 0.5 * D ** -0.5)
    inp['ada_b_0'] = nrm((N_MOD * D,), 0.02)
    inp['norm1_0'] = gain(D)
    inp['norm2_0'] = gain(D)
    inp['win_0'] = nrm((D, EVEN_IN_COLS), D ** -0.5)
    inp['da_lambda_0'] = nrm((4, DA_HEAD_DIM), 0.1)
    inp['da_subln_0'] = gain(DA_V_DIM)
    inp['rw_mu_0'] = jax.random.uniform(nk(), (RW_COLS,), F32)
    inp['rw_w0_0'] = jax.random.uniform(nk(), (2, RW_WIDTH), F32, -6.0, -1.0)
    inp['rw_w2_0'] = nrm((2, RW_DECAY_LORA, RW_WIDTH), 0.1)
    inp['rw_a0_0'] = nrm((2, RW_WIDTH), 0.1)
    inp['rw_a2_0'] = nrm((2, RW_AAA_LORA, RW_WIDTH), 0.1)
    inp['rw_g2_0'] = nrm((RW_GATE_LORA, RW_WIDTH), RW_GATE_LORA ** -0.5)
    inp['rw_kk_0'] = 0.85 + nrm((RW_WIDTH,), 0.02)
    inp['rw_ka_0'] = gain(RW_WIDTH)
    inp['rw_rk_0'] = nrm((RW_HEADS, RW_HEAD_DIM), 0.1)
    inp['rw_lnx_w_0'] = gain(RW_WIDTH)
    inp['rw_lnx_b_0'] = nrm((RW_WIDTH,), 0.02)
    inp['wout_0'] = nrm((D, D), D ** -0.5)
    inp['peer_q_0'] = nrm((D, PEER_HEADS * PEER_QDIM), D ** -0.5)
    inp['peer_keys_0'] = nrm((PEER_HEADS, 2, PEER_NKEYS, PEER_HALF), PEER_HALF ** -0.5)
    inp['peer_u_0'] = nrm((PEER_EXPERTS, D), D ** -0.5)
    inp['peer_v_0'] = nrm((PEER_EXPERTS, D), 0.5)
    inp['ada_w_1'] = nrm((D, N_MOD * D), 0.5 * D ** -0.5)
    inp['ada_b_1'] = nrm((N_MOD * D,), 0.02)
    inp['norm1_1'] = gain(D)
    inp['norm2_1'] = gain(D)
    inp['win_1'] = nrm((D, ODD_IN_COLS), D ** -0.5)
    inp['conv_w_1'] = nrm((M_CONV, M_CONV_DIM), M_CONV ** -0.5)
    inp['conv_b_1'] = nrm((M_CONV_DIM,), 0.02)
    u = jax.random.uniform(nk(), (2, M_HEADS), F32)
    dt0 = jnp.exp(u * (math.log(0.1) - math.log(1e-3)) + math.log(1e-3))
    inp['dt_bias_1'] = dt0 + jnp.log(-jnp.expm1(-dt0))
    inp['a_log_1'] = jnp.log(jax.random.uniform(nk(), (2, M_HEADS), F32, 1.0, 16.0))
    inp['d_skip_1'] = gain(M_HEADS)
    inp['gnorm_1'] = gain(M_INNER)
    inp['wout_1'] = nrm((M_INNER, D), M_INNER ** -0.5)
    inp['peer_q_1'] = nrm((D, PEER_HEADS * PEER_QDIM), D ** -0.5)
    inp['peer_keys_1'] = nrm((PEER_HEADS, 2, PEER_NKEYS, PEER_HALF), PEER_HALF ** -0.5)
    inp['peer_u_1'] = nrm((PEER_EXPERTS, D), D ** -0.5)
    inp['peer_v_1'] = nrm((PEER_EXPERTS, D), 0.5)
    inp['norm_f'] = gain(D)
    return inp


def reference(x, c, ctx, c_ctx, ada_w_0, ada_b_0, norm1_0, norm2_0, win_0, da_lambda_0, da_subln_0, rw_mu_0, rw_w0_0, rw_w2_0, rw_a0_0, rw_a2_0, rw_g2_0, rw_kk_0, rw_ka_0, rw_rk_0, rw_lnx_w_0, rw_lnx_b_0, wout_0, peer_q_0, peer_keys_0, peer_u_0, peer_v_0, ada_w_1, ada_b_1, norm1_1, norm2_1, win_1, conv_w_1, conv_b_1, dt_bias_1, a_log_1, d_skip_1, gnorm_1, wout_1, peer_q_1, peer_keys_1, peer_u_1, peer_v_1, norm_f):
    rows = x.shape[1] // GRID_W
    rope = axial_rope_tables(rows)
    layer_params = (
        (ada_w_0, ada_b_0, norm1_0, norm2_0, win_0, da_lambda_0, da_subln_0, rw_mu_0, rw_w0_0, rw_w2_0, rw_a0_0, rw_a2_0, rw_g2_0, rw_kk_0, rw_ka_0, rw_rk_0, rw_lnx_w_0, rw_lnx_b_0, wout_0, peer_q_0, peer_keys_0, peer_u_0, peer_v_0),
        (ada_w_1, ada_b_1, norm1_1, norm2_1, win_1, conv_w_1, conv_b_1, dt_bias_1, a_log_1, d_skip_1, gnorm_1, wout_1, peer_q_1, peer_keys_1, peer_u_1, peer_v_1),
    )
    h_lat, h_ctx = x, ctx
    for layer in range(DEPTH):
        last = layer == DEPTH - 1
        if layer % 2 == 0:
            lam_init = 0.8 - 0.6 * math.exp(-0.3 * layer)
            h_lat, h_ctx = even_layer(h_lat, h_ctx, c, c_ctx, *layer_params[layer], lam_init=lam_init, rope=rope, last=last)
        else:
            h_lat, h_ctx = odd_layer(h_lat, h_ctx, c, c_ctx, *layer_params[layer], last=last)
    return rmsnorm(h_lat, norm_f)
```

```python
import functools
import math

import jax
import jax.numpy as jnp
import numpy as np
from jax import lax
from jax.experimental import pallas as pl
from jax.experimental.pallas import tpu as pltpu

F32 = jnp.float32
BF16 = jnp.bfloat16
I32 = jnp.int32
HI = lax.Precision.HIGHEST

EPS = 1e-6
N_MOD = 6
GRID_W = 64
LANES = 128
VMEM_LIMIT_BYTES = 48 * 1024 * 1024

DA_HEAD_DIM = 64
DA_V_DIM = 128
ROPE_BASE = 10000.0
ROPE_NFREQ = DA_HEAD_DIM // 4
RW_HEAD_DIM = 64
RW_GN_EPS = 64e-5
RW_CHUNK = 64
M_HEAD_DIM = 64
M_STATE = 128
M_CHUNK = 128
PEER_HEADS = 8
PEER_NKEYS = 128
PEER_TOPK = 16
PEER_TOK = 8


def _cparams(*sem):
    return pltpu.CompilerParams(dimension_semantics=sem, vmem_limit_bytes=VMEM_LIMIT_BYTES)


def _nt(a, b, precision=None):
    return lax.dot_general(a, b, (((1,), (1,)), ((), ())), preferred_element_type=F32, precision=precision)


def _full(shape):
    nd = len(shape)
    return pl.BlockSpec(shape, lambda *_: (0,) * nd)


def _split_bf16(x):
    hi = x.astype(BF16)
    lo = (x - hi.astype(F32)).astype(BF16)
    return hi, lo


def _ada_kernel(c_ref, w_ref, b_ref, o_ref):
    c = c_ref[...]
    s = c * jax.nn.sigmoid(c)
    o_ref[...] = jnp.dot(s, w_ref[...], preferred_element_type=F32, precision=HI) + b_ref[...]


def _ada_mod(cvecs, w, b):
    r, d = cvecs.shape
    n = w.shape[1]
    tn = 1024
    m = pl.pallas_call(
        _ada_kernel,
        grid=(n // tn,),
        in_specs=[_full((r, d)), pl.BlockSpec((d, tn), lambda j: (0, j)), pl.BlockSpec((1, tn), lambda j: (0, j))],
        out_specs=pl.BlockSpec((r, tn), lambda j: (0, j)),
        out_shape=jax.ShapeDtypeStruct((r, n), F32),
        compiler_params=_cparams("parallel"),
        name="ada_mod",
    )(cvecs, w, b.reshape(1, n))
    return [m[:, k * d:(k + 1) * d].reshape(r, 1, d) for k in range(N_MOD)]


def _normmod_kernel(x_ref, g_ref, sh_ref, sc_ref, *refs, n_w, want_h):
    x = x_ref[...]
    y = x * lax.rsqrt(jnp.mean(x * x, axis=-1, keepdims=True) + EPS) * g_ref[...]
    h = y * (1.0 + sc_ref[0]) + sh_ref[0]
    hb = h.astype(BF16)
    for w_ref, o_ref in zip(refs[:n_w], refs[n_w:2 * n_w]):
        o_ref[...] = jnp.dot(hb, w_ref[...], preferred_element_type=F32).astype(o_ref.dtype)
    if want_h:
        refs[2 * n_w][...] = h


def _normmod_proj(x, g, shift, scale, ws, out_dtypes, bid, tm, want_h=False):
    t, d = x.shape
    n_w = len(ws)
    in_specs = [pl.BlockSpec((tm, d), lambda i: (i, 0)), _full((1, d)),
                pl.BlockSpec((1, 1, d), lambda i: (bid(i), 0, 0)),
                pl.BlockSpec((1, 1, d), lambda i: (bid(i), 0, 0))]
    in_specs += [_full(w.shape) for w in ws]
    out_specs = [pl.BlockSpec((tm, w.shape[1]), lambda i: (i, 0)) for w in ws]
    out_shape = [jax.ShapeDtypeStruct((t, w.shape[1]), dt) for w, dt in zip(ws, out_dtypes)]
    if want_h:
        out_specs.append(pl.BlockSpec((tm, d), lambda i: (i, 0)))
        out_shape.append(jax.ShapeDtypeStruct((t, d), F32))
    return pl.pallas_call(
        functools.partial(_normmod_kernel, n_w=n_w, want_h=want_h),
        grid=(t // tm,), in_specs=in_specs, out_specs=out_specs, out_shape=out_shape,
        compiler_params=_cparams("parallel"), name="normmod_proj",
    )(x, g.reshape(1, d), shift, scale, *ws)


def _proj_res_kernel(*refs, n_a):
    a_refs = refs[:n_a]
    w_refs = refs[n_a:2 * n_a]
    res_ref, gate_ref, o_ref = refs[2 * n_a:]
    acc = jnp.dot(a_refs[0][...], w_refs[0][...], preferred_element_type=F32)
    for a_ref, w_ref in zip(a_refs[1:], w_refs[1:]):
        acc += jnp.dot(a_ref[...], w_ref[...], preferred_element_type=F32)
    o_ref[...] = res_ref[...] + gate_ref[0] * acc


def _proj_residual(a_list, w_list, res, gate, bid, tm):
    t, n = res.shape
    n_a = len(a_list)
    in_specs = [pl.BlockSpec((tm, a.shape[1]), lambda i: (i, 0)) for a in a_list]
    in_specs += [_full(w.shape) for w in w_list]
    in_specs += [pl.BlockSpec((tm, n), lambda i: (i, 0)), pl.BlockSpec((1, 1, n), lambda i: (bid(i), 0, 0))]
    return pl.pallas_call(
        functools.partial(_proj_res_kernel, n_a=n_a),
        grid=(t // tm,), in_specs=in_specs, out_specs=pl.BlockSpec((tm, n), lambda i: (i, 0)),
        out_shape=jax.ShapeDtypeStruct((t, n), F32),
        compiler_params=_cparams("parallel"), name="proj_residual",
    )(*a_list, *w_list, res, gate)


def _rope_kernel(q_ref, k_ref, c_ref, s_ref, qo_ref, ko_ref):
    c = c_ref[...]
    s = s_ref[...]
    lane = lax.broadcasted_iota(I32, c.shape, 1)
    first = (lane % 32) < 16
    width = q_ref.shape[1]

    def rot(x):
        partner = jnp.where(first, pltpu.roll(x, LANES - 16, 1), pltpu.roll(x, 16, 1))
        return x * c + partner * s

    for g in range(width // LANES):
        sl = slice(g * LANES, (g + 1) * LANES)
        qo_ref[:, sl] = (rot(q_ref[:, sl]) * (DA_HEAD_DIM ** -0.5)).astype(qo_ref.dtype)
        ko_ref[:, sl] = rot(k_ref[:, sl]).astype(ko_ref.dtype)


def _rope_tables(seq_len, tm):
    rows = seq_len // GRID_W
    row = jnp.repeat(jnp.arange(rows, dtype=F32), GRID_W)
    col = (jnp.arange(seq_len) % GRID_W).astype(F32)
    inv = ROPE_BASE ** (-jnp.arange(ROPE_NFREQ, dtype=F32) / ROPE_NFREQ)
    ang_r = row[:, None] * inv
    ang_c = col[:, None] * inv
    cos64 = jnp.concatenate([jnp.cos(ang_r), jnp.cos(ang_r), jnp.cos(ang_c), jnp.cos(ang_c)], axis=1)
    sin64 = jnp.concatenate([-jnp.sin(ang_r), jnp.sin(ang_r), -jnp.sin(ang_c), jnp.sin(ang_c)], axis=1)
    cos = jnp.concatenate([jnp.tile(cos64, (1, 2)), jnp.ones((tm, LANES), F32)], axis=0)
    sin = jnp.concatenate([jnp.tile(sin64, (1, 2)), jnp.zeros((tm, LANES), F32)], axis=0)
    return cos, sin


def _rope(q, k, cos, sin, tab_block, tm):
    t, w = q.shape
    row = pl.BlockSpec((tm, w), lambda i: (i, 0))
    tab = pl.BlockSpec((tm, LANES), lambda i: (tab_block(i), 0))
    return pl.pallas_call(
        _rope_kernel, grid=(t // tm,), in_specs=[row, row, tab, tab], out_specs=[row, row],
        out_shape=[jax.ShapeDtypeStruct((t, w), BF16)] * 2,
        compiler_params=_cparams("parallel"), name="rope",
    )(q, k, cos, sin)


def _attn_kernel(lam_ref, sub_ref, q_ref, k_ref, v_ref, o_ref, *, lam_init):
    lp = lam_ref[...]
    lam = (jnp.exp(jnp.sum(lp[0:1] * lp[1:2], keepdims=True))
           - jnp.exp(jnp.sum(lp[2:3] * lp[3:4], keepdims=True)) + lam_init)
    q = q_ref[...]
    k = k_ref[...]
    v = v_ref[...]
    lane = lax.broadcasted_iota(I32, q.shape, 1)
    outs = []
    for m in range(2):
        sel = (lane < DA_HEAD_DIM) if m == 0 else (lane >= DA_HEAD_DIM)
        s = _nt(jnp.where(sel, q, jnp.zeros_like(q)), k)
        p = jnp.exp(s - jnp.max(s, axis=-1, keepdims=True))
        denom = jnp.sum(p, axis=-1, keepdims=True)
        outs.append(jnp.dot(p.astype(BF16), v, preferred_element_type=F32) / denom)
    o = outs[0] - lam * outs[1]
    o = o * lax.rsqrt(jnp.mean(o * o, axis=-1, keepdims=True) + EPS) * sub_ref[...] * (1.0 - lam_init)
    o_ref[...] = o.astype(o_ref.dtype)


def _diff_attention(q, k, v, lamp, subln, lam_init, nb, lq, lk, q_row0, tq):
    w = q.shape[1]
    heads = w // DA_V_DIM
    nq = lq // tq
    qb0 = q_row0 // tq
    return pl.pallas_call(
        functools.partial(_attn_kernel, lam_init=lam_init),
        grid=(nb, heads, nq),
        in_specs=[_full(lamp.shape), _full((1, DA_V_DIM)),
                  pl.BlockSpec((tq, DA_V_DIM), lambda b, h, i: (qb0 + b * nq + i, h)),
                  pl.BlockSpec((lk, DA_V_DIM), lambda b, h, i: (b, h)),
                  pl.BlockSpec((lk, DA_V_DIM), lambda b, h, i: (b, h))],
        out_specs=pl.BlockSpec((tq, DA_V_DIM), lambda b, h, i: (b * nq + i, h)),
        out_shape=jax.ShapeDtypeStruct((nb * lq, w), BF16),
        compiler_params=_cparams("parallel", "parallel", "arbitrary"), name="diff_attention",
    )(lamp, subln.reshape(1, DA_V_DIM), q, k, v)


def _softplus(z):
    return jnp.maximum(z, 0.0) + jnp.log(1.0 + jnp.exp(-jnp.abs(z)))


def _rw_prep_kernel(u_ref, prev_ref, next_ref, mu_ref, w0_ref, w2_ref, a0_ref, a2_ref, g2_ref, kk_ref, ka_ref,
                    rk_ref, ones_ref, r_ref, v_ref, nkk_ref, g_ref, bonus_ref, lw_ref, kd_ref, bd_ref):
    u = u_ref[...]
    tm = u.shape[0]
    width = r_ref.shape[1]
    row = lax.broadcasted_iota(I32, u.shape, 0)
    up = jnp.where(row == 0, prev_ref[0], pltpu.roll(u, 1, 0))
    dn = jnp.where(row == tm - 1, next_ref[0], pltpu.roll(u, tm - 1, 0))
    u = u + mu_ref[...] * (0.5 * (up + dn) - u)
    r = u[:, :width]
    k = u[:, width:2 * width]
    v = u[:, 2 * width:3 * width]
    o = 3 * width
    w_in = u[:, o:o + LANES]
    a_in = u[:, o + LANES:o + 2 * LANES]
    g_in = u[:, o + 2 * LANES:o + 3 * LANES]
    ones = ones_ref[...]
    hsum = lambda t: jnp.dot(t, ones, preferred_element_type=F32, precision=HI)
    g = jnp.dot(jax.nn.sigmoid(g_in), g2_ref[...], preferred_element_type=F32, precision=HI)
    kk = k * kk_ref[...]
    kk = kk / jnp.maximum(jnp.sqrt(hsum(kk * kk)), 1e-12)
    w_log = -_softplus(-(w0_ref[...] + jnp.dot(jnp.tanh(w_in), w2_ref[...], preferred_element_type=F32,
                                               precision=HI))) - 0.5
    logw = -jnp.exp(w_log)
    a = jax.nn.sigmoid(a0_ref[...] + jnp.dot(a_in, a2_ref[...], preferred_element_type=F32, precision=HI))
    ksum = jnp.zeros_like(k)
    for d in range(2):
        a_d = a[:, d * width:(d + 1) * width]
        k_d = k * (1.0 + (a_d - 1.0) * ka_ref[...])
        ksum = ksum + k_d
        lw_ref[d] = logw[:, d * width:(d + 1) * width]
        kd_ref[d] = k_d
        bd_ref[d] = kk * a_d
    r_ref[...] = r
    v_ref[...] = v
    nkk_ref[...] = -kk
    g_ref[...] = g
    bonus_ref[...] = hsum(r * ksum * rk_ref[...]) * v


def _halo_rows(x, tm, seq_lens):
    t = x.shape[0]
    nt = t // tm
    starts = np.cumsum([0] + [n for n in seq_lens])[:-1]
    ends = np.cumsum(seq_lens)
    tile_start = np.arange(nt) * tm
    has_prev = ~np.isin(tile_start, starts)
    has_next = ~np.isin(tile_start + tm, ends)
    last = x[tm - 1::tm]
    first = x[0::tm]
    zero = jnp.zeros_like(first[:1])
    prev = jnp.concatenate([zero, last[:-1]], axis=0) * jnp.asarray(has_prev, x.dtype)[:, None]
    nxt = jnp.concatenate([first[1:], zero], axis=0) * jnp.asarray(has_next, x.dtype)[:, None]
    return prev[:, None, :], nxt[:, None, :]


def _block_diag2(m):
    z = jnp.zeros_like(m[0])
    return jnp.concatenate([jnp.concatenate([m[0], z], axis=1), jnp.concatenate([z, m[1]], axis=1)], axis=0)


def _head_ones(width, hd):
    idx = np.arange(width) // hd
    return jnp.asarray((idx[:, None] == idx[None, :]).astype(np.float32))


def _rwkv_prepare(u, seq_lens, tm, mu, w0, w2, a0, a2, g2, k_k, k_a, r_k):
    t, cols = u.shape
    width = k_k.shape[0]
    prev, nxt = _halo_rows(u, tm, seq_lens)
    row = lambda c: pl.BlockSpec((tm, c), lambda i: (i, 0))
    halo = pl.BlockSpec((1, 1, cols), lambda i: (i, 0, 0))
    dir_out = pl.BlockSpec((2, tm, width), lambda i: (0, i, 0))
    consts = [mu.reshape(1, cols), w0.reshape(1, 2 * width), _block_diag2(w2), a0.reshape(1, 2 * width),
              _block_diag2(a2), g2, k_k.reshape(1, width), k_a.reshape(1, width), r_k.reshape(1, width),
              _head_ones(width, RW_HEAD_DIM)]
    f = jax.ShapeDtypeStruct((t, width), F32)
    f2 = jax.ShapeDtypeStruct((2, t, width), F32)
    return pl.pallas_call(
        _rw_prep_kernel, grid=(t // tm,),
        in_specs=[row(cols), halo, halo] + [_full(c.shape) for c in consts],
        out_specs=[row(width)] * 5 + [dir_out] * 3,
        out_shape=[f] * 5 + [f2] * 3,
        compiler_params=_cparams("parallel"), name="rwkv_prepare",
    )(u, prev, nxt, *consts)


def _rw_scan_kernel(r_ref, v_ref, nkk_ref, lw_ref, kd_ref, bd_ref, h0_ref, y_ref, hf_ref, h_scr):
    d = pl.program_id(2)
    c = pl.program_id(3)
    nc = pl.num_programs(3)
    cs = RW_CHUNK
    sgn = 1 - 2 * d

    @pl.when(c == 0)
    def _():
        h_scr[...] = h0_ref[0, 0, 0]

    logw = lw_ref[0]
    r = r_ref[...]
    v = v_ref[...]
    a = nkk_ref[...]
    k = kd_ref[0]
    b = bd_ref[0]

    ri = lax.broadcasted_iota(I32, (cs, cs), 0)
    ci = lax.broadcasted_iota(I32, (cs, cs), 1)
    before_eq = jnp.where((ri - ci) * sgn >= 0, 1.0, 0.0)
    cum = jnp.dot(before_eq, logw, preferred_element_type=F32, precision=HI)
    mid = cum[cs // 2:cs // 2 + 1]
    tot = jnp.sum(logw, axis=0, keepdims=True)
    e_in = jnp.exp(mid - cum)
    e_end = jnp.exp(tot - cum)

    lane = lax.broadcasted_iota(I32, (cs, LANES), 1)
    lo_half = lane < RW_HEAD_DIM

    def stack2(x):
        return jnp.concatenate([jnp.where(lo_half, x, 0.0), jnp.where(lo_half, 0.0, x)], axis=0)

    a2 = stack2(a * jnp.exp(cum - logw - mid))
    r2 = stack2(r * jnp.exp(cum - mid))
    a2_abs = stack2(a * jnp.exp(cum - logw))
    r2_abs = stack2(r * jnp.exp(cum))
    b2 = stack2(b * e_in)
    k2 = stack2(k * e_in)
    v2 = stack2(v)
    bh2 = stack2(b * e_end)
    kh2 = stack2(k * e_end)

    n2 = 2 * cs
    rt = lax.broadcasted_iota(I32, (n2, n2), 0)
    ct = lax.broadcasted_iota(I32, (n2, n2), 1)
    dtok = ((rt & (cs - 1)) - (ct & (cs - 1))) * sgn
    strict = dtok > 0
    incl = dtok >= 0
    eye = rt == ct
    mm = lambda x, y: jnp.dot(x, y, preferred_element_type=F32, precision=HI)

    nmat = jnp.where(strict, _nt(a2, b2, HI), 0.0)
    mmat = jnp.where(strict, _nt(a2, k2, HI), 0.0)
    qb = jnp.where(incl, _nt(r2, b2, HI), 0.0)
    qk = jnp.where(incl, _nt(r2, k2, HI), 0.0)

    tinv = jnp.where(eye, 1.0, 0.0) + nmat
    pw = nmat
    for _ in range(int(math.log2(cs)) - 1):
        pw = mm(pw, pw)
        tinv = tinv + mm(tinv, pw)

    w2 = mm(tinv, mm(mmat, v2))
    a2p = mm(tinv, a2_abs)
    y_intra = mm(qk, v2) + mm(qb, w2)
    r2p = r2_abs + mm(qb, a2p)
    bh2t = bh2.T
    gmat = jnp.where(eye, jnp.exp(tot), 0.0) + mm(bh2t, a2p)
    dmat = mm(bh2t, w2) + mm(kh2.T, v2)

    h = h_scr[...]
    y2 = y_intra + mm(r2p, h)
    y_ref[0] = y2[:cs] + y2[cs:]
    h_new = mm(gmat, h) + dmat
    h_scr[...] = h_new

    @pl.when(c == nc - 1)
    def _():
        hf_ref[0, 0, 0] = h_new


def _rwkv_scan(r, v, nkk, lw, kd, bd, h0, nb, seq_len, row0):
    w = r.shape[1]
    pairs = w // LANES
    nc = seq_len // RW_CHUNK
    rb0 = row0 // RW_CHUNK

    def chunk(c, d):
        return c + d * (nc - 1 - 2 * c)

    shared = pl.BlockSpec((RW_CHUNK, LANES), lambda b, p, d, c: (rb0 + b * nc + chunk(c, d), p))
    perdir = pl.BlockSpec((1, RW_CHUNK, LANES), lambda b, p, d, c: (d, rb0 + b * nc + chunk(c, d), p))
    state = pl.BlockSpec((1, 1, 1, LANES, LANES), lambda b, p, d, c: (d, b, p, 0, 0))
    return pl.pallas_call(
        _rw_scan_kernel, grid=(nb, pairs, 2, nc),
        in_specs=[shared, shared, shared, perdir, perdir, perdir, state],
        out_specs=[pl.BlockSpec((1, RW_CHUNK, LANES), lambda b, p, d, c: (d, b * nc + chunk(c, d), p)), state],
        out_shape=[jax.ShapeDtypeStruct((2, nb * seq_len, w), F32), jax.ShapeDtypeStruct(h0.shape, F32)],
        scratch_shapes=[pltpu.VMEM((LANES, LANES), F32)],
        compiler_params=_cparams("parallel", "parallel", "parallel", "arbitrary"), name="rwkv_scan",
    )(r, v, nkk, lw, kd, bd, h0)


def _rw_post_kernel(y_ref, g_ref, bonus_ref, lnw_ref, lnb_ref, ones_ref, o_ref):
    y = y_ref[0] + y_ref[1]
    ones = ones_ref[...]
    hmean = lambda t: jnp.dot(t, ones, preferred_element_type=F32, precision=HI) * (1.0 / RW_HEAD_DIM)
    yc = y - hmean(y)
    var = hmean(yc * yc)
    yn = yc * lax.rsqrt(var + RW_GN_EPS) * lnw_ref[...] + lnb_ref[...]
    o_ref[...] = ((yn + bonus_ref[...]) * g_ref[...]).astype(o_ref.dtype)


def _rwkv_post(y, g, bonus, lnw, lnb, tm):
    _, t, w = y.shape
    row = pl.BlockSpec((tm, w), lambda i: (i, 0))
    return pl.pallas_call(
        _rw_post_kernel, grid=(t // tm,),
        in_specs=[pl.BlockSpec((2, tm, w), lambda i: (0, i, 0)), row, row, _full((1, w)), _full((1, w)),
                  _full((w, w))],
        out_specs=row, out_shape=jax.ShapeDtypeStruct((t, w), BF16),
        compiler_params=_cparams("parallel"), name="rwkv_post",
    )(y, g, bonus, lnw.reshape(1, w), lnb.reshape(1, w), _head_ones(w, RW_HEAD_DIM))


def _extract_topk(s, payload, count):
    n = s.shape[0]
    pos = lax.broadcasted_iota(I32, s.shape, 0).astype(F32)
    vals, pays = [], []
    for _ in range(count):
        m = jnp.max(s, axis=0, keepdims=True)
        first = jnp.min(jnp.where(s == m, pos, float(n)), axis=0, keepdims=True)
        hit = pos == first
        vals.append(m)
        pays.append(jnp.sum(jnp.where(hit, payload, 0.0), axis=0, keepdims=True))
        s = jnp.where(hit, -jnp.inf, s)
    return jnp.concatenate(vals, axis=0), jnp.concatenate(pays, axis=0)


def _peer_topk_kernel(q_ref, keys_ref, idx_ref, gate_ref):
    tt = q_ref.shape[0]
    kpos = lax.broadcasted_iota(I32, (PEER_NKEYS, tt), 0).astype(F32)

    def head(h, carry):
        vs, ids = [], []
        for p in range(2):
            col = pl.multiple_of((2 * h + p) * LANES, LANES)
            s = _nt(keys_ref[h, p], q_ref[:, pl.ds(col, LANES)], HI)
            v_p, i_p = _extract_topk(s, kpos, PEER_TOPK)
            vs.append(v_p)
            ids.append(i_p)
        cand = jnp.concatenate([vs[0][i:i + 1] + vs[1] for i in range(PEER_TOPK)], axis=0)
        cidx = jnp.concatenate([ids[0][i:i + 1] * float(PEER_NKEYS) + ids[1] for i in range(PEER_TOPK)], axis=0)
        top_s, top_i = _extract_topk(cand, cidx, PEER_TOPK)
        e = jnp.exp(top_s - top_s[0:1])
        rows = pl.ds(pl.multiple_of(h * PEER_TOPK, PEER_TOPK), PEER_TOPK)
        gate_ref[rows, :] = e / jnp.sum(e, axis=0, keepdims=True)
        idx_ref[rows, :] = top_i.astype(I32)
        return carry

    lax.fori_loop(0, PEER_HEADS, head, 0)


def _peer_topk(q, keys, tt):
    t = q.shape[0]
    ne = PEER_HEADS * PEER_TOPK
    out = pl.BlockSpec((ne, tt), lambda i: (0, i))
    return pl.pallas_call(
        _peer_topk_kernel, grid=(t // tt,),
        in_specs=[pl.BlockSpec((tt, q.shape[1]), lambda i: (i, 0)), _full(keys.shape)],
        out_specs=[out, out],
        out_shape=[jax.ShapeDtypeStruct((ne, t), I32), jax.ShapeDtypeStruct((ne, t), F32)],
        compiler_params=_cparams("parallel"), name="peer_topk",
    )(q, keys)


def _peer_gather_kernel(idx_ref, idxn_ref, h_ref, gates_ref, x_ref, mod_ref, nf_ref, uv_ref, o_ref, buf, sem,
                        *, final_norm):
    i = pl.program_id(0)
    n = pl.num_programs(0)
    tt, ne = gates_ref.shape
    d = h_ref.shape[1]
    rows = tt * ne
    slot = i % 2

    def issue(ids_ref, s):
        def body(j, carry):
            e = ids_ref[j // ne, j % ne]
            pltpu.make_async_copy(uv_ref.at[pl.ds(e, 1)], buf.at[s, pl.ds(j, 1)], sem.at[s]).start()
            return carry
        lax.fori_loop(0, rows, body, 0, unroll=8)

    @pl.when(i == 0)
    def _():
        issue(idx_ref, 0)

    @pl.when(i + 1 < n)
    def _():
        issue(idxn_ref, 1 - slot)

    pltpu.make_async_copy(uv_ref.at[pl.ds(0, rows)], buf.at[slot], sem.at[slot]).wait()

    zeros_h = jnp.zeros((6, d), BF16)
    zeros_e = jnp.zeros((6, ne), BF16)
    outs = []
    for t in range(tt):
        rows_t = buf[slot, t * ne:(t + 1) * ne, :]
        u_rows = rows_t[:, :d].astype(BF16)
        v_rows = rows_t[:, d:].astype(BF16)
        hi, lo = _split_bf16(h_ref[t:t + 1, :])
        pre = _nt(jnp.concatenate([hi, lo, zeros_h], axis=0), u_rows)
        pre = pre[0:1] + pre[1:2]
        act = 0.5 * pre * (1.0 + lax.erf(pre * (2.0 ** -0.5)))
        whi, wlo = _split_bf16(gates_ref[t:t + 1, :] * act)
        o = jnp.dot(jnp.concatenate([whi, wlo, zeros_e], axis=0), v_rows, preferred_element_type=F32)
        outs.append(o[0:1] + o[1:2])
    y = x_ref[...] + mod_ref[0] * jnp.concatenate(outs, axis=0)
    if final_norm:
        y = y * lax.rsqrt(jnp.mean(y * y, axis=-1, keepdims=True) + EPS) * nf_ref[...]
    o_ref[...] = y


def _peer_gather(idx, gates, h, x, gate_mod, norm_f, uv, bid, final_norm):
    t, d = x.shape
    ne = idx.shape[1]
    tt = PEER_TOK
    n = t // tt
    row = lambda c: pl.BlockSpec((tt, c), lambda i: (i, 0))
    return pl.pallas_call(
        functools.partial(_peer_gather_kernel, final_norm=final_norm),
        grid=(n,),
        in_specs=[pl.BlockSpec((tt, ne), lambda i: (i, 0), memory_space=pltpu.SMEM),
                  pl.BlockSpec((tt, ne), lambda i: (jnp.minimum(i + 1, n - 1), 0), memory_space=pltpu.SMEM),
                  row(d), row(ne), row(d), pl.BlockSpec((1, 1, d), lambda i: (bid(i), 0, 0)), _full((1, d)),
                  pl.BlockSpec(memory_space=pl.ANY)],
        out_specs=row(d), out_shape=jax.ShapeDtypeStruct((t, d), F32),
        scratch_shapes=[pltpu.VMEM((2, tt * ne, 2 * d), F32), pltpu.SemaphoreType.DMA((2,))],
        compiler_params=_cparams("arbitrary"), name="peer_gather",
    )(idx, idx, h, gates, x, gate_mod, norm_f.reshape(1, d), uv)


def _peer(x, norm2, shift, scale, gate_mod, wq, keys, uv, norm_f, bid_fn, tm, final_norm):
    t = x.shape[0]
    q, h = _normmod_proj(x, norm2, shift, scale, [wq], [F32], bid_fn(tm), tm, want_h=True)
    idx_t, gates_t = _peer_topk(q, keys, LANES)
    return _peer_gather(idx_t.T, gates_t.T, h, x, gate_mod, norm_f, uv, bid_fn(PEER_TOK), final_norm)


def _conv_kernel(x_ref, prev_ref, next_ref, w_ref, b_ref, dtr_ref, dtb_ref, o_ref, dt_ref, *, heads):
    x = x_ref[...]
    tm = x.shape[0]
    row = lax.broadcasted_iota(I32, x.shape, 0)
    up = jnp.where(row == 0, prev_ref[0], pltpu.roll(x, 1, 0))
    dn = jnp.where(row == tm - 1, next_ref[0], pltpu.roll(x, tm - 1, 0))
    y = up * w_ref[0:1] + x * w_ref[1:2] + dn * w_ref[2:3] + b_ref[...]
    o_ref[...] = y * jax.nn.sigmoid(y)
    lane = lax.broadcasted_iota(I32, (tm, LANES), 1)
    for d in range(2):
        dt_ref[d] = jnp.where(lane < heads, _softplus(dtr_ref[d] + dtb_ref[d]), 0.0)


def _mamba_conv(xbc, dt_raw, seq_lens, tm, conv_w, conv_b, dt_bias_pad, heads):
    t, c = xbc.shape
    prev, nxt = _halo_rows(xbc, tm, seq_lens)
    row = pl.BlockSpec((tm, c), lambda i: (i, 0))
    halo = pl.BlockSpec((1, 1, c), lambda i: (i, 0, 0))
    dts = pl.BlockSpec((2, tm, LANES), lambda i: (0, i, 0))
    return pl.pallas_call(
        functools.partial(_conv_kernel, heads=heads), grid=(t // tm,),
        in_specs=[row, halo, halo, _full(conv_w.shape), _full((1, c)), dts, _full((2, 1, LANES))],
        out_specs=[row, dts],
        out_shape=[jax.ShapeDtypeStruct((t, c), F32), jax.ShapeDtypeStruct((2, t, LANES), F32)],
        compiler_params=_cparams("parallel"), name="mamba_conv",
    )(xbc, prev, nxt, conv_w, conv_b.reshape(1, c), dt_raw, dt_bias_pad)


def _ssd_kernel(*refs, reverse, inner, groups, add_prev):
    if add_prev:
        xbc_ref, dt_ref, dtt_ref, alr_ref, alc_ref, rep_ref, h0_ref, yin_ref, y_ref, hf_ref, h_scr = refs
    else:
        xbc_ref, dt_ref, dtt_ref, alr_ref, alc_ref, rep_ref, h0_ref, y_ref, hf_ref, h_scr = refs
        yin_ref = None
    c = pl.program_id(1)
    nc = pl.num_programs(1)
    cs = M_CHUNK
    gw = inner // groups
    hpg = gw // M_HEAD_DIM

    @pl.when(c == 0)
    def _():
        h_scr[...] = h0_ref[0]

    dt = dt_ref[0]
    dtt = dtt_ref[0]
    a = dt * (-jnp.exp(alr_ref[...]))
    at = dtt * (-jnp.exp(alc_ref[...]))
    ri = lax.broadcasted_iota(I32, (cs, cs), 0)
    ci = lax.broadcasted_iota(I32, (cs, cs), 1)
    incl = (ri <= ci) if reverse else (ri >= ci)
    tri = jnp.where(incl, 1.0, 0.0)
    cum = jnp.dot(tri, a, preferred_element_type=F32, precision=HI)
    cumt = _nt(at, tri, HI)
    tot = jnp.sum(a, axis=0, keepdims=True)

    rep = rep_ref[...]

    def spread(t):
        hi, lo = _split_bf16(t)
        return jnp.dot(hi, rep, preferred_element_type=F32) + jnp.dot(lo, rep, preferred_element_type=F32)

    e_cum = spread(jnp.exp(cum))
    e_end = spread(jnp.exp(tot - cum) * dt)
    e_tot = spread(jnp.broadcast_to(jnp.exp(tot), (8, LANES)))[0:1]

    lane = lax.broadcasted_iota(I32, (cs, LANES), 1)
    lo_half = lane < M_HEAD_DIM
    ys = []
    for g in range(groups):
        bg32 = xbc_ref[:, inner + g * M_STATE:inner + (g + 1) * M_STATE]
        bg = bg32.astype(BF16)
        cg = xbc_ref[:, inner + groups * M_STATE + g * M_STATE:inner + groups * M_STATE + (g + 1) * M_STATE]
        cg = cg.astype(BF16)
        cb = _nt(cg, bg)
        hprev = h_scr[g]
        xg = xbc_ref[:, g * gw:(g + 1) * gw]
        y_off = jnp.dot(cg, hprev.astype(BF16), preferred_element_type=F32) * e_cum[:, g * gw:(g + 1) * gw]
        xd = (xg * e_end[:, g * gw:(g + 1) * gw]).astype(BF16)
        h_scr[g] = e_tot[:, g * gw:(g + 1) * gw] * hprev + jnp.dot(bg32.T.astype(BF16), xd, preferred_element_type=F32)
        for j in range(hpg // 2):
            xpair = xg[:, j * LANES:(j + 1) * LANES].astype(BF16)
            halves = []
            for hh in range(2):
                h = g * hpg + 2 * j + hh
                seg = jnp.minimum(cum[:, h:h + 1] - cumt[h:h + 1, :], 0.0)
                m = jnp.where(incl, cb * jnp.exp(seg), 0.0) * dtt[h:h + 1, :]
                halves.append(jnp.dot(m.astype(BF16), xpair, preferred_element_type=F32))
            ys.append(jnp.where(lo_half, halves[0], halves[1]) + y_off[:, j * LANES:(j + 1) * LANES])
    y = jnp.concatenate(ys, axis=1)
    if add_prev:
        y = y + yin_ref[...]
    y_ref[...] = y

    @pl.when(c == nc - 1)
    def _():
        hf_ref[0] = h_scr[...]


def _ssd_pass(xbc, dt, dtt, a_log, h0, y_prev, nb, seq_len, row0, reverse, inner, groups, heads):
    nc = seq_len // M_CHUNK
    rb0 = row0 // M_CHUNK
    c_all = xbc.shape[1]
    gw = inner // groups
    alr = jnp.zeros((1, LANES), F32).at[0, :heads].set(a_log)
    alc = jnp.broadcast_to(jnp.zeros((LANES,), F32).at[:heads].set(a_log)[:, None], (LANES, LANES))
    hid = np.arange(inner) // M_HEAD_DIM
    rep = jnp.asarray((np.arange(LANES)[:, None] == hid[None, :]).astype(np.float32), BF16)
    chunk = (lambda c: nc - 1 - c) if reverse else (lambda c: c)
    add_prev = y_prev is not None
    in_specs = [pl.BlockSpec((M_CHUNK, c_all), lambda b, c: (rb0 + b * nc + chunk(c), 0)),
                pl.BlockSpec((1, M_CHUNK, LANES), lambda b, c: (0, rb0 + b * nc + chunk(c), 0)),
                pl.BlockSpec((1, LANES, M_CHUNK), lambda b, c: (0, 0, rb0 + b * nc + chunk(c))),
                _full((1, LANES)), _full((LANES, LANES)), _full((LANES, inner)),
                pl.BlockSpec((1, groups, M_STATE, gw), lambda b, c: (b, 0, 0, 0))]
    args = [xbc, dt, dtt, alr, alc, rep, h0]
    yspec = pl.BlockSpec((M_CHUNK, inner), lambda b, c: (b * nc + chunk(c), 0))
    if add_prev:
        in_specs.append(yspec)
        args.append(y_prev)
    return pl.pallas_call(
        functools.partial(_ssd_kernel, reverse=reverse, inner=inner, groups=groups, add_prev=add_prev),
        grid=(nb, nc), in_specs=in_specs,
        out_specs=[yspec, pl.BlockSpec((1, groups, M_STATE, gw), lambda b, c: (b, 0, 0, 0))],
        out_shape=[jax.ShapeDtypeStruct((nb * seq_len, inner), F32), jax.ShapeDtypeStruct(h0.shape, F32)],
        scratch_shapes=[pltpu.VMEM((groups, M_STATE, gw), F32)],
        compiler_params=_cparams("parallel", "arbitrary"), name="ssd_pass",
    )(*args)


def _mamba_gate_kernel(y_ref, x_ref, z_ref, dsk_ref, gn_ref, o_ref, *, groups):
    z = z_ref[...]
    y = (y_ref[...] + dsk_ref[...] * x_ref[...]) * (z * jax.nn.sigmoid(z))
    gw = y.shape[1] // groups
    for g in range(groups):
        yg = y[:, g * gw:(g + 1) * gw]
        yg = yg * lax.rsqrt(jnp.mean(yg * yg, axis=-1, keepdims=True) + EPS) * gn_ref[:, g * gw:(g + 1) * gw]
        o_ref[:, g * gw:(g + 1) * gw] = yg.astype(o_ref.dtype)


def _mamba_gate(y, xbc, z, d_skip_cols, gnorm, groups, tm):
    t, inner = y.shape
    row = pl.BlockSpec((tm, inner), lambda i: (i, 0))
    return pl.pallas_call(
        functools.partial(_mamba_gate_kernel, groups=groups), grid=(t // tm,),
        in_specs=[row, row, row, _full((1, inner)), _full((1, inner))],
        out_specs=row, out_shape=jax.ShapeDtypeStruct((t, inner), BF16),
        compiler_params=_cparams("parallel"), name="mamba_gate",
    )(y, xbc, z, d_skip_cols.reshape(1, inner), gnorm.reshape(1, inner))


def _even_layer(xs, mods, nb, seq, ctx_len, tm, bid_fn, norm1, norm2, win, da_lambda, da_subln, rw_mu, rw_w0,
                rw_w2, rw_a0, rw_a2, rw_g2, rw_kk, rw_ka, rw_rk, rw_lnx_w, rw_lnx_b, wout, peer_q, peer_keys,
                peer_uv, lam_init):
    d = xs.shape[1]
    t_lat = nb * seq
    da_w = d // 2
    rw_w = d - da_w
    winb = win.astype(BF16)
    ws = [winb[:, :da_w], winb[:, da_w:2 * da_w], winb[:, 2 * da_w:3 * da_w], winb[:, 3 * da_w:]]
    q, k, v, u = _normmod_proj(xs, norm1, mods[0], mods[1], ws, [F32, F32, BF16, F32], bid_fn(tm), tm)

    cos, sin = _rope_tables(seq, tm)
    lat_tiles = t_lat // tm
    tab_block = lambda i: jnp.where(i < lat_tiles, i % (seq // tm), seq // tm)
    qr, kr = _rope(q, k, cos, sin, tab_block, tm)
    lk = ctx_len + seq
    cat = lambda a: jnp.concatenate([a[t_lat:].reshape(nb, ctx_len, da_w), a[:t_lat].reshape(nb, seq, da_w)],
                                    axis=1).reshape(nb * lk, da_w)
    tq = min(256, seq)
    o_lat = _diff_attention(qr, cat(kr), cat(v), da_lambda, da_subln, lam_init, nb, seq, lk, 0, tq)
    tqc = min(256, ctx_len)
    o_ctx = _diff_attention(qr, kr[t_lat:], v[t_lat:], da_lambda, da_subln, lam_init, nb, ctx_len, ctx_len,
                            t_lat, tqc)
    o_att = jnp.concatenate([o_lat, o_ctx], axis=0)

    seq_lens = [seq] * nb + [ctx_len] * nb
    r, vv, nkk, g, bonus, lw, kd, bd = _rwkv_prepare(u, seq_lens, tm, rw_mu, rw_w0, rw_w2, rw_a0, rw_a2, rw_g2,
                                                     rw_kk, rw_ka, rw_rk)
    zero = jnp.zeros((2, nb, rw_w // LANES, LANES, LANES), F32)
    y_ctx, h_ctx = _rwkv_scan(r, vv, nkk, lw, kd, bd, zero, nb, ctx_len, t_lat)
    y_lat, _ = _rwkv_scan(r, vv, nkk, lw, kd, bd, h_ctx, nb, seq, 0)
    y = jnp.concatenate([y_lat, y_ctx], axis=1)
    o_rw = _rwkv_post(y, g, bonus, rw_lnx_w, rw_lnx_b, tm)

    woutb = wout.astype(BF16)
    xs = _proj_residual([o_att, o_rw], [woutb[:da_w], woutb[da_w:]], xs, mods[2], bid_fn(tm), tm)
    return _peer(xs, norm2, mods[3], mods[4], mods[5], peer_q.astype(BF16), peer_keys, peer_uv, norm2, bid_fn, tm,
                 False)


def _odd_layer_last(xs, mods, nb, seq, ctx_len, tm, bid_fn, norm1, norm2, win, conv_w, conv_b, dt_bias, a_log,
                    d_skip, gnorm, wout, peer_q, peer_keys, peer_uv, norm_f):
    d = xs.shape[1]
    t_lat = nb * seq
    inner = wout.shape[0]
    heads = a_log.shape[1]
    conv_dim = conv_w.shape[1]
    groups = (conv_dim - inner) // (2 * M_STATE)
    winb = win.astype(BF16)
    pad = jnp.zeros((d, LANES - heads), BF16)
    w_dt = [jnp.concatenate([winb[:, inner + conv_dim + k * heads:inner + conv_dim + (k + 1) * heads], pad], axis=1)
            for k in range(2)]
    ws = [winb[:, :inner], winb[:, inner:inner + conv_dim]] + w_dt
    z, xbc_raw, dtr_f, dtr_b = _normmod_proj(xs, norm1, mods[0], mods[1], ws, [F32] * 4, bid_fn(tm), tm)
    seq_lens = [seq] * nb + [ctx_len] * nb
    dtb = jnp.zeros((2, 1, LANES), F32).at[:, 0, :heads].set(dt_bias)
    xbc, dt = _mamba_conv(xbc_raw, jnp.stack([dtr_f, dtr_b]), seq_lens, tm, conv_w, conv_b, dtb, heads)
    dtt = jnp.swapaxes(dt, 1, 2)
    h0 = jnp.zeros((nb, groups, M_STATE, inner // groups), F32)
    ssd = functools.partial(_ssd_pass, xbc, inner=inner, groups=groups, heads=heads)
    _, hf = ssd(dt[0:1], dtt[0:1], a_log[0], h0, None, nb, ctx_len, t_lat, False)
    _, hb = ssd(dt[1:2], dtt[1:2], a_log[1], h0, None, nb, ctx_len, t_lat, True)
    y, _ = ssd(dt[0:1], dtt[0:1], a_log[0], hf, None, nb, seq, 0, False)
    y, _ = ssd(dt[1:2], dtt[1:2], a_log[1], hb, y, nb, seq, 0, True)
    x_lat = xs[:t_lat]
    gated = _mamba_gate(y, xbc, z, jnp.repeat(d_skip, M_HEAD_DIM), gnorm, groups, tm)
    x_lat = _proj_residual([gated], [wout.astype(BF16)], x_lat, mods[2], bid_fn(tm), tm)
    return _peer(x_lat, norm2, mods[3], mods[4], mods[5], peer_q.astype(BF16), peer_keys, peer_uv, norm_f, bid_fn,
                 tm, True)


def kernel(x, c, ctx, c_ctx, ada_w_0, ada_b_0, norm1_0, norm2_0, win_0, da_lambda_0, da_subln_0, rw_mu_0, rw_w0_0, rw_w2_0, rw_a0_0, rw_a2_0, rw_g2_0, rw_kk_0, rw_ka_0, rw_rk_0, rw_lnx_w_0, rw_lnx_b_0, wout_0, peer_q_0, peer_keys_0, peer_u_0, peer_v_0, ada_w_1, ada_b_1, norm1_1, norm2_1, win_1, conv_w_1, conv_b_1, dt_bias_1, a_log_1, d_skip_1, gnorm_1, wout_1, peer_q_1, peer_keys_1, peer_u_1, peer_v_1, norm_f):
    nb, seq, d = x.shape
    ctx_len = ctx.shape[1]
    tm = 256 if (seq % 256 == 0 and ctx_len % 256 == 0) else 128
    assert seq % tm == 0 and ctx_len % tm == 0 and seq % GRID_W == 0

    def bid_fn(tile):
        per = seq // tile
        return lambda i: jnp.minimum(i // per, nb)

    xs = jnp.concatenate([x.reshape(nb * seq, d), ctx.reshape(nb * ctx_len, d)], axis=0)
    cvecs = jnp.zeros((16, d), F32).at[:nb].set(c).at[nb].set(c_ctx)

    mods0 = _ada_mod(cvecs, ada_w_0, ada_b_0)
    uv0 = jnp.concatenate([peer_u_0, peer_v_0], axis=1)
    xs = _even_layer(xs, mods0, nb, seq, ctx_len, tm, bid_fn, norm1_0, norm2_0, win_0, da_lambda_0, da_subln_0,
                     rw_mu_0, rw_w0_0, rw_w2_0, rw_a0_0, rw_a2_0, rw_g2_0, rw_kk_0, rw_ka_0, rw_rk_0, rw_lnx_w_0,
                     rw_lnx_b_0, wout_0, peer_q_0, peer_keys_0, uv0, 0.8 - 0.6 * math.exp(-0.3 * 0))

    mods1 = _ada_mod(cvecs, ada_w_1, ada_b_1)
    uv1 = jnp.concatenate([peer_u_1, peer_v_1], axis=1)
    out = _odd_layer_last(xs, mods1, nb, seq, ctx_len, tm, bid_fn, norm1_1, norm2_1, win_1, conv_w_1, conv_b_1,
                          dt_bias_1, a_log_1, d_skip_1, gnorm_1, wout_1, peer_q_1, peer_keys_1, uv1, norm_f)
    return out.reshape(nb, seq, d)
```

```python
import functools
import math

import jax
import jax.numpy as jnp
import numpy as np
from jax import lax
from jax.experimental import pallas as pl
from jax.experimental.pallas import tpu as pltpu

F32 = jnp.float32
BF16 = jnp.bfloat16
I32 = jnp.int32
HI = lax.Precision.HIGHEST

EPS = 1e-6
N_MOD = 6
GRID_W = 64
LANES = 128
VMEM_LIMIT_BYTES = 48 * 1024 * 1024

DA_HEAD_DIM = 64
DA_V_DIM = 128
ROPE_BASE = 10000.0
ROPE_NFREQ = DA_HEAD_DIM // 4
RW_HEAD_DIM = 64
RW_GN_EPS = 64e-5
RW_CHUNK = 64
RW_PASSES = 1
RW_STATE_PASSES = 3
M_HEAD_DIM = 64
M_STATE = 128
M_CHUNK = 128
PEER_HEADS = 8
PEER_NKEYS = 128
PEER_TOPK = 16
PEER_TOK = 8


def _cparams(*sem):
    return pltpu.CompilerParams(dimension_semantics=sem, vmem_limit_bytes=VMEM_LIMIT_BYTES)


def _nt(a, b, precision=None):
    return lax.dot_general(a, b, (((1,), (1,)), ((), ())), preferred_element_type=F32, precision=precision)


def _full(shape):
    nd = len(shape)
    return pl.BlockSpec(shape, lambda *_: (0,) * nd)


def _split_bf16(x):
    hi = x.astype(BF16)
    lo = (x - hi.astype(F32)).astype(BF16)
    return hi, lo


def _ada_kernel(c_ref, w_ref, b_ref, o_ref):
    c = c_ref[...]
    s = c * jax.nn.sigmoid(c)
    o_ref[...] = jnp.dot(s, w_ref[...], preferred_element_type=F32, precision=HI) + b_ref[...]


def _ada_mod(cvecs, w, b):
    r, d = cvecs.shape
    n = w.shape[1]
    tn = 1024
    m = pl.pallas_call(
        _ada_kernel,
        grid=(n // tn,),
        in_specs=[_full((r, d)), pl.BlockSpec((d, tn), lambda j: (0, j)), pl.BlockSpec((1, tn), lambda j: (0, j))],
        out_specs=pl.BlockSpec((r, tn), lambda j: (0, j)),
        out_shape=jax.ShapeDtypeStruct((r, n), F32),
        compiler_params=_cparams("parallel"),
        name="ada_mod",
    )(cvecs, w, b.reshape(1, n))
    return [m[:, k * d:(k + 1) * d].reshape(r, 1, d) for k in range(N_MOD)]


def _normmod_kernel(x_ref, g_ref, sh_ref, sc_ref, *refs, n_w, want_h):
    x = x_ref[...]
    y = x * lax.rsqrt(jnp.mean(x * x, axis=-1, keepdims=True) + EPS) * g_ref[...]
    h = y * (1.0 + sc_ref[0]) + sh_ref[0]
    hb = h.astype(BF16)
    for w_ref, o_ref in zip(refs[:n_w], refs[n_w:2 * n_w]):
        o_ref[...] = jnp.dot(hb, w_ref[...], preferred_element_type=F32).astype(o_ref.dtype)
    if want_h:
        refs[2 * n_w][...] = h


def _normmod_proj(x, g, shift, scale, ws, out_dtypes, bid, tm, want_h=False):
    t, d = x.shape
    n_w = len(ws)
    in_specs = [pl.BlockSpec((tm, d), lambda i: (i, 0)), _full((1, d)),
                pl.BlockSpec((1, 1, d), lambda i: (bid(i), 0, 0)),
                pl.BlockSpec((1, 1, d), lambda i: (bid(i), 0, 0))]
    in_specs += [_full(w.shape) for w in ws]
    out_specs = [pl.BlockSpec((tm, w.shape[1]), lambda i: (i, 0)) for w in ws]
    out_shape = [jax.ShapeDtypeStruct((t, w.shape[1]), dt) for w, dt in zip(ws, out_dtypes)]
    if want_h:
        out_specs.append(pl.BlockSpec((tm, d), lambda i: (i, 0)))
        out_shape.append(jax.ShapeDtypeStruct((t, d), F32))
    return pl.pallas_call(
        functools.partial(_normmod_kernel, n_w=n_w, want_h=want_h),
        grid=(t // tm,), in_specs=in_specs, out_specs=out_specs, out_shape=out_shape,
        compiler_params=_cparams("parallel"), name="normmod_proj",
    )(x, g.reshape(1, d), shift, scale, *ws)


def _proj_res_kernel(*refs, n_a):
    a_refs = refs[:n_a]
    w_refs = refs[n_a:2 * n_a]
    res_ref, gate_ref, o_ref = refs[2 * n_a:]
    acc = jnp.dot(a_refs[0][...], w_refs[0][...], preferred_element_type=F32)
    for a_ref, w_ref in zip(a_refs[1:], w_refs[1:]):
        acc += jnp.dot(a_ref[...], w_ref[...], preferred_element_type=F32)
    o_ref[...] = res_ref[...] + gate_ref[0] * acc


def _proj_residual(a_list, w_list, res, gate, bid, tm):
    t, n = res.shape
    n_a = len(a_list)
    in_specs = [pl.BlockSpec((tm, a.shape[1]), lambda i: (i, 0)) for a in a_list]
    in_specs += [_full(w.shape) for w in w_list]
    in_specs += [pl.BlockSpec((tm, n), lambda i: (i, 0)), pl.BlockSpec((1, 1, n), lambda i: (bid(i), 0, 0))]
    return pl.pallas_call(
        functools.partial(_proj_res_kernel, n_a=n_a),
        grid=(t // tm,), in_specs=in_specs, out_specs=pl.BlockSpec((tm, n), lambda i: (i, 0)),
        out_shape=jax.ShapeDtypeStruct((t, n), F32),
        compiler_params=_cparams("parallel"), name="proj_residual",
    )(*a_list, *w_list, res, gate)


def _rope_kernel(q_ref, k_ref, c_ref, s_ref, qo_ref, ko_ref):
    c = c_ref[...]
    s = s_ref[...]
    lane = lax.broadcasted_iota(I32, c.shape, 1)
    first = (lane % 32) < 16
    width = q_ref.shape[1]

    def rot(x):
        partner = jnp.where(first, pltpu.roll(x, LANES - 16, 1), pltpu.roll(x, 16, 1))
        return x * c + partner * s

    for g in range(width // LANES):
        sl = slice(g * LANES, (g + 1) * LANES)
        qo_ref[:, sl] = (rot(q_ref[:, sl]) * (DA_HEAD_DIM ** -0.5)).astype(qo_ref.dtype)
        ko_ref[:, sl] = rot(k_ref[:, sl]).astype(ko_ref.dtype)


def _rope_tables(seq_len, tm):
    rows = seq_len // GRID_W
    row = jnp.repeat(jnp.arange(rows, dtype=F32), GRID_W)
    col = (jnp.arange(seq_len) % GRID_W).astype(F32)
    inv = ROPE_BASE ** (-jnp.arange(ROPE_NFREQ, dtype=F32) / ROPE_NFREQ)
    ang_r = row[:, None] * inv
    ang_c = col[:, None] * inv
    cos64 = jnp.concatenate([jnp.cos(ang_r), jnp.cos(ang_r), jnp.cos(ang_c), jnp.cos(ang_c)], axis=1)
    sin64 = jnp.concatenate([-jnp.sin(ang_r), jnp.sin(ang_r), -jnp.sin(ang_c), jnp.sin(ang_c)], axis=1)
    cos = jnp.concatenate([jnp.tile(cos64, (1, 2)), jnp.ones((tm, LANES), F32)], axis=0)
    sin = jnp.concatenate([jnp.tile(sin64, (1, 2)), jnp.zeros((tm, LANES), F32)], axis=0)
    return cos, sin


def _rope(q, k, cos, sin, tab_block, tm):
    t, w = q.shape
    row = pl.BlockSpec((tm, w), lambda i: (i, 0))
    tab = pl.BlockSpec((tm, LANES), lambda i: (tab_block(i), 0))
    return pl.pallas_call(
        _rope_kernel, grid=(t // tm,), in_specs=[row, row, tab, tab], out_specs=[row, row],
        out_shape=[jax.ShapeDtypeStruct((t, w), BF16)] * 2,
        compiler_params=_cparams("parallel"), name="rope",
    )(q, k, cos, sin)


def _attn_kernel(lam_ref, sub_ref, q_ref, k_ref, v_ref, o_ref, *, lam_init):
    lp = lam_ref[...]
    lam = (jnp.exp(jnp.sum(lp[0:1] * lp[1:2], keepdims=True))
           - jnp.exp(jnp.sum(lp[2:3] * lp[3:4], keepdims=True)) + lam_init)
    q = q_ref[...]
    k = k_ref[...]
    v = v_ref[...]
    lane = lax.broadcasted_iota(I32, q.shape, 1)
    outs = []
    for m in range(2):
        sel = (lane < DA_HEAD_DIM) if m == 0 else (lane >= DA_HEAD_DIM)
        s = _nt(jnp.where(sel, q, jnp.zeros_like(q)), k)
        p = jnp.exp(s - jnp.max(s, axis=-1, keepdims=True))
        denom = jnp.sum(p, axis=-1, keepdims=True)
        outs.append(jnp.dot(p.astype(BF16), v, preferred_element_type=F32) / denom)
    o = outs[0] - lam * outs[1]
    o = o * lax.rsqrt(jnp.mean(o * o, axis=-1, keepdims=True) + EPS) * sub_ref[...] * (1.0 - lam_init)
    o_ref[...] = o.astype(o_ref.dtype)


def _diff_attention(q, k, v, lamp, subln, lam_init, nb, lq, lk, q_row0, tq):
    w = q.shape[1]
    heads = w // DA_V_DIM
    nq = lq // tq
    qb0 = q_row0 // tq
    return pl.pallas_call(
        functools.partial(_attn_kernel, lam_init=lam_init),
        grid=(nb, heads, nq),
        in_specs=[_full(lamp.shape), _full((1, DA_V_DIM)),
                  pl.BlockSpec((tq, DA_V_DIM), lambda b, h, i: (qb0 + b * nq + i, h)),
                  pl.BlockSpec((lk, DA_V_DIM), lambda b, h, i: (b, h)),
                  pl.BlockSpec((lk, DA_V_DIM), lambda b, h, i: (b, h))],
        out_specs=pl.BlockSpec((tq, DA_V_DIM), lambda b, h, i: (b * nq + i, h)),
        out_shape=jax.ShapeDtypeStruct((nb * lq, w), BF16),
        compiler_params=_cparams("parallel", "parallel", "arbitrary"), name="diff_attention",
    )(lamp, subln.reshape(1, DA_V_DIM), q, k, v)


def _softplus(z):
    return jnp.maximum(z, 0.0) + jnp.log(1.0 + jnp.exp(-jnp.abs(z)))


def _rw_prep_kernel(u_ref, prev_ref, next_ref, mu_ref, w0_ref, w2_ref, a0_ref, a2_ref, g2_ref, kk_ref, ka_ref,
                    rk_ref, ones_ref, r_ref, v_ref, nkk_ref, g_ref, bonus_ref, lw_ref, kd_ref, bd_ref):
    u = u_ref[...]
    tm = u.shape[0]
    width = r_ref.shape[1]
    row = lax.broadcasted_iota(I32, u.shape, 0)
    up = jnp.where(row == 0, prev_ref[0], pltpu.roll(u, 1, 0))
    dn = jnp.where(row == tm - 1, next_ref[0], pltpu.roll(u, tm - 1, 0))
    u = u + mu_ref[...] * (0.5 * (up + dn) - u)
    r = u[:, :width]
    k = u[:, width:2 * width]
    v = u[:, 2 * width:3 * width]
    o = 3 * width
    w_in = u[:, o:o + LANES]
    a_in = u[:, o + LANES:o + 2 * LANES]
    g_in = u[:, o + 2 * LANES:o + 3 * LANES]
    ones = ones_ref[...]
    hsum = lambda t: jnp.dot(t, ones, preferred_element_type=F32, precision=HI)
    g = jnp.dot(jax.nn.sigmoid(g_in), g2_ref[...], preferred_element_type=F32, precision=HI)
    kk = k * kk_ref[...]
    kk = kk / jnp.maximum(jnp.sqrt(hsum(kk * kk)), 1e-12)
    w_log = -_softplus(-(w0_ref[...] + jnp.dot(jnp.tanh(w_in), w2_ref[...], preferred_element_type=F32,
                                               precision=HI))) - 0.5
    logw = -jnp.exp(w_log)
    a = jax.nn.sigmoid(a0_ref[...] + jnp.dot(a_in, a2_ref[...], preferred_element_type=F32, precision=HI))
    ksum = jnp.zeros_like(k)
    for d in range(2):
        a_d = a[:, d * width:(d + 1) * width]
        k_d = k * (1.0 + (a_d - 1.0) * ka_ref[...])
        ksum = ksum + k_d
        lw_ref[d] = logw[:, d * width:(d + 1) * width]
        kd_ref[d] = k_d
        bd_ref[d] = kk * a_d
    r_ref[...] = r
    v_ref[...] = v
    nkk_ref[...] = -kk
    g_ref[...] = g
    bonus_ref[...] = hsum(r * ksum * rk_ref[...]) * v


def _halo_rows(x, tm, seq_lens):
    t = x.shape[0]
    nt = t // tm
    starts = np.cumsum([0] + [n for n in seq_lens])[:-1]
    ends = np.cumsum(seq_lens)
    tile_start = np.arange(nt) * tm
    has_prev = ~np.isin(tile_start, starts)
    has_next = ~np.isin(tile_start + tm, ends)
    last = x[tm - 1::tm]
    first = x[0::tm]
    zero = jnp.zeros_like(first[:1])
    prev = jnp.concatenate([zero, last[:-1]], axis=0) * jnp.asarray(has_prev, x.dtype)[:, None]
    nxt = jnp.concatenate([first[1:], zero], axis=0) * jnp.asarray(has_next, x.dtype)[:, None]
    return prev[:, None, :], nxt[:, None, :]


def _block_diag2(m):
    z = jnp.zeros_like(m[0])
    return jnp.concatenate([jnp.concatenate([m[0], z], axis=1), jnp.concatenate([z, m[1]], axis=1)], axis=0)


def _head_ones(width, hd):
    idx = np.arange(width) // hd
    return jnp.asarray((idx[:, None] == idx[None, :]).astype(np.float32))


def _rwkv_prepare(u, seq_lens, tm, mu, w0, w2, a0, a2, g2, k_k, k_a, r_k):
    t, cols = u.shape
    width = k_k.shape[0]
    prev, nxt = _halo_rows(u, tm, seq_lens)
    row = lambda c: pl.BlockSpec((tm, c), lambda i: (i, 0))
    halo = pl.BlockSpec((1, 1, cols), lambda i: (i, 0, 0))
    dir_out = pl.BlockSpec((2, tm, width), lambda i: (0, i, 0))
    consts = [mu.reshape(1, cols), w0.reshape(1, 2 * width), _block_diag2(w2), a0.reshape(1, 2 * width),
              _block_diag2(a2), g2, k_k.reshape(1, width), k_a.reshape(1, width), r_k.reshape(1, width),
              _head_ones(width, RW_HEAD_DIM)]
    f = jax.ShapeDtypeStruct((t, width), F32)
    f2 = jax.ShapeDtypeStruct((2, t, width), F32)
    return pl.pallas_call(
        _rw_prep_kernel, grid=(t // tm,),
        in_specs=[row(cols), halo, halo] + [_full(c.shape) for c in consts],
        out_specs=[row(width)] * 5 + [dir_out] * 3,
        out_shape=[f] * 5 + [f2] * 3,
        compiler_params=_cparams("parallel"), name="rwkv_prepare",
    )(u, prev, nxt, *consts)


def _mm(x, y, passes):
    if passes == 6:
        return jnp.dot(x, y, preferred_element_type=F32, precision=HI)
    dot = lambda p, q: jnp.dot(p, q, preferred_element_type=F32)
    if passes == 1:
        return dot(x.astype(BF16), y.astype(BF16))
    xh, xl = _split_bf16(x)
    yh, yl = _split_bf16(y)
    return dot(xh, yh) + (dot(xh, yl) + dot(xl, yh))


def _rw_scan_kernel(r_ref, v_ref, nkk_ref, lw_ref, kd_ref, bd_ref, h0_ref, y_ref, hf_ref, h_scr):
    d = pl.program_id(1)
    c = pl.program_id(2)
    nc = pl.num_programs(2)
    cs = RW_CHUNK
    pairs = h_scr.shape[0]
    sgn = 1 - 2 * d

    @pl.when(c == 0)
    def _():
        h_scr[...] = h0_ref[0, 0]

    ri = lax.broadcasted_iota(I32, (cs, cs), 0)
    ci = lax.broadcasted_iota(I32, (cs, cs), 1)
    before_eq = jnp.where((ri - ci) * sgn >= 0, 1.0, 0.0).astype(BF16)
    lane = lax.broadcasted_iota(I32, (cs, LANES), 1)
    lo_half = lane < RW_HEAD_DIM
    n2 = 2 * cs
    rt = lax.broadcasted_iota(I32, (n2, n2), 0)
    ct = lax.broadcasted_iota(I32, (n2, n2), 1)
    dtok = ((rt & (cs - 1)) - (ct & (cs - 1))) * sgn
    strict = dtok > 0
    incl = dtok >= 0
    eye = rt == ct

    def stack2(x):
        return jnp.concatenate([jnp.where(lo_half, x, 0.0), jnp.where(lo_half, 0.0, x)], axis=0)

    mm = functools.partial(_mm, passes=RW_PASSES)
    mm_state = functools.partial(_mm, passes=RW_STATE_PASSES)

    for p in range(pairs):
        sl = slice(p * LANES, (p + 1) * LANES)
        logw = lw_ref[0, :, sl]
        r = r_ref[:, sl]
        v = v_ref[:, sl]
        a = nkk_ref[:, sl]
        k = kd_ref[0, :, sl]
        b = bd_ref[0, :, sl]

        lw_hi, lw_lo = _split_bf16(logw)
        cum = (jnp.dot(before_eq, lw_hi, preferred_element_type=F32)
               + jnp.dot(before_eq, lw_lo, preferred_element_type=F32))
        mid = cum[cs // 2:cs // 2 + 1]
        tot = jnp.sum(logw, axis=0, keepdims=True)
        e_in = jnp.exp(mid - cum)
        e_end = jnp.exp(tot - cum)

        a2 = stack2(a * jnp.exp(cum - logw - mid))
        r2 = stack2(r * jnp.exp(cum - mid))
        a2_abs = stack2(a * jnp.exp(cum - logw))
        r2_abs = stack2(r * jnp.exp(cum))
        b2 = stack2(b * e_in)
        k2 = stack2(k * e_in)
        v2 = stack2(v)
        bh2 = stack2(b * e_end)
        kh2 = stack2(k * e_end)

        b2t = b2.T
        k2t = k2.T
        nmat = jnp.where(strict, mm(a2, b2t), 0.0)
        mmat = jnp.where(strict, mm(a2, k2t), 0.0)
        qb = jnp.where(incl, mm(r2, b2t), 0.0)
        qk = jnp.where(incl, mm(r2, k2t), 0.0)

        tinv = jnp.where(eye, 1.0, 0.0) + nmat
        pw = nmat
        for _ in range(int(math.log2(cs)) - 1):
            pw = mm(pw, pw)
            tinv = tinv + mm(tinv, pw)

        w2 = mm(tinv, mm(mmat, v2))
        a2p = mm(tinv, a2_abs)
        y_intra = mm(qk, v2) + mm(qb, w2)
        r2p = r2_abs + mm(qb, a2p)
        bh2t = bh2.T
        gmat = jnp.where(eye, jnp.exp(tot), 0.0) + mm(bh2t, a2p)
        dmat = mm(bh2t, w2) + mm(kh2.T, v2)

        h = h_scr[p]
        y2 = y_intra + mm_state(r2p, h)
        y_ref[0, :, sl] = y2[:cs] + y2[cs:]
        h_scr[p] = mm_state(gmat, h) + dmat

    @pl.when(c == nc - 1)
    def _():
        hf_ref[0, 0] = h_scr[...]


def _rwkv_scan(r, v, nkk, lw, kd, bd, h0, nb, seq_len, row0):
    w = r.shape[1]
    pairs = w // LANES
    nc = seq_len // RW_CHUNK
    rb0 = row0 // RW_CHUNK

    def chunk(c, d):
        return c + d * (nc - 1 - 2 * c)

    shared = pl.BlockSpec((RW_CHUNK, w), lambda b, d, c: (rb0 + b * nc + chunk(c, d), 0))
    perdir = pl.BlockSpec((1, RW_CHUNK, w), lambda b, d, c: (d, rb0 + b * nc + chunk(c, d), 0))
    state = pl.BlockSpec((1, 1, pairs, LANES, LANES), lambda b, d, c: (d, b, 0, 0, 0))
    return pl.pallas_call(
        _rw_scan_kernel, grid=(nb, 2, nc),
        in_specs=[shared, shared, shared, perdir, perdir, perdir, state],
        out_specs=[pl.BlockSpec((1, RW_CHUNK, w), lambda b, d, c: (d, b * nc + chunk(c, d), 0)), state],
        out_shape=[jax.ShapeDtypeStruct((2, nb * seq_len, w), F32), jax.ShapeDtypeStruct(h0.shape, F32)],
        scratch_shapes=[pltpu.VMEM((pairs, LANES, LANES), F32)],
        compiler_params=_cparams("parallel", "parallel", "arbitrary"), name="rwkv_scan",
    )(r, v, nkk, lw, kd, bd, h0)


def _rw_post_kernel(y_ref, g_ref, bonus_ref, lnw_ref, lnb_ref, ones_ref, o_ref):
    y = y_ref[0] + y_ref[1]
    ones = ones_ref[...]
    hmean = lambda t: jnp.dot(t, ones, preferred_element_type=F32, precision=HI) * (1.0 / RW_HEAD_DIM)
    yc = y - hmean(y)
    var = hmean(yc * yc)
    yn = yc * lax.rsqrt(var + RW_GN_EPS) * lnw_ref[...] + lnb_ref[...]
    o_ref[...] = ((yn + bonus_ref[...]) * g_ref[...]).astype(o_ref.dtype)


def _rwkv_post(y, g, bonus, lnw, lnb, tm):
    _, t, w = y.shape
    row = pl.BlockSpec((tm, w), lambda i: (i, 0))
    return pl.pallas_call(
        _rw_post_kernel, grid=(t // tm,),
        in_specs=[pl.BlockSpec((2, tm, w), lambda i: (0, i, 0)), row, row, _full((1, w)), _full((1, w)),
                  _full((w, w))],
        out_specs=row, out_shape=jax.ShapeDtypeStruct((t, w), BF16),
        compiler_params=_cparams("parallel"), name="rwkv_post",
    )(y, g, bonus, lnw.reshape(1, w), lnb.reshape(1, w), _head_ones(w, RW_HEAD_DIM))


def _rwkv_mixer(u, seq_lens, nb, seq, ctx_len, tm, mu, w0, w2, a0, a2, g2, k_k, k_a, r_k, lnx_w, lnx_b):
    t_lat = nb * seq
    r, vv, nkk, g, bonus, lw, kd, bd = _rwkv_prepare(u, seq_lens, tm, mu, w0, w2, a0, a2, g2, k_k, k_a, r_k)
    zero = jnp.zeros((2, nb, r.shape[1] // LANES, LANES, LANES), F32)
    y_ctx, h_ctx = _rwkv_scan(r, vv, nkk, lw, kd, bd, zero, nb, ctx_len, t_lat)
    y_lat, _ = _rwkv_scan(r, vv, nkk, lw, kd, bd, h_ctx, nb, seq, 0)
    y = jnp.concatenate([y_lat, y_ctx], axis=1)
    return _rwkv_post(y, g, bonus, lnx_w, lnx_b, tm)


def _extract_topk(s, order, payload, count):
    big = float(2 ** 24)
    vals, pays = [], []
    for _ in range(count):
        m = jnp.max(s, axis=0, keepdims=True)
        first = jnp.min(jnp.where(s == m, order, big), axis=0, keepdims=True)
        hit = order == first
        vals.append(m)
        pays.append(first if payload is None else jnp.sum(jnp.where(hit, payload, 0.0), axis=0, keepdims=True))
        s = jnp.where(hit, -jnp.inf, s)
    return jnp.concatenate(vals, axis=0), jnp.concatenate(pays, axis=0)


def _pruned_candidates(v1, i1, v2, i2):
    k = PEER_TOPK
    tt = v1.shape[1]
    row8 = lax.broadcasted_iota(I32, (8, tt), 0).astype(F32)
    row16 = lax.broadcasted_iota(I32, (k, tt), 0).astype(F32)
    nk = float(PEER_NKEYS)
    sums, flats, eids = [], [], []

    def add(valid, s, flat, eid):
        unused = float(k * k + 16 * len(sums))
        sums.append(s if valid is None else jnp.where(valid, s, -jnp.inf))
        flats.append(flat if valid is None else jnp.where(valid, flat, flat + unused))
        eids.append(eid)

    def vary_j(i, rows, nvalid):
        r = row16 if rows == k else row8
        add(None if nvalid == rows else r < nvalid, v1[i:i + 1] + v2[:rows], r + float(i * k),
            i1[i:i + 1] * nk + i2[:rows])

    def vary_i(j, i0, lo, hi):
        r = row8 + float(i0)
        add(None if (lo == i0 and hi == i0 + 8) else (r >= lo) & (r < hi), v1[i0:i0 + 8] + v2[j:j + 1],
            r * float(k) + float(j), i1[i0:i0 + 8] * nk + i2[j:j + 1])

    vary_j(0, k, k)
    vary_j(1, 8, 8)
    vary_j(2, 8, 5)
    vary_j(3, 8, 4)
    vary_i(0, 8, 8, 16)
    vary_i(0, 0, 4, 8)
    vary_i(1, 0, 4, 8)
    vary_i(2, 0, 4, 5)
    return jnp.concatenate(sums, axis=0), jnp.concatenate(flats, axis=0), jnp.concatenate(eids, axis=0)


def _peer_topk_kernel(q_ref, keys_ref, idx_ref, gate_ref):
    tt = q_ref.shape[0]
    kpos = lax.broadcasted_iota(I32, (PEER_NKEYS, tt), 0).astype(F32)

    def head(h, carry):
        vs, ids = [], []
        for p in range(2):
            col = pl.multiple_of((2 * h + p) * LANES, LANES)
            s = _nt(keys_ref[h, p], q_ref[:, pl.ds(col, LANES)], HI)
            v_p, i_p = _extract_topk(s, kpos, None, PEER_TOPK)
            vs.append(v_p)
            ids.append(i_p)
        cand, flat, eid = _pruned_candidates(vs[0], ids[0], vs[1], ids[1])
        top_s, top_i = _extract_topk(cand, flat, eid, PEER_TOPK)
        e = jnp.exp(top_s - top_s[0:1])
        rows = pl.ds(pl.multiple_of(h * PEER_TOPK, PEER_TOPK), PEER_TOPK)
        gate_ref[rows, :] = e / jnp.sum(e, axis=0, keepdims=True)
        idx_ref[rows, :] = top_i.astype(I32)
        return carry

    lax.fori_loop(0, PEER_HEADS, head, 0)


def _peer_topk(q, keys, tt):
    t = q.shape[0]
    ne = PEER_HEADS * PEER_TOPK
    out = pl.BlockSpec((ne, tt), lambda i: (0, i))
    return pl.pallas_call(
        _peer_topk_kernel, grid=(t // tt,),
        in_specs=[pl.BlockSpec((tt, q.shape[1]), lambda i: (i, 0)), _full(keys.shape)],
        out_specs=[out, out],
        out_shape=[jax.ShapeDtypeStruct((ne, t), I32), jax.ShapeDtypeStruct((ne, t), F32)],
        compiler_params=_cparams("parallel"), name="peer_topk",
    )(q, keys)


def _peer_gather_kernel(idx_ref, idxn_ref, h_ref, gates_ref, x_ref, mod_ref, nf_ref, uv_ref, o_ref, buf0, buf1, sem,
                        *, final_norm):
    i = pl.program_id(0)
    n = pl.num_programs(0)
    half = PEER_TOK
    ne = gates_ref.shape[1]
    d = h_ref.shape[1]
    nch = d // LANES
    zeros_h = jnp.zeros((6, d), BF16)
    zeros_e = jnp.zeros((6, ne), BF16)

    def issue(ids_ref, row, buf, s, t):
        for e in range(ne):
            pltpu.make_async_copy(uv_ref.at[ids_ref[row, e]], buf.at[:, t * ne + e, :], sem.at[s]).start()

    def wait_all(buf, s):
        pltpu.make_async_copy(buf, buf, sem.at[s]).wait()

    def compute(buf, t, row):
        rows = slice(t * ne, (t + 1) * ne)
        hi, lo = _split_bf16(h_ref[row:row + 1, :])
        hp = jnp.concatenate([hi, lo, zeros_h], axis=0)
        pre = _nt(hp[:, :LANES], buf[0, rows, :].astype(BF16))
        for c in range(1, nch):
            pre = pre + _nt(hp[:, c * LANES:(c + 1) * LANES], buf[c, rows, :].astype(BF16))
        pre = pre[0:1] + pre[1:2]
        act = 0.5 * pre * (1.0 + lax.erf(pre * (2.0 ** -0.5)))
        whi, wlo = _split_bf16(gates_ref[row:row + 1, :] * act)
        wp = jnp.concatenate([whi, wlo, zeros_e], axis=0)
        o = jnp.concatenate([jnp.dot(wp, buf[nch + c, rows, :].astype(BF16), preferred_element_type=F32)
                             for c in range(nch)], axis=1)
        return o[0:1] + o[1:2]

    @pl.when(i == 0)
    def _():
        for t in range(half):
            issue(idx_ref, t, buf0, 0, t)

    outs = []
    wait_all(buf0, 0)
    for t in range(half):
        issue(idx_ref, half + t, buf1, 1, t)
        outs.append(compute(buf0, t, t))
    wait_all(buf1, 1)
    for t in range(half):
        issue(idxn_ref, t, buf0, 0, t)
        outs.append(compute(buf1, t, half + t))

    @pl.when(i == n - 1)
    def _():
        wait_all(buf0, 0)

    y = x_ref[...] + mod_ref[0] * jnp.concatenate(outs, axis=0)
    if final_norm:
        y = y * lax.rsqrt(jnp.mean(y * y, axis=-1, keepdims=True) + EPS) * nf_ref[...]
    o_ref[...] = y


def _peer_gather(idx, gates, h, x, gate_mod, norm_f, uv, bid, final_norm):
    t, d = x.shape
    ne = idx.shape[1]
    tt = 2 * PEER_TOK
    n = t // tt
    nch2 = uv.shape[1]
    row = lambda c: pl.BlockSpec((tt, c), lambda i: (i, 0))
    return pl.pallas_call(
        functools.partial(_peer_gather_kernel, final_norm=final_norm),
        grid=(n,),
        in_specs=[pl.BlockSpec((tt, ne), lambda i: (i, 0), memory_space=pltpu.SMEM),
                  pl.BlockSpec((tt, ne), lambda i: (jnp.minimum(i + 1, n - 1), 0), memory_space=pltpu.SMEM),
                  row(d), row(ne), row(d), pl.BlockSpec((1, 1, d), lambda i: (bid(i), 0, 0)), _full((1, d)),
                  pl.BlockSpec(memory_space=pl.ANY)],
        out_specs=row(d), out_shape=jax.ShapeDtypeStruct((t, d), F32),
        scratch_shapes=[pltpu.VMEM((nch2, PEER_TOK * ne, LANES), F32), pltpu.VMEM((nch2, PEER_TOK * ne, LANES), F32),
                        pltpu.SemaphoreType.DMA((2,))],
        compiler_params=_cparams("arbitrary"), name="peer_gather",
    )(idx, idx, h, gates, x, gate_mod, norm_f.reshape(1, d), uv)


def _peer(x, norm2, shift, scale, gate_mod, wq, keys, uv, norm_f, bid_fn, tm, final_norm):
    q, h = _normmod_proj(x, norm2, shift, scale, [wq], [F32], bid_fn(tm), tm, want_h=True)
    idx_t, gates_t = _peer_topk(q, keys, LANES)
    return _peer_gather(idx_t.T, gates_t.T, h, x, gate_mod, norm_f, uv, bid_fn(2 * PEER_TOK), final_norm)


def _conv_kernel(x_ref, prev_ref, next_ref, w_ref, b_ref, dtr_ref, dtb_ref, o_ref, dt_ref, *, heads):
    x = x_ref[...]
    tm = x.shape[0]
    row = lax.broadcasted_iota(I32, x.shape, 0)
    up = jnp.where(row == 0, prev_ref[0], pltpu.roll(x, 1, 0))
    dn = jnp.where(row == tm - 1, next_ref[0], pltpu.roll(x, tm - 1, 0))
    y = up * w_ref[0:1] + x * w_ref[1:2] + dn * w_ref[2:3] + b_ref[...]
    o_ref[...] = y * jax.nn.sigmoid(y)
    lane = lax.broadcasted_iota(I32, (tm, LANES), 1)
    for d in range(2):
        dt_ref[d] = jnp.where(lane < heads, _softplus(dtr_ref[d] + dtb_ref[d]), 0.0)


def _mamba_conv(xbc, dt_raw, seq_lens, tm, conv_w, conv_b, dt_bias_pad, heads):
    t, c = xbc.shape
    prev, nxt = _halo_rows(xbc, tm, seq_lens)
    row = pl.BlockSpec((tm, c), lambda i: (i, 0))
    halo = pl.BlockSpec((1, 1, c), lambda i: (i, 0, 0))
    dts = pl.BlockSpec((2, tm, LANES), lambda i: (0, i, 0))
    return pl.pallas_call(
        functools.partial(_conv_kernel, heads=heads), grid=(t // tm,),
        in_specs=[row, halo, halo, _full(conv_w.shape), _full((1, c)), dts, _full((2, 1, LANES))],
        out_specs=[row, dts],
        out_shape=[jax.ShapeDtypeStruct((t, c), F32), jax.ShapeDtypeStruct((2, t, LANES), F32)],
        compiler_params=_cparams("parallel"), name="mamba_conv",
    )(xbc, prev, nxt, conv_w, conv_b.reshape(1, c), dt_raw, dt_bias_pad)


def _ssd_kernel(*refs, reverse, inner, groups, add_prev):
    if add_prev:
        xbc_ref, dt_ref, dtt_ref, alr_ref, alc_ref, rep_ref, h0_ref, yin_ref, y_ref, hf_ref, h_scr = refs
    else:
        xbc_ref, dt_ref, dtt_ref, alr_ref, alc_ref, rep_ref, h0_ref, y_ref, hf_ref, h_scr = refs
        yin_ref = None
    c = pl.program_id(1)
    nc = pl.num_programs(1)
    cs = M_CHUNK
    gw = inner // groups
    hpg = gw // M_HEAD_DIM

    @pl.when(c == 0)
    def _():
        h_scr[...] = h0_ref[0]

    dt = dt_ref[0]
    dtt = dtt_ref[0]
    a = dt * (-jnp.exp(alr_ref[...]))
    at = dtt * (-jnp.exp(alc_ref[...]))
    ri = lax.broadcasted_iota(I32, (cs, cs), 0)
    ci = lax.broadcasted_iota(I32, (cs, cs), 1)
    incl = (ri <= ci) if reverse else (ri >= ci)
    tri = jnp.where(incl, 1.0, 0.0)
    cum = jnp.dot(tri, a, preferred_element_type=F32, precision=HI)
    cumt = _nt(at, tri, HI)
    tot = jnp.sum(a, axis=0, keepdims=True)

    rep = rep_ref[...]

    def spread(t):
        hi, lo = _split_bf16(t)
        return jnp.dot(hi, rep, preferred_element_type=F32) + jnp.dot(lo, rep, preferred_element_type=F32)

    e_cum = spread(jnp.exp(cum))
    e_end = spread(jnp.exp(tot - cum) * dt)
    e_tot = spread(jnp.broadcast_to(jnp.exp(tot), (8, LANES)))[0:1]

    lane = lax.broadcasted_iota(I32, (cs, LANES), 1)
    lo_half = lane < M_HEAD_DIM
    ys = []
    for g in range(groups):
        bg32 = xbc_ref[:, inner + g * M_STATE:inner + (g + 1) * M_STATE]
        bg = bg32.astype(BF16)
        cg = xbc_ref[:, inner + groups * M_STATE + g * M_STATE:inner + groups * M_STATE + (g + 1) * M_STATE]
        cg = cg.astype(BF16)
        cb = _nt(cg, bg)
        hprev = h_scr[g]
        xg = xbc_ref[:, g * gw:(g + 1) * gw]
        y_off = jnp.dot(cg, hprev.astype(BF16), preferred_element_type=F32) * e_cum[:, g * gw:(g + 1) * gw]
        xd = (xg * e_end[:, g * gw:(g + 1) * gw]).astype(BF16)
        h_scr[g] = e_tot[:, g * gw:(g + 1) * gw] * hprev + jnp.dot(bg32.T.astype(BF16), xd, preferred_element_type=F32)
        for j in range(hpg // 2):
            xpair = xg[:, j * LANES:(j + 1) * LANES].astype(BF16)
            halves = []
            for hh in range(2):
                h = g * hpg + 2 * j + hh
                seg = jnp.minimum(cum[:, h:h + 1] - cumt[h:h + 1, :], 0.0)
                m = jnp.where(incl, cb * jnp.exp(seg), 0.0) * dtt[h:h + 1, :]
                halves.append(jnp.dot(m.astype(BF16), xpair, preferred_element_type=F32))
            ys.append(jnp.where(lo_half, halves[0], halves[1]) + y_off[:, j * LANES:(j + 1) * LANES])
    y = jnp.concatenate(ys, axis=1)
    if add_prev:
        y = y + yin_ref[...]
    y_ref[...] = y

    @pl.when(c == nc - 1)
    def _():
        hf_ref[0] = h_scr[...]


def _ssd_pass(xbc, dt, dtt, a_log, h0, y_prev, nb, seq_len, row0, reverse, inner, groups, heads):
    nc = seq_len // M_CHUNK
    rb0 = row0 // M_CHUNK
    c_all = xbc.shape[1]
    gw = inner // groups
    alr = jnp.zeros((1, LANES), F32).at[0, :heads].set(a_log)
    alc = jnp.broadcast_to(jnp.zeros((LANES,), F32).at[:heads].set(a_log)[:, None], (LANES, LANES))
    hid = np.arange(inner) // M_HEAD_DIM
    rep = jnp.asarray((np.arange(LANES)[:, None] == hid[None, :]).astype(np.float32), BF16)
    chunk = (lambda c: nc - 1 - c) if reverse else (lambda c: c)
    add_prev = y_prev is not None
    in_specs = [pl.BlockSpec((M_CHUNK, c_all), lambda b, c: (rb0 + b * nc + chunk(c), 0)),
                pl.BlockSpec((1, M_CHUNK, LANES), lambda b, c: (0, rb0 + b * nc + chunk(c), 0)),
                pl.BlockSpec((1, LANES, M_CHUNK), lambda b, c: (0, 0, rb0 + b * nc + chunk(c))),
                _full((1, LANES)), _full((LANES, LANES)), _full((LANES, inner)),
                pl.BlockSpec((1, groups, M_STATE, gw), lambda b, c: (b, 0, 0, 0))]
    args = [xbc, dt, dtt, alr, alc, rep, h0]
    yspec = pl.BlockSpec((M_CHUNK, inner), lambda b, c: (b * nc + chunk(c), 0))
    if add_prev:
        in_specs.append(yspec)
        args.append(y_prev)
    return pl.pallas_call(
        functools.partial(_ssd_kernel, reverse=reverse, inner=inner, groups=groups, add_prev=add_prev),
        grid=(nb, nc), in_specs=in_specs,
        out_specs=[yspec, pl.BlockSpec((1, groups, M_STATE, gw), lambda b, c: (b, 0, 0, 0))],
        out_shape=[jax.ShapeDtypeStruct((nb * seq_len, inner), F32), jax.ShapeDtypeStruct(h0.shape, F32)],
        scratch_shapes=[pltpu.VMEM((groups, M_STATE, gw), F32)],
        compiler_params=_cparams("parallel", "arbitrary"), name="ssd_pass",
    )(*args)


def _mamba_gate_kernel(y_ref, x_ref, z_ref, dsk_ref, gn_ref, o_ref, *, groups):
    z = z_ref[...]
    y = (y_ref[...] + dsk_ref[...] * x_ref[...]) * (z * jax.nn.sigmoid(z))
    gw = y.shape[1] // groups
    for g in range(groups):
        yg = y[:, g * gw:(g + 1) * gw]
        yg = yg * lax.rsqrt(jnp.mean(yg * yg, axis=-1, keepdims=True) + EPS) * gn_ref[:, g * gw:(g + 1) * gw]
        o_ref[:, g * gw:(g + 1) * gw] = yg.astype(o_ref.dtype)


def _mamba_gate(y, xbc, z, d_skip_cols, gnorm, groups, tm):
    t, inner = y.shape
    row = pl.BlockSpec((tm, inner), lambda i: (i, 0))
    return pl.pallas_call(
        functools.partial(_mamba_gate_kernel, groups=groups), grid=(t // tm,),
        in_specs=[row, row, row, _full((1, inner)), _full((1, inner))],
        out_specs=row, out_shape=jax.ShapeDtypeStruct((t, inner), BF16),
        compiler_params=_cparams("parallel"), name="mamba_gate",
    )(y, xbc, z, d_skip_cols.reshape(1, inner), gnorm.reshape(1, inner))


def _even_layer(xs, mods, nb, seq, ctx_len, tm, bid_fn, norm1, norm2, win, da_lambda, da_subln, rw_mu, rw_w0,
                rw_w2, rw_a0, rw_a2, rw_g2, rw_kk, rw_ka, rw_rk, rw_lnx_w, rw_lnx_b, wout, peer_q, peer_keys,
                peer_uv, lam_init):
    d = xs.shape[1]
    t_lat = nb * seq
    da_w = d // 2
    rw_w = d - da_w
    winb = win.astype(BF16)
    ws = [winb[:, :da_w], winb[:, da_w:2 * da_w], winb[:, 2 * da_w:3 * da_w], winb[:, 3 * da_w:]]
    q, k, v, u = _normmod_proj(xs, norm1, mods[0], mods[1], ws, [F32, F32, BF16, F32], bid_fn(tm), tm)

    cos, sin = _rope_tables(seq, tm)
    lat_tiles = t_lat // tm
    tab_block = lambda i: jnp.where(i < lat_tiles, i % (seq // tm), seq // tm)
    qr, kr = _rope(q, k, cos, sin, tab_block, tm)
    lk = ctx_len + seq
    cat = lambda a: jnp.concatenate([a[t_lat:].reshape(nb, ctx_len, da_w), a[:t_lat].reshape(nb, seq, da_w)],
                                    axis=1).reshape(nb * lk, da_w)
    tq = min(256, seq)
    o_lat = _diff_attention(qr, cat(kr), cat(v), da_lambda, da_subln, lam_init, nb, seq, lk, 0, tq)
    tqc = min(256, ctx_len)
    o_ctx = _diff_attention(qr, kr[t_lat:], v[t_lat:], da_lambda, da_subln, lam_init, nb, ctx_len, ctx_len,
                            t_lat, tqc)
    o_att = jnp.concatenate([o_lat, o_ctx], axis=0)

    seq_lens = [seq] * nb + [ctx_len] * nb
    o_rw = _rwkv_mixer(u, seq_lens, nb, seq, ctx_len, tm, rw_mu, rw_w0, rw_w2, rw_a0, rw_a2, rw_g2, rw_kk, rw_ka,
                       rw_rk, rw_lnx_w, rw_lnx_b)

    woutb = wout.astype(BF16)
    xs = _proj_residual([o_att, o_rw], [woutb[:da_w], woutb[da_w:]], xs, mods[2], bid_fn(tm), tm)
    return _peer(xs, norm2, mods[3], mods[4], mods[5], peer_q.astype(BF16), peer_keys, peer_uv, norm2, bid_fn, tm,
                 False)


def _odd_layer_last(xs, mods, nb, seq, ctx_len, tm, bid_fn, norm1, norm2, win, conv_w, conv_b, dt_bias, a_log,
                    d_skip, gnorm, wout, peer_q, peer_keys, peer_uv, norm_f):
    d = xs.shape[1]
    t_lat = nb * seq
    inner = wout.shape[0]
    heads = a_log.shape[1]
    conv_dim = conv_w.shape[1]
    groups = (conv_dim - inner) // (2 * M_STATE)
    winb = win.astype(BF16)
    pad = jnp.zeros((d, LANES - heads), BF16)
    w_dt = [jnp.concatenate([winb[:, inner + conv_dim + k * heads:inner + conv_dim + (k + 1) * heads], pad], axis=1)
            for k in range(2)]
    ws = [winb[:, :inner], winb[:, inner:inner + conv_dim]] + w_dt
    z, xbc_raw, dtr_f, dtr_b = _normmod_proj(xs, norm1, mods[0], mods[1], ws, [F32] * 4, bid_fn(tm), tm)
    seq_lens = [seq] * nb + [ctx_len] * nb
    dtb = jnp.zeros((2, 1, LANES), F32).at[:, 0, :heads].set(dt_bias)
    xbc, dt = _mamba_conv(xbc_raw, jnp.stack([dtr_f, dtr_b]), seq_lens, tm, conv_w, conv_b, dtb, heads)
    dtt = jnp.swapaxes(dt, 1, 2)
    h0 = jnp.zeros((nb, groups, M_STATE, inner // groups), F32)
    ssd = functools.partial(_ssd_pass, xbc, inner=inner, groups=groups, heads=heads)
    _, hf = ssd(dt[0:1], dtt[0:1], a_log[0], h0, None, nb, ctx_len, t_lat, False)
    _, hb = ssd(dt[1:2], dtt[1:2], a_log[1], h0, None, nb, ctx_len, t_lat, True)
    y, _ = ssd(dt[0:1], dtt[0:1], a_log[0], hf, None, nb, seq, 0, False)
    y, _ = ssd(dt[1:2], dtt[1:2], a_log[1], hb, y, nb, seq, 0, True)
    x_lat = xs[:t_lat]
    gated = _mamba_gate(y, xbc, z, jnp.repeat(d_skip, M_HEAD_DIM), gnorm, groups, tm)
    x_lat = _proj_residual([gated], [wout.astype(BF16)], x_lat, mods[2], bid_fn(tm), tm)
    return _peer(x_lat, norm2, mods[3], mods[4], mods[5], peer_q.astype(BF16), peer_keys, peer_uv, norm_f, bid_fn,
                 tm, True)


def kernel(x, c, ctx, c_ctx, ada_w_0, ada_b_0, norm1_0, norm2_0, win_0, da_lambda_0, da_subln_0, rw_mu_0, rw_w0_0, rw_w2_0, rw_a0_0, rw_a2_0, rw_g2_0, rw_kk_0, rw_ka_0, rw_rk_0, rw_lnx_w_0, rw_lnx_b_0, wout_0, peer_q_0, peer_keys_0, peer_u_0, peer_v_0, ada_w_1, ada_b_1, norm1_1, norm2_1, win_1, conv_w_1, conv_b_1, dt_bias_1, a_log_1, d_skip_1, gnorm_1, wout_1, peer_q_1, peer_keys_1, peer_u_1, peer_v_1, norm_f):
    nb, seq, d = x.shape
    ctx_len = ctx.shape[1]
    tm = 256 if (seq % 256 == 0 and ctx_len % 256 == 0) else 128
    assert seq % tm == 0 and ctx_len % tm == 0 and seq % GRID_W == 0

    def bid_fn(tile):
        per = seq // tile
        return lambda i: jnp.minimum(i // per, nb)

    xs = jnp.concatenate([x.reshape(nb * seq, d), ctx.reshape(nb * ctx_len, d)], axis=0)
    cvecs = jnp.zeros((16, d), F32).at[:nb].set(c).at[nb].set(c_ctx)

    expert_rows = lambda u, v: jnp.concatenate([u, v], axis=1).reshape(u.shape[0], 2 * d // LANES, LANES)

    mods0 = _ada_mod(cvecs, ada_w_0, ada_b_0)
    uv0 = expert_rows(peer_u_0, peer_v_0)
    xs = _even_layer(xs, mods0, nb, seq, ctx_len, tm, bid_fn, norm1_0, norm2_0, win_0, da_lambda_0, da_subln_0,
                     rw_mu_0, rw_w0_0, rw_w2_0, rw_a0_0, rw_a2_0, rw_g2_0, rw_kk_0, rw_ka_0, rw_rk_0, rw_lnx_w_0,
                     rw_lnx_b_0, wout_0, peer_q_0, peer_keys_0, uv0, 0.8 - 0.6 * math.exp(-0.3 * 0))

    mods1 = _ada_mod(cvecs, ada_w_1, ada_b_1)
    uv1 = expert_rows(peer_u_1, peer_v_1)
    out = _odd_layer_last(xs, mods1, nb, seq, ctx_len, tm, bid_fn, norm1_1, norm2_1, win_1, conv_w_1, conv_b_1,
                          dt_bias_1, a_log_1, d_skip_1, gnorm_1, wout_1, peer_q_1, peer_keys_1, uv1, norm_f)
    return out.reshape(nb, seq, d)
```

```python
import functools
import math

import jax
import jax.numpy as jnp
import numpy as np
from jax import lax
from jax.experimental import pallas as pl
from jax.experimental.pallas import tpu as pltpu
from jax.experimental.pallas import tpu_sc as plsc

F32 = jnp.float32
BF16 = jnp.bfloat16
I32 = jnp.int32
HI = lax.Precision.HIGHEST

EPS = 1e-6
N_MOD = 6
GRID_W = 64
LANES = 128
VMEM_LIMIT_BYTES = 48 * 1024 * 1024

DA_HEAD_DIM = 64
DA_V_DIM = 128
ROPE_BASE = 10000.0
ROPE_NFREQ = DA_HEAD_DIM // 4
RW_HEAD_DIM = 64
RW_GN_EPS = 64e-5
RW_CHUNK = 64
RW_PASSES = 1
RW_STATE_PASSES = 3
M_HEAD_DIM = 64
M_STATE = 128
M_CHUNK = 128
PEER_HEADS = 8
PEER_NKEYS = 128
PEER_TOPK = 16
PEER_TOK = 8
PEER_CHUNKS = 4
SC_CORES = 2
SC_SUBCORES = 16
SC_WINDOW = 16
SC_NBUF = 4


def _cparams(*sem):
    return pltpu.CompilerParams(dimension_semantics=sem, vmem_limit_bytes=VMEM_LIMIT_BYTES)


def _nt(a, b, precision=None):
    return lax.dot_general(a, b, (((1,), (1,)), ((), ())), preferred_element_type=F32, precision=precision)


def _full(shape):
    nd = len(shape)
    return pl.BlockSpec(shape, lambda *_: (0,) * nd)


def _split_bf16(x):
    hi = x.astype(BF16)
    lo = (x - hi.astype(F32)).astype(BF16)
    return hi, lo


def _ada_kernel(c_ref, w_ref, b_ref, o_ref):
    c = c_ref[...]
    s = c * jax.nn.sigmoid(c)
    o_ref[...] = jnp.dot(s, w_ref[...], preferred_element_type=F32, precision=HI) + b_ref[...]


def _ada_mod(cvecs, w, b):
    r, d = cvecs.shape
    n = w.shape[1]
    tn = 1024
    m = pl.pallas_call(
        _ada_kernel,
        grid=(n // tn,),
        in_specs=[_full((r, d)), pl.BlockSpec((d, tn), lambda j: (0, j)), pl.BlockSpec((1, tn), lambda j: (0, j))],
        out_specs=pl.BlockSpec((r, tn), lambda j: (0, j)),
        out_shape=jax.ShapeDtypeStruct((r, n), F32),
        compiler_params=_cparams("parallel"),
        name="ada_mod",
    )(cvecs, w, b.reshape(1, n))
    return [m[:, k * d:(k + 1) * d].reshape(r, 1, d) for k in range(N_MOD)]


def _normmod_kernel(x_ref, g_ref, sh_ref, sc_ref, *refs, n_w, want_h):
    x = x_ref[...]
    y = x * lax.rsqrt(jnp.mean(x * x, axis=-1, keepdims=True) + EPS) * g_ref[...]
    h = y * (1.0 + sc_ref[0]) + sh_ref[0]
    hb = h.astype(BF16)
    for w_ref, o_ref in zip(refs[:n_w], refs[n_w:2 * n_w]):
        o_ref[...] = jnp.dot(hb, w_ref[...], preferred_element_type=F32).astype(o_ref.dtype)
    if want_h:
        refs[2 * n_w][...] = h


def _normmod_proj(x, g, shift, scale, ws, out_dtypes, bid, tm, want_h=False):
    t, d = x.shape
    n_w = len(ws)
    in_specs = [pl.BlockSpec((tm, d), lambda i: (i, 0)), _full((1, d)),
                pl.BlockSpec((1, 1, d), lambda i: (bid(i), 0, 0)),
                pl.BlockSpec((1, 1, d), lambda i: (bid(i), 0, 0))]
    in_specs += [_full(w.shape) for w in ws]
    out_specs = [pl.BlockSpec((tm, w.shape[1]), lambda i: (i, 0)) for w in ws]
    out_shape = [jax.ShapeDtypeStruct((t, w.shape[1]), dt) for w, dt in zip(ws, out_dtypes)]
    if want_h:
        out_specs.append(pl.BlockSpec((tm, d), lambda i: (i, 0)))
        out_shape.append(jax.ShapeDtypeStruct((t, d), F32))
    return pl.pallas_call(
        functools.partial(_normmod_kernel, n_w=n_w, want_h=want_h),
        grid=(t // tm,), in_specs=in_specs, out_specs=out_specs, out_shape=out_shape,
        compiler_params=_cparams("parallel"), name="normmod_proj",
    )(x, g.reshape(1, d), shift, scale, *ws)


def _proj_res_kernel(*refs, n_a):
    a_refs = refs[:n_a]
    w_refs = refs[n_a:2 * n_a]
    res_ref, gate_ref, o_ref = refs[2 * n_a:]
    acc = jnp.dot(a_refs[0][...], w_refs[0][...], preferred_element_type=F32)
    for a_ref, w_ref in zip(a_refs[1:], w_refs[1:]):
        acc += jnp.dot(a_ref[...], w_ref[...], preferred_element_type=F32)
    o_ref[...] = res_ref[...] + gate_ref[0] * acc


def _proj_residual(a_list, w_list, res, gate, bid, tm):
    t, n = res.shape
    n_a = len(a_list)
    in_specs = [pl.BlockSpec((tm, a.shape[1]), lambda i: (i, 0)) for a in a_list]
    in_specs += [_full(w.shape) for w in w_list]
    in_specs += [pl.BlockSpec((tm, n), lambda i: (i, 0)), pl.BlockSpec((1, 1, n), lambda i: (bid(i), 0, 0))]
    return pl.pallas_call(
        functools.partial(_proj_res_kernel, n_a=n_a),
        grid=(t // tm,), in_specs=in_specs, out_specs=pl.BlockSpec((tm, n), lambda i: (i, 0)),
        out_shape=jax.ShapeDtypeStruct((t, n), F32),
        compiler_params=_cparams("parallel"), name="proj_residual",
    )(*a_list, *w_list, res, gate)


def _rope_kernel(q_ref, k_ref, c_ref, s_ref, qo_ref, ko_ref):
    c = c_ref[...]
    s = s_ref[...]
    lane = lax.broadcasted_iota(I32, c.shape, 1)
    first = (lane % 32) < 16
    width = q_ref.shape[1]

    def rot(x):
        partner = jnp.where(first, pltpu.roll(x, LANES - 16, 1), pltpu.roll(x, 16, 1))
        return x * c + partner * s

    for g in range(width // LANES):
        sl = slice(g * LANES, (g + 1) * LANES)
        qo_ref[:, sl] = (rot(q_ref[:, sl]) * (DA_HEAD_DIM ** -0.5)).astype(qo_ref.dtype)
        ko_ref[:, sl] = rot(k_ref[:, sl]).astype(ko_ref.dtype)


def _rope_tables(seq_len, tm):
    rows = seq_len // GRID_W
    row = jnp.repeat(jnp.arange(rows, dtype=F32), GRID_W)
    col = (jnp.arange(seq_len) % GRID_W).astype(F32)
    inv = ROPE_BASE ** (-jnp.arange(ROPE_NFREQ, dtype=F32) / ROPE_NFREQ)
    ang_r = row[:, None] * inv
    ang_c = col[:, None] * inv
    cos64 = jnp.concatenate([jnp.cos(ang_r), jnp.cos(ang_r), jnp.cos(ang_c), jnp.cos(ang_c)], axis=1)
    sin64 = jnp.concatenate([-jnp.sin(ang_r), jnp.sin(ang_r), -jnp.sin(ang_c), jnp.sin(ang_c)], axis=1)
    cos = jnp.concatenate([jnp.tile(cos64, (1, 2)), jnp.ones((tm, LANES), F32)], axis=0)
    sin = jnp.concatenate([jnp.tile(sin64, (1, 2)), jnp.zeros((tm, LANES), F32)], axis=0)
    return cos, sin


def _rope(q, k, cos, sin, tab_block, tm):
    t, w = q.shape
    row = pl.BlockSpec((tm, w), lambda i: (i, 0))
    tab = pl.BlockSpec((tm, LANES), lambda i: (tab_block(i), 0))
    return pl.pallas_call(
        _rope_kernel, grid=(t // tm,), in_specs=[row, row, tab, tab], out_specs=[row, row],
        out_shape=[jax.ShapeDtypeStruct((t, w), BF16)] * 2,
        compiler_params=_cparams("parallel"), name="rope",
    )(q, k, cos, sin)


def _attn_kernel(lam_ref, sub_ref, q_ref, k_ref, v_ref, o_ref, *, lam_init):
    lp = lam_ref[...]
    lam = (jnp.exp(jnp.sum(lp[0:1] * lp[1:2], keepdims=True))
           - jnp.exp(jnp.sum(lp[2:3] * lp[3:4], keepdims=True)) + lam_init)
    q = q_ref[...]
    k = k_ref[...]
    v = v_ref[...]
    lane = lax.broadcasted_iota(I32, q.shape, 1)
    outs = []
    for m in range(2):
        sel = (lane < DA_HEAD_DIM) if m == 0 else (lane >= DA_HEAD_DIM)
        s = _nt(jnp.where(sel, q, jnp.zeros_like(q)), k)
        p = jnp.exp(s - jnp.max(s, axis=-1, keepdims=True))
        denom = jnp.sum(p, axis=-1, keepdims=True)
        outs.append(jnp.dot(p.astype(BF16), v, preferred_element_type=F32) / denom)
    o = outs[0] - lam * outs[1]
    o = o * lax.rsqrt(jnp.mean(o * o, axis=-1, keepdims=True) + EPS) * sub_ref[...] * (1.0 - lam_init)
    o_ref[...] = o.astype(o_ref.dtype)


def _diff_attention(q, k, v, lamp, subln, lam_init, nb, lq, lk, q_row0, tq):
    w = q.shape[1]
    heads = w // DA_V_DIM
    nq = lq // tq
    qb0 = q_row0 // tq
    return pl.pallas_call(
        functools.partial(_attn_kernel, lam_init=lam_init),
        grid=(nb, heads, nq),
        in_specs=[_full(lamp.shape), _full((1, DA_V_DIM)),
                  pl.BlockSpec((tq, DA_V_DIM), lambda b, h, i: (qb0 + b * nq + i, h)),
                  pl.BlockSpec((lk, DA_V_DIM), lambda b, h, i: (b, h)),
                  pl.BlockSpec((lk, DA_V_DIM), lambda b, h, i: (b, h))],
        out_specs=pl.BlockSpec((tq, DA_V_DIM), lambda b, h, i: (b * nq + i, h)),
        out_shape=jax.ShapeDtypeStruct((nb * lq, w), BF16),
        compiler_params=_cparams("parallel", "parallel", "arbitrary"), name="diff_attention",
    )(lamp, subln.reshape(1, DA_V_DIM), q, k, v)


def _softplus(z):
    return jnp.maximum(z, 0.0) + jnp.log(1.0 + jnp.exp(-jnp.abs(z)))


def _rw_prep_kernel(u_ref, prev_ref, next_ref, mu_ref, w0_ref, w2_ref, a0_ref, a2_ref, g2_ref, kk_ref, ka_ref,
                    rk_ref, ones_ref, r_ref, v_ref, nkk_ref, g_ref, bonus_ref, lw_ref, kd_ref, bd_ref):
    u = u_ref[...]
    tm = u.shape[0]
    width = r_ref.shape[1]
    row = lax.broadcasted_iota(I32, u.shape, 0)
    up = jnp.where(row == 0, prev_ref[0], pltpu.roll(u, 1, 0))
    dn = jnp.where(row == tm - 1, next_ref[0], pltpu.roll(u, tm - 1, 0))
    u = u + mu_ref[...] * (0.5 * (up + dn) - u)
    r = u[:, :width]
    k = u[:, width:2 * width]
    v = u[:, 2 * width:3 * width]
    o = 3 * width
    w_in = u[:, o:o + LANES]
    a_in = u[:, o + LANES:o + 2 * LANES]
    g_in = u[:, o + 2 * LANES:o + 3 * LANES]
    ones = ones_ref[...]
    hsum = lambda t: jnp.dot(t, ones, preferred_element_type=F32, precision=HI)
    g = jnp.dot(jax.nn.sigmoid(g_in), g2_ref[...], preferred_element_type=F32, precision=HI)
    kk = k * kk_ref[...]
    kk = kk / jnp.maximum(jnp.sqrt(hsum(kk * kk)), 1e-12)
    w_log = -_softplus(-(w0_ref[...] + jnp.dot(jnp.tanh(w_in), w2_ref[...], preferred_element_type=F32,
                                               precision=HI))) - 0.5
    logw = -jnp.exp(w_log)
    a = jax.nn.sigmoid(a0_ref[...] + jnp.dot(a_in, a2_ref[...], preferred_element_type=F32, precision=HI))
    ksum = jnp.zeros_like(k)
    for d in range(2):
        a_d = a[:, d * width:(d + 1) * width]
        k_d = k * (1.0 + (a_d - 1.0) * ka_ref[...])
        ksum = ksum + k_d
        lw_ref[d] = logw[:, d * width:(d + 1) * width]
        kd_ref[d] = k_d
        bd_ref[d] = kk * a_d
    r_ref[...] = r
    v_ref[...] = v
    nkk_ref[...] = -kk
    g_ref[...] = g
    bonus_ref[...] = hsum(r * ksum * rk_ref[...]) * v


def _halo_rows(x, tm, seq_lens):
    t = x.shape[0]
    nt = t // tm
    starts = np.cumsum([0] + [n for n in seq_lens])[:-1]
    ends = np.cumsum(seq_lens)
    tile_start = np.arange(nt) * tm
    has_prev = ~np.isin(tile_start, starts)
    has_next = ~np.isin(tile_start + tm, ends)
    last = x[tm - 1::tm]
    first = x[0::tm]
    zero = jnp.zeros_like(first[:1])
    prev = jnp.concatenate([zero, last[:-1]], axis=0) * jnp.asarray(has_prev, x.dtype)[:, None]
    nxt = jnp.concatenate([first[1:], zero], axis=0) * jnp.asarray(has_next, x.dtype)[:, None]
    return prev[:, None, :], nxt[:, None, :]


def _block_diag2(m):
    z = jnp.zeros_like(m[0])
    return jnp.concatenate([jnp.concatenate([m[0], z], axis=1), jnp.concatenate([z, m[1]], axis=1)], axis=0)


def _head_ones(width, hd):
    idx = np.arange(width) // hd
    return jnp.asarray((idx[:, None] == idx[None, :]).astype(np.float32))


def _rwkv_prepare(u, seq_lens, tm, mu, w0, w2, a0, a2, g2, k_k, k_a, r_k):
    t, cols = u.shape
    width = k_k.shape[0]
    prev, nxt = _halo_rows(u, tm, seq_lens)
    row = lambda c: pl.BlockSpec((tm, c), lambda i: (i, 0))
    halo = pl.BlockSpec((1, 1, cols), lambda i: (i, 0, 0))
    dir_out = pl.BlockSpec((2, tm, width), lambda i: (0, i, 0))
    consts = [mu.reshape(1, cols), w0.reshape(1, 2 * width), _block_diag2(w2), a0.reshape(1, 2 * width),
              _block_diag2(a2), g2, k_k.reshape(1, width), k_a.reshape(1, width), r_k.reshape(1, width),
              _head_ones(width, RW_HEAD_DIM)]
    f = jax.ShapeDtypeStruct((t, width), F32)
    f2 = jax.ShapeDtypeStruct((2, t, width), F32)
    return pl.pallas_call(
        _rw_prep_kernel, grid=(t // tm,),
        in_specs=[row(cols), halo, halo] + [_full(c.shape) for c in consts],
        out_specs=[row(width)] * 5 + [dir_out] * 3,
        out_shape=[f] * 5 + [f2] * 3,
        compiler_params=_cparams("parallel"), name="rwkv_prepare",
    )(u, prev, nxt, *consts)


def _mm(x, y, passes):
    if passes == 6:
        return jnp.dot(x, y, preferred_element_type=F32, precision=HI)
    dot = lambda p, q: jnp.dot(p, q, preferred_element_type=F32)
    if passes == 1:
        return dot(x.astype(BF16), y.astype(BF16))
    xh, xl = _split_bf16(x)
    yh, yl = _split_bf16(y)
    return dot(xh, yh) + (dot(xh, yl) + dot(xl, yh))


def _rw_scan_kernel(r_ref, v_ref, nkk_ref, lw_ref, kd_ref, bd_ref, h0_ref, y_ref, hf_ref, h_scr):
    d = pl.program_id(1)
    c = pl.program_id(2)
    nc = pl.num_programs(2)
    cs = RW_CHUNK
    pairs = h_scr.shape[0]
    sgn = 1 - 2 * d

    @pl.when(c == 0)
    def _():
        h_scr[...] = h0_ref[0, 0]

    ri = lax.broadcasted_iota(I32, (cs, cs), 0)
    ci = lax.broadcasted_iota(I32, (cs, cs), 1)
    before_eq = jnp.where((ri - ci) * sgn >= 0, 1.0, 0.0).astype(BF16)
    lane = lax.broadcasted_iota(I32, (cs, LANES), 1)
    lo_half = lane < RW_HEAD_DIM
    n2 = 2 * cs
    rt = lax.broadcasted_iota(I32, (n2, n2), 0)
    ct = lax.broadcasted_iota(I32, (n2, n2), 1)
    dtok = ((rt & (cs - 1)) - (ct & (cs - 1))) * sgn
    strict = dtok > 0
    incl = dtok >= 0
    eye = rt == ct

    def stack2(x):
        return jnp.concatenate([jnp.where(lo_half, x, 0.0), jnp.where(lo_half, 0.0, x)], axis=0)

    mm = functools.partial(_mm, passes=RW_PASSES)
    mm_state = functools.partial(_mm, passes=RW_STATE_PASSES)

    h_in = [h_scr[p] for p in range(pairs)]
    ys, h_out = [], []
    for p in range(pairs):
        sl = slice(p * LANES, (p + 1) * LANES)
        logw = lw_ref[0, :, sl]
        r = r_ref[:, sl]
        v = v_ref[:, sl]
        a = nkk_ref[:, sl]
        k = kd_ref[0, :, sl]
        b = bd_ref[0, :, sl]

        lw_hi, lw_lo = _split_bf16(logw)
        cum = (jnp.dot(before_eq, lw_hi, preferred_element_type=F32)
               + jnp.dot(before_eq, lw_lo, preferred_element_type=F32))
        mid = cum[cs // 2:cs // 2 + 1]
        tot = jnp.sum(logw, axis=0, keepdims=True)
        e_in = jnp.exp(mid - cum)
        e_end = jnp.exp(tot - cum)

        a2 = stack2(a * jnp.exp(cum - logw - mid))
        r2 = stack2(r * jnp.exp(cum - mid))
        a2_abs = stack2(a * jnp.exp(cum - logw))
        r2_abs = stack2(r * jnp.exp(cum))
        b2 = stack2(b * e_in)
        k2 = stack2(k * e_in)
        v2 = stack2(v)
        bh2 = stack2(b * e_end)
        kh2 = stack2(k * e_end)

        b2t = b2.T
        k2t = k2.T
        nmat = jnp.where(strict, mm(a2, b2t), 0.0)
        mmat = jnp.where(strict, mm(a2, k2t), 0.0)
        qb = jnp.where(incl, mm(r2, b2t), 0.0)
        qk = jnp.where(incl, mm(r2, k2t), 0.0)

        tinv = jnp.where(eye, 1.0, 0.0) + nmat
        pw = nmat
        for _ in range(int(math.log2(cs)) - 1):
            pw = mm(pw, pw)
            tinv = tinv + mm(tinv, pw)

        w2 = mm(tinv, mm(mmat, v2))
        a2p = mm(tinv, a2_abs)
        y_intra = mm(qk, v2) + mm(qb, w2)
        r2p = r2_abs + mm(qb, a2p)
        bh2t = bh2.T
        gmat = jnp.where(eye, jnp.exp(tot), 0.0) + mm(bh2t, a2p)
        dmat = mm(bh2t, w2) + mm(kh2.T, v2)

        h = h_in[p]
        y2 = y_intra + mm_state(r2p, h)
        ys.append(y2[:cs] + y2[cs:])
        h_out.append(mm_state(gmat, h) + dmat)

    y_ref[0] = jnp.concatenate(ys, axis=1)
    for p in range(pairs):
        h_scr[p] = h_out[p]

    @pl.when(c == nc - 1)
    def _():
        hf_ref[0, 0] = h_scr[...]


def _rwkv_scan(r, v, nkk, lw, kd, bd, h0, nb, seq_len, row0):
    w = r.shape[1]
    pairs = w // LANES
    nc = seq_len // RW_CHUNK
    rb0 = row0 // RW_CHUNK

    def chunk(c, d):
        return c + d * (nc - 1 - 2 * c)

    shared = pl.BlockSpec((RW_CHUNK, w), lambda b, d, c: (rb0 + b * nc + chunk(c, d), 0))
    perdir = pl.BlockSpec((1, RW_CHUNK, w), lambda b, d, c: (d, rb0 + b * nc + chunk(c, d), 0))
    state = pl.BlockSpec((1, 1, pairs, LANES, LANES), lambda b, d, c: (d, b, 0, 0, 0))
    return pl.pallas_call(
        _rw_scan_kernel, grid=(nb, 2, nc),
        in_specs=[shared, shared, shared, perdir, perdir, perdir, state],
        out_specs=[pl.BlockSpec((1, RW_CHUNK, w), lambda b, d, c: (d, b * nc + chunk(c, d), 0)), state],
        out_shape=[jax.ShapeDtypeStruct((2, nb * seq_len, w), F32), jax.ShapeDtypeStruct(h0.shape, F32)],
        scratch_shapes=[pltpu.VMEM((pairs, LANES, LANES), F32)],
        compiler_params=_cparams("parallel", "parallel", "arbitrary"), name="rwkv_scan",
    )(r, v, nkk, lw, kd, bd, h0)


def _rw_post_kernel(y_ref, g_ref, bonus_ref, lnw_ref, lnb_ref, ones_ref, o_ref):
    y = y_ref[0] + y_ref[1]
    ones = ones_ref[...]
    hmean = lambda t: jnp.dot(t, ones, preferred_element_type=F32, precision=HI) * (1.0 / RW_HEAD_DIM)
    yc = y - hmean(y)
    var = hmean(yc * yc)
    yn = yc * lax.rsqrt(var + RW_GN_EPS) * lnw_ref[...] + lnb_ref[...]
    o_ref[...] = ((yn + bonus_ref[...]) * g_ref[...]).astype(o_ref.dtype)


def _rwkv_post(y, g, bonus, lnw, lnb, tm):
    _, t, w = y.shape
    row = pl.BlockSpec((tm, w), lambda i: (i, 0))
    return pl.pallas_call(
        _rw_post_kernel, grid=(t // tm,),
        in_specs=[pl.BlockSpec((2, tm, w), lambda i: (0, i, 0)), row, row, _full((1, w)), _full((1, w)),
                  _full((w, w))],
        out_specs=row, out_shape=jax.ShapeDtypeStruct((t, w), BF16),
        compiler_params=_cparams("parallel"), name="rwkv_post",
    )(y, g, bonus, lnw.reshape(1, w), lnb.reshape(1, w), _head_ones(w, RW_HEAD_DIM))


def _rwkv_mixer(u, seq_lens, nb, seq, ctx_len, tm, mu, w0, w2, a0, a2, g2, k_k, k_a, r_k, lnx_w, lnx_b):
    t_lat = nb * seq
    r, vv, nkk, g, bonus, lw, kd, bd = _rwkv_prepare(u, seq_lens, tm, mu, w0, w2, a0, a2, g2, k_k, k_a, r_k)
    zero = jnp.zeros((2, nb, r.shape[1] // LANES, LANES, LANES), F32)
    y_ctx, h_ctx = _rwkv_scan(r, vv, nkk, lw, kd, bd, zero, nb, ctx_len, t_lat)
    y_lat, _ = _rwkv_scan(r, vv, nkk, lw, kd, bd, h_ctx, nb, seq, 0)
    y = jnp.concatenate([y_lat, y_ctx], axis=1)
    return _rwkv_post(y, g, bonus, lnx_w, lnx_b, tm)


def _extract_topk(s, order, payload, count):
    big = float(2 ** 24)
    vals, pays = [], []
    for _ in range(count):
        m = jnp.max(s, axis=0, keepdims=True)
        first = jnp.min(jnp.where(s == m, order, big), axis=0, keepdims=True)
        hit = order == first
        vals.append(m)
        pays.append(first if payload is None else jnp.sum(jnp.where(hit, payload, 0.0), axis=0, keepdims=True))
        s = jnp.where(hit, -jnp.inf, s)
    return jnp.concatenate(vals, axis=0), jnp.concatenate(pays, axis=0)


def _pruned_candidates(v1, i1, v2, i2):
    k = PEER_TOPK
    tt = v1.shape[1]
    row8 = lax.broadcasted_iota(I32, (8, tt), 0).astype(F32)
    row16 = lax.broadcasted_iota(I32, (k, tt), 0).astype(F32)
    nk = float(PEER_NKEYS)
    sums, flats, eids = [], [], []

    def add(valid, s, flat, eid):
        unused = float(k * k + 16 * len(sums))
        sums.append(s if valid is None else jnp.where(valid, s, -jnp.inf))
        flats.append(flat if valid is None else jnp.where(valid, flat, flat + unused))
        eids.append(eid)

    def vary_j(i, rows, nvalid):
        r = row16 if rows == k else row8
        add(None if nvalid == rows else r < nvalid, v1[i:i + 1] + v2[:rows], r + float(i * k),
            i1[i:i + 1] * nk + i2[:rows])

    def vary_i(j, i0, lo, hi):
        r = row8 + float(i0)
        add(None if (lo == i0 and hi == i0 + 8) else (r >= lo) & (r < hi), v1[i0:i0 + 8] + v2[j:j + 1],
            r * float(k) + float(j), i1[i0:i0 + 8] * nk + i2[j:j + 1])

    vary_j(0, k, k)
    vary_j(1, 8, 8)
    vary_j(2, 8, 5)
    vary_j(3, 8, 4)
    vary_i(0, 8, 8, 16)
    vary_i(0, 0, 4, 8)
    vary_i(1, 0, 4, 8)
    vary_i(2, 0, 4, 5)
    return jnp.concatenate(sums, axis=0), jnp.concatenate(flats, axis=0), jnp.concatenate(eids, axis=0)


def _peer_topk_kernel(q_ref, keys_ref, idx_ref, gate_ref):
    tt = q_ref.shape[0]
    kpos = lax.broadcasted_iota(I32, (PEER_NKEYS, tt), 0).astype(F32)

    def head(h, carry):
        vs, ids = [], []
        for p in range(2):
            col = pl.multiple_of((2 * h + p) * LANES, LANES)
            s = _nt(keys_ref[h, p], q_ref[:, pl.ds(col, LANES)], HI)
            v_p, i_p = _extract_topk(s, kpos, None, PEER_TOPK)
            vs.append(v_p)
            ids.append(i_p)
        cand, flat, eid = _pruned_candidates(vs[0], ids[0], vs[1], ids[1])
        top_s, top_i = _extract_topk(cand, flat, eid, PEER_TOPK)
        e = jnp.exp(top_s - top_s[0:1])
        rows = pl.ds(pl.multiple_of(h * PEER_TOPK, PEER_TOPK), PEER_TOPK)
        gate_ref[rows, :] = e / jnp.sum(e, axis=0, keepdims=True)
        idx_ref[rows, :] = top_i.astype(I32)
        return carry

    lax.fori_loop(0, PEER_HEADS, head, 0)


def _peer_topk(q, keys, tt):
    t = q.shape[0]
    ne = PEER_HEADS * PEER_TOPK
    out = pl.BlockSpec((ne, tt), lambda i: (0, i))
    return pl.pallas_call(
        _peer_topk_kernel, grid=(t // tt,),
        in_specs=[pl.BlockSpec((tt, q.shape[1]), lambda i: (i, 0)), _full(keys.shape)],
        out_specs=[out, out],
        out_shape=[jax.ShapeDtypeStruct((ne, t), I32), jax.ShapeDtypeStruct((ne, t), F32)],
        compiler_params=_cparams("parallel"), name="peer_topk",
    )(q, keys)


def _sc_gather(table, idx):
    n = idx.shape[0]
    r = table.shape[1]
    workers = SC_CORES * SC_SUBCORES
    per_worker = n // workers
    nwin = per_worker // SC_WINDOW
    assert n == workers * nwin * SC_WINDOW and nwin % SC_NBUF == 0
    mesh = plsc.VectorSubcoreMesh(core_axis_name="c", subcore_axis_name="s")

    def body(table_hbm, idx_hbm, out_hbm, idx_v, *rest):
        bufs = rest[:SC_NBUF]
        gsem = rest[SC_NBUF:2 * SC_NBUF]
        osem = rest[2 * SC_NBUF:]
        base = (lax.axis_index("s") * SC_CORES + lax.axis_index("c")) * per_worker
        pltpu.sync_copy(idx_hbm.at[pl.ds(base, per_worker)], idx_v)

        def gather(w, b):
            return pltpu.make_async_copy(table_hbm.at[idx_v.at[pl.ds(w * SC_WINDOW, SC_WINDOW)]], bufs[b], gsem[b])

        def put(w, b):
            return pltpu.make_async_copy(bufs[b], out_hbm.at[pl.ds(base + w * SC_WINDOW, SC_WINDOW)], osem[b])

        for b in range(SC_NBUF):
            gather(b, b).start()

        @pl.loop(0, nwin, step=SC_NBUF)
        def _(w0):
            for b in range(SC_NBUF):
                w = w0 + b
                gather(w, b).wait()
                put(w, b).start()
                put(w, b).wait()

                @pl.when(w + SC_NBUF < nwin)
                def _():
                    gather(w + SC_NBUF, b).start()

    return pl.kernel(
        body, mesh=mesh, out_type=jax.ShapeDtypeStruct((n, r), table.dtype),
        scratch_types=[pltpu.VMEM((per_worker,), I32)] + [pltpu.VMEM((SC_WINDOW, r), table.dtype)] * SC_NBUF
        + [pltpu.SemaphoreType.DMA] * (2 * SC_NBUF),
    )(table, idx)


def _peer_apply_kernel(rows_ref, h_ref, gates_ref, x_ref, mod_ref, nf_ref, o_ref, *, final_norm):
    tt = h_ref.shape[0]
    ne = gates_ref.shape[1] // 2
    d = h_ref.shape[1]
    zeros_h = jnp.zeros((6, d), BF16)
    zeros_e = jnp.zeros((6, 2 * ne), BF16)
    outs = []
    for t in range(tt):
        uv = pltpu.bitcast(rows_ref[t * ne:(t + 1) * ne, :], BF16)
        hi, lo = _split_bf16(h_ref[t:t + 1, :])
        pre = _nt(jnp.concatenate([hi, lo, zeros_h], axis=0), uv)
        pre = pltpu.roll(pre, 1, 1)
        pre = pre[0:1] + pre[1:2]
        act = 0.5 * pre * (1.0 + lax.erf(pre * (2.0 ** -0.5)))
        whi, wlo = _split_bf16(gates_ref[t:t + 1, :] * act)
        o = jnp.dot(jnp.concatenate([whi, wlo, zeros_e], axis=0), uv, preferred_element_type=F32)
        outs.append(o[0:1] + o[1:2])
    y = x_ref[...] + mod_ref[0] * jnp.concatenate(outs, axis=0)
    if final_norm:
        y = y * lax.rsqrt(jnp.mean(y * y, axis=-1, keepdims=True) + EPS) * nf_ref[...]
    o_ref[...] = y


def _peer_apply(rows, h, gates, x, gate_mod, norm_f, token0, bid, final_norm):
    t, d = x.shape
    ne = gates.shape[1] // 2
    tt = PEER_TOK
    steps = rows.shape[0] // (tt * ne)
    blk0 = token0 // tt
    row = lambda c: pl.BlockSpec((tt, c), lambda i: (blk0 + i, 0))
    return pl.pallas_call(
        functools.partial(_peer_apply_kernel, final_norm=final_norm),
        grid=(steps,),
        in_specs=[pl.BlockSpec((tt * ne, d), lambda i: (i, 0)), row(d), row(2 * ne), row(d),
                  pl.BlockSpec((1, 1, d), lambda i: (bid(blk0 + i), 0, 0)), _full((1, d))],
        out_specs=row(d), out_shape=jax.ShapeDtypeStruct((t, d), F32),
        input_output_aliases={3: 0},
        compiler_params=_cparams("parallel"), name="peer_apply",
    )(rows, h, gates, x, gate_mod, norm_f.reshape(1, d))


def _peer(x, norm2, shift, scale, gate_mod, wq, keys, uv, norm_f, bid_fn, tm, final_norm):
    t = x.shape[0]
    q, h = _normmod_proj(x, norm2, shift, scale, [wq], [F32], bid_fn(tm), tm, want_h=True)
    idx_t, gates_t = _peer_topk(q, keys, LANES)
    ne = idx_t.shape[0]
    idx = idx_t.T.reshape(t * ne)
    gates = gates_t.T
    gates2 = jnp.stack([jnp.zeros_like(gates), gates], axis=-1).reshape(t, 2 * ne)
    per = t // PEER_CHUNKS
    for k in range(PEER_CHUNKS):
        rows = _sc_gather(uv, idx[k * per * ne:(k + 1) * per * ne])
        x = _peer_apply(rows, h, gates2, x, gate_mod, norm_f, k * per, bid_fn(PEER_TOK), final_norm)
    return x


def _conv_kernel(x_ref, prev_ref, next_ref, w_ref, b_ref, dtr_ref, dtb_ref, o_ref, dt_ref, *, heads):
    x = x_ref[...]
    tm = x.shape[0]
    row = lax.broadcasted_iota(I32, x.shape, 0)
    up = jnp.where(row == 0, prev_ref[0], pltpu.roll(x, 1, 0))
    dn = jnp.where(row == tm - 1, next_ref[0], pltpu.roll(x, tm - 1, 0))
    y = up * w_ref[0:1] + x * w_ref[1:2] + dn * w_ref[2:3] + b_ref[...]
    o_ref[...] = y * jax.nn.sigmoid(y)
    lane = lax.broadcasted_iota(I32, (tm, LANES), 1)
    for d in range(2):
        dt_ref[d] = jnp.where(lane < heads, _softplus(dtr_ref[d] + dtb_ref[d]), 0.0)


def _mamba_conv(xbc, dt_raw, seq_lens, tm, conv_w, conv_b, dt_bias_pad, heads):
    t, c = xbc.shape
    prev, nxt = _halo_rows(xbc, tm, seq_lens)
    row = pl.BlockSpec((tm, c), lambda i: (i, 0))
    halo = pl.BlockSpec((1, 1, c), lambda i: (i, 0, 0))
    dts = pl.BlockSpec((2, tm, LANES), lambda i: (0, i, 0))
    return pl.pallas_call(
        functools.partial(_conv_kernel, heads=heads), grid=(t // tm,),
        in_specs=[row, halo, halo, _full(conv_w.shape), _full((1, c)), dts, _full((2, 1, LANES))],
        out_specs=[row, dts],
        out_shape=[jax.ShapeDtypeStruct((t, c), F32), jax.ShapeDtypeStruct((2, t, LANES), F32)],
        compiler_params=_cparams("parallel"), name="mamba_conv",
    )(xbc, prev, nxt, conv_w, conv_b.reshape(1, c), dt_raw, dt_bias_pad)


def _ssd_kernel(*refs, reverse, inner, groups, add_prev):
    if add_prev:
        xbc_ref, dt_ref, dtt_ref, alr_ref, alc_ref, rep_ref, h0_ref, yin_ref, y_ref, hf_ref, h_scr = refs
    else:
        xbc_ref, dt_ref, dtt_ref, alr_ref, alc_ref, rep_ref, h0_ref, y_ref, hf_ref, h_scr = refs
        yin_ref = None
    c = pl.program_id(1)
    nc = pl.num_programs(1)
    cs = M_CHUNK
    gw = inner // groups
    hpg = gw // M_HEAD_DIM

    @pl.when(c == 0)
    def _():
        h_scr[...] = h0_ref[0]

    dt = dt_ref[0]
    dtt = dtt_ref[0]
    a = dt * (-jnp.exp(alr_ref[...]))
    at = dtt * (-jnp.exp(alc_ref[...]))
    ri = lax.broadcasted_iota(I32, (cs, cs), 0)
    ci = lax.broadcasted_iota(I32, (cs, cs), 1)
    incl = (ri <= ci) if reverse else (ri >= ci)
    tri = jnp.where(incl, 1.0, 0.0)
    cum = jnp.dot(tri, a, preferred_element_type=F32, precision=HI)
    cumt = _nt(at, tri, HI)
    tot = jnp.sum(a, axis=0, keepdims=True)

    rep = rep_ref[...]

    def spread(t):
        hi, lo = _split_bf16(t)
        return jnp.dot(hi, rep, preferred_element_type=F32) + jnp.dot(lo, rep, preferred_element_type=F32)

    e_cum = spread(jnp.exp(cum))
    e_end = spread(jnp.exp(tot - cum) * dt)
    e_tot = spread(jnp.broadcast_to(jnp.exp(tot), (8, LANES)))[0:1]

    lane = lax.broadcasted_iota(I32, (cs, LANES), 1)
    lo_half = lane < M_HEAD_DIM
    ys = []
    for g in range(groups):
        bg32 = xbc_ref[:, inner + g * M_STATE:inner + (g + 1) * M_STATE]
        bg = bg32.astype(BF16)
        cg = xbc_ref[:, inner + groups * M_STATE + g * M_STATE:inner + groups * M_STATE + (g + 1) * M_STATE]
        cg = cg.astype(BF16)
        cb = _nt(cg, bg)
        hprev = h_scr[g]
        xg = xbc_ref[:, g * gw:(g + 1) * gw]
        y_off = jnp.dot(cg, hprev.astype(BF16), preferred_element_type=F32) * e_cum[:, g * gw:(g + 1) * gw]
        xd = (xg * e_end[:, g * gw:(g + 1) * gw]).astype(BF16)
        h_scr[g] = e_tot[:, g * gw:(g + 1) * gw] * hprev + jnp.dot(bg32.T.astype(BF16), xd, preferred_element_type=F32)
        for j in range(hpg // 2):
            xpair = xg[:, j * LANES:(j + 1) * LANES].astype(BF16)
            halves = []
            for hh in range(2):
                h = g * hpg + 2 * j + hh
                seg = jnp.minimum(cum[:, h:h + 1] - cumt[h:h + 1, :], 0.0)
                m = jnp.where(incl, cb * jnp.exp(seg), 0.0) * dtt[h:h + 1, :]
                halves.append(jnp.dot(m.astype(BF16), xpair, preferred_element_type=F32))
            ys.append(jnp.where(lo_half, halves[0], halves[1]) + y_off[:, j * LANES:(j + 1) * LANES])
    y = jnp.concatenate(ys, axis=1)
    if add_prev:
        y = y + yin_ref[...]
    y_ref[...] = y

    @pl.when(c == nc - 1)
    def _():
        hf_ref[0] = h_scr[...]


def _ssd_pass(xbc, dt, dtt, a_log, h0, y_prev, nb, seq_len, row0, reverse, inner, groups, heads):
    nc = seq_len // M_CHUNK
    rb0 = row0 // M_CHUNK
    c_all = xbc.shape[1]
    gw = inner // groups
    alr = jnp.zeros((1, LANES), F32).at[0, :heads].set(a_log)
    alc = jnp.broadcast_to(jnp.zeros((LANES,), F32).at[:heads].set(a_log)[:, None], (LANES, LANES))
    hid = np.arange(inner) // M_HEAD_DIM
    rep = jnp.asarray((np.arange(LANES)[:, None] == hid[None, :]).astype(np.float32), BF16)
    chunk = (lambda c: nc - 1 - c) if reverse else (lambda c: c)
    add_prev = y_prev is not None
    in_specs = [pl.BlockSpec((M_CHUNK, c_all), lambda b, c: (rb0 + b * nc + chunk(c), 0)),
                pl.BlockSpec((1, M_CHUNK, LANES), lambda b, c: (0, rb0 + b * nc + chunk(c), 0)),
                pl.BlockSpec((1, LANES, M_CHUNK), lambda b, c: (0, 0, rb0 + b * nc + chunk(c))),
                _full((1, LANES)), _full((LANES, LANES)), _full((LANES, inner)),
                pl.BlockSpec((1, groups, M_STATE, gw), lambda b, c: (b, 0, 0, 0))]
    args = [xbc, dt, dtt, alr, alc, rep, h0]
    yspec = pl.BlockSpec((M_CHUNK, inner), lambda b, c: (b * nc + chunk(c), 0))
    if add_prev:
        in_specs.append(yspec)
        args.append(y_prev)
    return pl.pallas_call(
        functools.partial(_ssd_kernel, reverse=reverse, inner=inner, groups=groups, add_prev=add_prev),
        grid=(nb, nc), in_specs=in_specs,
        out_specs=[yspec, pl.BlockSpec((1, groups, M_STATE, gw), lambda b, c: (b, 0, 0, 0))],
        out_shape=[jax.ShapeDtypeStruct((nb * seq_len, inner), F32), jax.ShapeDtypeStruct(h0.shape, F32)],
        scratch_shapes=[pltpu.VMEM((groups, M_STATE, gw), F32)],
        compiler_params=_cparams("parallel", "arbitrary"), name="ssd_pass",
    )(*args)


def _mamba_gate_kernel(y_ref, x_ref, z_ref, dsk_ref, gn_ref, o_ref, *, groups):
    z = z_ref[...]
    y = (y_ref[...] + dsk_ref[...] * x_ref[...]) * (z * jax.nn.sigmoid(z))
    gw = y.shape[1] // groups
    for g in range(groups):
        yg = y[:, g * gw:(g + 1) * gw]
        yg = yg * lax.rsqrt(jnp.mean(yg * yg, axis=-1, keepdims=True) + EPS) * gn_ref[:, g * gw:(g + 1) * gw]
        o_ref[:, g * gw:(g + 1) * gw] = yg.astype(o_ref.dtype)


def _mamba_gate(y, xbc, z, d_skip_cols, gnorm, groups, tm):
    t, inner = y.shape
    row = pl.BlockSpec((tm, inner), lambda i: (i, 0))
    return pl.pallas_call(
        functools.partial(_mamba_gate_kernel, groups=groups), grid=(t // tm,),
        in_specs=[row, row, row, _full((1, inner)), _full((1, inner))],
        out_specs=row, out_shape=jax.ShapeDtypeStruct((t, inner), BF16),
        compiler_params=_cparams("parallel"), name="mamba_gate",
    )(y, xbc, z, d_skip_cols.reshape(1, inner), gnorm.reshape(1, inner))


def _even_layer(xs, mods, nb, seq, ctx_len, tm, bid_fn, norm1, norm2, win, da_lambda, da_subln, rw_mu, rw_w0,
                rw_w2, rw_a0, rw_a2, rw_g2, rw_kk, rw_ka, rw_rk, rw_lnx_w, rw_lnx_b, wout, peer_q, peer_keys,
                peer_uv, lam_init):
    d = xs.shape[1]
    t_lat = nb * seq
    da_w = d // 2
    rw_w = d - da_w
    winb = win.astype(BF16)
    ws = [winb[:, :da_w], winb[:, da_w:2 * da_w], winb[:, 2 * da_w:3 * da_w], winb[:, 3 * da_w:]]
    q, k, v, u = _normmod_proj(xs, norm1, mods[0], mods[1], ws, [F32, F32, BF16, F32], bid_fn(tm), tm)

    cos, sin = _rope_tables(seq, tm)
    lat_tiles = t_lat // tm
    tab_block = lambda i: jnp.where(i < lat_tiles, i % (seq // tm), seq // tm)
    qr, kr = _rope(q, k, cos, sin, tab_block, tm)
    lk = ctx_len + seq
    cat = lambda a: jnp.concatenate([a[t_lat:].reshape(nb, ctx_len, da_w), a[:t_lat].reshape(nb, seq, da_w)],
                                    axis=1).reshape(nb * lk, da_w)
    tq = min(256, seq)
    o_lat = _diff_attention(qr, cat(kr), cat(v), da_lambda, da_subln, lam_init, nb, seq, lk, 0, tq)
    tqc = min(256, ctx_len)
    o_ctx = _diff_attention(qr, kr[t_lat:], v[t_lat:], da_lambda, da_subln, lam_init, nb, ctx_len, ctx_len,
                            t_lat, tqc)
    o_att = jnp.concatenate([o_lat, o_ctx], axis=0)

    seq_lens = [seq] * nb + [ctx_len] * nb
    o_rw = _rwkv_mixer(u, seq_lens, nb, seq, ctx_len, tm, rw_mu, rw_w0, rw_w2, rw_a0, rw_a2, rw_g2, rw_kk, rw_ka,
                       rw_rk, rw_lnx_w, rw_lnx_b)

    woutb = wout.astype(BF16)
    xs = _proj_residual([o_att, o_rw], [woutb[:da_w], woutb[da_w:]], xs, mods[2], bid_fn(tm), tm)
    return _peer(xs, norm2, mods[3], mods[4], mods[5], peer_q.astype(BF16), peer_keys, peer_uv, norm2, bid_fn, tm,
                 False)


def _odd_layer_last(xs, mods, nb, seq, ctx_len, tm, bid_fn, norm1, norm2, win, conv_w, conv_b, dt_bias, a_log,
                    d_skip, gnorm, wout, peer_q, peer_keys, peer_uv, norm_f):
    d = xs.shape[1]
    t_lat = nb * seq
    inner = wout.shape[0]
    heads = a_log.shape[1]
    conv_dim = conv_w.shape[1]
    groups = (conv_dim - inner) // (2 * M_STATE)
    winb = win.astype(BF16)
    pad = jnp.zeros((d, LANES - heads), BF16)
    w_dt = [jnp.concatenate([winb[:, inner + conv_dim + k * heads:inner + conv_dim + (k + 1) * heads], pad], axis=1)
            for k in range(2)]
    ws = [winb[:, :inner], winb[:, inner:inner + conv_dim]] + w_dt
    z, xbc_raw, dtr_f, dtr_b = _normmod_proj(xs, norm1, mods[0], mods[1], ws, [F32] * 4, bid_fn(tm), tm)
    seq_lens = [seq] * nb + [ctx_len] * nb
    dtb = jnp.zeros((2, 1, LANES), F32).at[:, 0, :heads].set(dt_bias)
    xbc, dt = _mamba_conv(xbc_raw, jnp.stack([dtr_f, dtr_b]), seq_lens, tm, conv_w, conv_b, dtb, heads)
    dtt = jnp.swapaxes(dt, 1, 2)
    h0 = jnp.zeros((nb, groups, M_STATE, inner // groups), F32)
    ssd = functools.partial(_ssd_pass, xbc, inner=inner, groups=groups, heads=heads)
    _, hf = ssd(dt[0:1], dtt[0:1], a_log[0], h0, None, nb, ctx_len, t_lat, False)
    _, hb = ssd(dt[1:2], dtt[1:2], a_log[1], h0, None, nb, ctx_len, t_lat, True)
    y, _ = ssd(dt[0:1], dtt[0:1], a_log[0], hf, None, nb, seq, 0, False)
    y, _ = ssd(dt[1:2], dtt[1:2], a_log[1], hb, y, nb, seq, 0, True)
    x_lat = xs[:t_lat]
    gated = _mamba_gate(y, xbc, z, jnp.repeat(d_skip, M_HEAD_DIM), gnorm, groups, tm)
    x_lat = _proj_residual([gated], [wout.astype(BF16)], x_lat, mods[2], bid_fn(tm), tm)
    return _peer(x_lat, norm2, mods[3], mods[4], mods[5], peer_q.astype(BF16), peer_keys, peer_uv, norm_f, bid_fn,
                 tm, True)


def kernel(x, c, ctx, c_ctx, ada_w_0, ada_b_0, norm1_0, norm2_0, win_0, da_lambda_0, da_subln_0, rw_mu_0, rw_w0_0, rw_w2_0, rw_a0_0, rw_a2_0, rw_g2_0, rw_kk_0, rw_ka_0, rw_rk_0, rw_lnx_w_0, rw_lnx_b_0, wout_0, peer_q_0, peer_keys_0, peer_u_0, peer_v_0, ada_w_1, ada_b_1, norm1_1, norm2_1, win_1, conv_w_1, conv_b_1, dt_bias_1, a_log_1, d_skip_1, gnorm_1, wout_1, peer_q_1, peer_keys_1, peer_u_1, peer_v_1, norm_f):
    nb, seq, d = x.shape
    ctx_len = ctx.shape[1]
    tm = 256 if (seq % 256 == 0 and ctx_len % 256 == 0) else 128
    assert seq % tm == 0 and ctx_len % tm == 0 and seq % GRID_W == 0

    def bid_fn(tile):
        per = seq // tile
        return lambda i: jnp.minimum(i // per, nb)

    xs = jnp.concatenate([x.reshape(nb * seq, d), ctx.reshape(nb * ctx_len, d)], axis=0)
    cvecs = jnp.zeros((16, d), F32).at[:nb].set(c).at[nb].set(c_ctx)

    def expert_rows(u, v):
        bits = lambda a: lax.bitcast_convert_type(a.astype(BF16), jnp.uint16).astype(jnp.uint32)
        return bits(u) | (bits(v) << 16)

    mods0 = _ada_mod(cvecs, ada_w_0, ada_b_0)
    uv0 = expert_rows(peer_u_0, peer_v_0)
    xs = _even_layer(xs, mods0, nb, seq, ctx_len, tm, bid_fn, norm1_0, norm2_0, win_0, da_lambda_0, da_subln_0,
                     rw_mu_0, rw_w0_0, rw_w2_0, rw_a0_0, rw_a2_0, rw_g2_0, rw_kk_0, rw_ka_0, rw_rk_0, rw_lnx_w_0,
                     rw_lnx_b_0, wout_0, peer_q_0, peer_keys_0, uv0, 0.8 - 0.6 * math.exp(-0.3 * 0))

    mods1 = _ada_mod(cvecs, ada_w_1, ada_b_1)
    uv1 = expert_rows(peer_u_1, peer_v_1)
    out = _odd_layer_last(xs, mods1, nb, seq, ctx_len, tm, bid_fn, norm1_1, norm2_1, win_1, conv_w_1, conv_b_1,
                          dt_bias_1, a_log_1, d_skip_1, gnorm_1, wout_1, peer_q_1, peer_keys_1, uv1, norm_f)
    return out.reshape(nb, seq, d)
```

```python
import functools
import math

import jax
import jax.numpy as jnp
import numpy as np
from jax import lax
from jax.experimental import pallas as pl
from jax.experimental.pallas import tpu as pltpu
from jax.experimental.pallas import tpu_sc as plsc

F32 = jnp.float32
BF16 = jnp.bfloat16
I32 = jnp.int32
HI = lax.Precision.HIGHEST

EPS = 1e-6
N_MOD = 6
GRID_W = 64
LANES = 128
VMEM_LIMIT_BYTES = 48 * 1024 * 1024

DA_HEAD_DIM = 64
DA_V_DIM = 128
ROPE_BASE = 10000.0
ROPE_NFREQ = DA_HEAD_DIM // 4
RW_HEAD_DIM = 64
RW_GN_EPS = 64e-5
RW_CHUNK = 64
RW_PASSES = 1
RW_STATE_PASSES = 3
M_HEAD_DIM = 64
M_STATE = 128
M_CHUNK = 128
PEER_HEADS = 8
PEER_NKEYS = 128
PEER_TOPK = 16
PEER_TOK = 8
PEER_CHUNKS = 8
SC_CORES = 2
SC_SUBCORES = 16
SC_WINDOW = 16
SC_NBUF = 4


def _cparams(*sem):
    return pltpu.CompilerParams(dimension_semantics=sem, vmem_limit_bytes=VMEM_LIMIT_BYTES)


def _nt(a, b, precision=None):
    return lax.dot_general(a, b, (((1,), (1,)), ((), ())), preferred_element_type=F32, precision=precision)


def _full(shape):
    nd = len(shape)
    return pl.BlockSpec(shape, lambda *_: (0,) * nd)


def _split_bf16(x):
    hi = x.astype(BF16)
    lo = (x - hi.astype(F32)).astype(BF16)
    return hi, lo


def _ada_kernel(c_ref, w_ref, b_ref, o_ref):
    c = c_ref[...]
    s = c * jax.nn.sigmoid(c)
    o_ref[...] = jnp.dot(s, w_ref[...], preferred_element_type=F32, precision=HI) + b_ref[...]


def _ada_mod(cvecs, w, b):
    r, d = cvecs.shape
    n = w.shape[1]
    tn = 1024
    m = pl.pallas_call(
        _ada_kernel,
        grid=(n // tn,),
        in_specs=[_full((r, d)), pl.BlockSpec((d, tn), lambda j: (0, j)), pl.BlockSpec((1, tn), lambda j: (0, j))],
        out_specs=pl.BlockSpec((r, tn), lambda j: (0, j)),
        out_shape=jax.ShapeDtypeStruct((r, n), F32),
        compiler_params=_cparams("parallel"),
        name="ada_mod",
    )(cvecs, w, b.reshape(1, n))
    return [m[:, k * d:(k + 1) * d].reshape(r, 1, d) for k in range(N_MOD)]


def _normmod_kernel(x_ref, g_ref, sh_ref, sc_ref, *refs, n_w, want_h):
    x = x_ref[...]
    y = x * lax.rsqrt(jnp.mean(x * x, axis=-1, keepdims=True) + EPS) * g_ref[...]
    h = y * (1.0 + sc_ref[0]) + sh_ref[0]
    hb = h.astype(BF16)
    for w_ref, o_ref in zip(refs[:n_w], refs[n_w:2 * n_w]):
        o_ref[...] = jnp.dot(hb, w_ref[...], preferred_element_type=F32).astype(o_ref.dtype)
    if want_h:
        refs[2 * n_w][...] = h


def _normmod_proj(x, g, shift, scale, ws, out_dtypes, bid, tm, want_h=False, row0=0, nrows=None):
    d = x.shape[1]
    t = x.shape[0] if nrows is None else nrows
    blk0 = row0 // tm
    n_w = len(ws)
    in_specs = [pl.BlockSpec((tm, d), lambda i: (blk0 + i, 0)), _full((1, d)),
                pl.BlockSpec((1, 1, d), lambda i: (bid(blk0 + i), 0, 0)),
                pl.BlockSpec((1, 1, d), lambda i: (bid(blk0 + i), 0, 0))]
    in_specs += [_full(w.shape) for w in ws]
    out_specs = [pl.BlockSpec((tm, w.shape[1]), lambda i: (i, 0)) for w in ws]
    out_shape = [jax.ShapeDtypeStruct((t, w.shape[1]), dt) for w, dt in zip(ws, out_dtypes)]
    if want_h:
        out_specs.append(pl.BlockSpec((tm, d), lambda i: (i, 0)))
        out_shape.append(jax.ShapeDtypeStruct((t, d), F32))
    return pl.pallas_call(
        functools.partial(_normmod_kernel, n_w=n_w, want_h=want_h),
        grid=(t // tm,), in_specs=in_specs, out_specs=out_specs, out_shape=out_shape,
        compiler_params=_cparams("parallel"), name="normmod_proj",
    )(x, g.reshape(1, d), shift, scale, *ws)


def _proj_res_kernel(*refs, n_a):
    a_refs = refs[:n_a]
    w_refs = refs[n_a:2 * n_a]
    res_ref, gate_ref, o_ref = refs[2 * n_a:]
    acc = jnp.dot(a_refs[0][...], w_refs[0][...], preferred_element_type=F32)
    for a_ref, w_ref in zip(a_refs[1:], w_refs[1:]):
        acc += jnp.dot(a_ref[...], w_ref[...], preferred_element_type=F32)
    o_ref[...] = res_ref[...] + gate_ref[0] * acc


def _proj_residual(a_list, w_list, res, gate, bid, tm):
    t, n = res.shape
    n_a = len(a_list)
    in_specs = [pl.BlockSpec((tm, a.shape[1]), lambda i: (i, 0)) for a in a_list]
    in_specs += [_full(w.shape) for w in w_list]
    in_specs += [pl.BlockSpec((tm, n), lambda i: (i, 0)), pl.BlockSpec((1, 1, n), lambda i: (bid(i), 0, 0))]
    return pl.pallas_call(
        functools.partial(_proj_res_kernel, n_a=n_a),
        grid=(t // tm,), in_specs=in_specs, out_specs=pl.BlockSpec((tm, n), lambda i: (i, 0)),
        out_shape=jax.ShapeDtypeStruct((t, n), F32),
        compiler_params=_cparams("parallel"), name="proj_residual",
    )(*a_list, *w_list, res, gate)


def _rope_kernel(q_ref, k_ref, c_ref, s_ref, qo_ref, ko_ref):
    c = c_ref[...]
    s = s_ref[...]
    lane = lax.broadcasted_iota(I32, c.shape, 1)
    first = (lane % 32) < 16
    width = q_ref.shape[1]

    def rot(x):
        partner = jnp.where(first, pltpu.roll(x, LANES - 16, 1), pltpu.roll(x, 16, 1))
        return x * c + partner * s

    for g in range(width // LANES):
        sl = slice(g * LANES, (g + 1) * LANES)
        qo_ref[:, sl] = (rot(q_ref[:, sl]) * (DA_HEAD_DIM ** -0.5)).astype(qo_ref.dtype)
        ko_ref[:, sl] = rot(k_ref[:, sl]).astype(ko_ref.dtype)


def _rope_tables(seq_len, tm):
    rows = seq_len // GRID_W
    row = jnp.repeat(jnp.arange(rows, dtype=F32), GRID_W)
    col = (jnp.arange(seq_len) % GRID_W).astype(F32)
    inv = ROPE_BASE ** (-jnp.arange(ROPE_NFREQ, dtype=F32) / ROPE_NFREQ)
    ang_r = row[:, None] * inv
    ang_c = col[:, None] * inv
    cos64 = jnp.concatenate([jnp.cos(ang_r), jnp.cos(ang_r), jnp.cos(ang_c), jnp.cos(ang_c)], axis=1)
    sin64 = jnp.concatenate([-jnp.sin(ang_r), jnp.sin(ang_r), -jnp.sin(ang_c), jnp.sin(ang_c)], axis=1)
    cos = jnp.concatenate([jnp.tile(cos64, (1, 2)), jnp.ones((tm, LANES), F32)], axis=0)
    sin = jnp.concatenate([jnp.tile(sin64, (1, 2)), jnp.zeros((tm, LANES), F32)], axis=0)
    return cos, sin


def _rope(q, k, cos, sin, tab_block, tm):
    t, w = q.shape
    row = pl.BlockSpec((tm, w), lambda i: (i, 0))
    tab = pl.BlockSpec((tm, LANES), lambda i: (tab_block(i), 0))
    return pl.pallas_call(
        _rope_kernel, grid=(t // tm,), in_specs=[row, row, tab, tab], out_specs=[row, row],
        out_shape=[jax.ShapeDtypeStruct((t, w), BF16)] * 2,
        compiler_params=_cparams("parallel"), name="rope",
    )(q, k, cos, sin)


def _attn_kernel(lam_ref, sub_ref, q_ref, k_ref, v_ref, o_ref, *, lam_init):
    lp = lam_ref[...]
    lam = (jnp.exp(jnp.sum(lp[0:1] * lp[1:2], keepdims=True))
           - jnp.exp(jnp.sum(lp[2:3] * lp[3:4], keepdims=True)) + lam_init)
    q = q_ref[...]
    k = k_ref[...]
    v = v_ref[...]
    lane = lax.broadcasted_iota(I32, q.shape, 1)
    outs = []
    for m in range(2):
        sel = (lane < DA_HEAD_DIM) if m == 0 else (lane >= DA_HEAD_DIM)
        s = _nt(jnp.where(sel, q, jnp.zeros_like(q)), k)
        p = jnp.exp(s - jnp.max(s, axis=-1, keepdims=True))
        denom = jnp.sum(p, axis=-1, keepdims=True)
        outs.append(jnp.dot(p.astype(BF16), v, preferred_element_type=F32) / denom)
    o = outs[0] - lam * outs[1]
    o = o * lax.rsqrt(jnp.mean(o * o, axis=-1, keepdims=True) + EPS) * sub_ref[...] * (1.0 - lam_init)
    o_ref[...] = o.astype(o_ref.dtype)


def _diff_attention(q, k, v, lamp, subln, lam_init, nb, lq, lk, q_row0, tq):
    w = q.shape[1]
    heads = w // DA_V_DIM
    nq = lq // tq
    qb0 = q_row0 // tq
    return pl.pallas_call(
        functools.partial(_attn_kernel, lam_init=lam_init),
        grid=(nb, heads, nq),
        in_specs=[_full(lamp.shape), _full((1, DA_V_DIM)),
                  pl.BlockSpec((tq, DA_V_DIM), lambda b, h, i: (qb0 + b * nq + i, h)),
                  pl.BlockSpec((lk, DA_V_DIM), lambda b, h, i: (b, h)),
                  pl.BlockSpec((lk, DA_V_DIM), lambda b, h, i: (b, h))],
        out_specs=pl.BlockSpec((tq, DA_V_DIM), lambda b, h, i: (b * nq + i, h)),
        out_shape=jax.ShapeDtypeStruct((nb * lq, w), BF16),
        compiler_params=_cparams("parallel", "parallel", "arbitrary"), name="diff_attention",
    )(lamp, subln.reshape(1, DA_V_DIM), q, k, v)


def _softplus(z):
    return jnp.maximum(z, 0.0) + jnp.log(1.0 + jnp.exp(-jnp.abs(z)))


def _rw_prep_kernel(u_ref, prev_ref, next_ref, mu_ref, w0_ref, w2_ref, a0_ref, a2_ref, g2_ref, kk_ref, ka_ref,
                    rk_ref, ones_ref, r_ref, v_ref, nkk_ref, g_ref, bonus_ref, lw_ref, kd_ref, bd_ref):
    u = u_ref[...]
    tm = u.shape[0]
    width = r_ref.shape[1]
    row = lax.broadcasted_iota(I32, u.shape, 0)
    up = jnp.where(row == 0, prev_ref[0], pltpu.roll(u, 1, 0))
    dn = jnp.where(row == tm - 1, next_ref[0], pltpu.roll(u, tm - 1, 0))
    u = u + mu_ref[...] * (0.5 * (up + dn) - u)
    r = u[:, :width]
    k = u[:, width:2 * width]
    v = u[:, 2 * width:3 * width]
    o = 3 * width
    w_in = u[:, o:o + LANES]
    a_in = u[:, o + LANES:o + 2 * LANES]
    g_in = u[:, o + 2 * LANES:o + 3 * LANES]
    ones = ones_ref[...]
    hsum = lambda t: jnp.dot(t, ones, preferred_element_type=F32, precision=HI)
    g = jnp.dot(jax.nn.sigmoid(g_in), g2_ref[...], preferred_element_type=F32, precision=HI)
    kk = k * kk_ref[...]
    kk = kk / jnp.maximum(jnp.sqrt(hsum(kk * kk)), 1e-12)
    w_log = -_softplus(-(w0_ref[...] + jnp.dot(jnp.tanh(w_in), w2_ref[...], preferred_element_type=F32,
                                               precision=HI))) - 0.5
    logw = -jnp.exp(w_log)
    a = jax.nn.sigmoid(a0_ref[...] + jnp.dot(a_in, a2_ref[...], preferred_element_type=F32, precision=HI))
    ksum = jnp.zeros_like(k)
    for d in range(2):
        a_d = a[:, d * width:(d + 1) * width]
        k_d = k * (1.0 + (a_d - 1.0) * ka_ref[...])
        ksum = ksum + k_d
        lw_ref[d] = logw[:, d * width:(d + 1) * width]
        kd_ref[d] = k_d
        bd_ref[d] = kk * a_d
    r_ref[...] = r
    v_ref[...] = v
    nkk_ref[...] = -kk
    g_ref[...] = g
    bonus_ref[...] = hsum(r * ksum * rk_ref[...]) * v


def _halo_rows(x, tm, seq_lens):
    t = x.shape[0]
    nt = t // tm
    starts = np.cumsum([0] + [n for n in seq_lens])[:-1]
    ends = np.cumsum(seq_lens)
    tile_start = np.arange(nt) * tm
    has_prev = ~np.isin(tile_start, starts)
    has_next = ~np.isin(tile_start + tm, ends)
    last = x[tm - 1::tm]
    first = x[0::tm]
    zero = jnp.zeros_like(first[:1])
    prev = jnp.concatenate([zero, last[:-1]], axis=0) * jnp.asarray(has_prev, x.dtype)[:, None]
    nxt = jnp.concatenate([first[1:], zero], axis=0) * jnp.asarray(has_next, x.dtype)[:, None]
    return prev[:, None, :], nxt[:, None, :]


def _block_diag2(m):
    z = jnp.zeros_like(m[0])
    return jnp.concatenate([jnp.concatenate([m[0], z], axis=1), jnp.concatenate([z, m[1]], axis=1)], axis=0)


def _head_ones(width, hd):
    idx = np.arange(width) // hd
    return jnp.asarray((idx[:, None] == idx[None, :]).astype(np.float32))


def _rwkv_prepare(u, seq_lens, tm, mu, w0, w2, a0, a2, g2, k_k, k_a, r_k):
    t, cols = u.shape
    width = k_k.shape[0]
    prev, nxt = _halo_rows(u, tm, seq_lens)
    row = lambda c: pl.BlockSpec((tm, c), lambda i: (i, 0))
    halo = pl.BlockSpec((1, 1, cols), lambda i: (i, 0, 0))
    dir_out = pl.BlockSpec((2, tm, width), lambda i: (0, i, 0))
    consts = [mu.reshape(1, cols), w0.reshape(1, 2 * width), _block_diag2(w2), a0.reshape(1, 2 * width),
              _block_diag2(a2), g2, k_k.reshape(1, width), k_a.reshape(1, width), r_k.reshape(1, width),
              _head_ones(width, RW_HEAD_DIM)]
    f = jax.ShapeDtypeStruct((t, width), F32)
    f2 = jax.ShapeDtypeStruct((2, t, width), F32)
    return pl.pallas_call(
        _rw_prep_kernel, grid=(t // tm,),
        in_specs=[row(cols), halo, halo] + [_full(c.shape) for c in consts],
        out_specs=[row(width)] * 5 + [dir_out] * 3,
        out_shape=[f] * 5 + [f2] * 3,
        compiler_params=_cparams("parallel"), name="rwkv_prepare",
    )(u, prev, nxt, *consts)


def _mm(x, y, passes):
    if passes == 6:
        return jnp.dot(x, y, preferred_element_type=F32, precision=HI)
    dot = lambda p, q: jnp.dot(p, q, preferred_element_type=F32)
    if passes == 1:
        return dot(x.astype(BF16), y.astype(BF16))
    xh, xl = _split_bf16(x)
    yh, yl = _split_bf16(y)
    return dot(xh, yh) + (dot(xh, yl) + dot(xl, yh))


def _rw_scan_kernel(r_ref, v_ref, nkk_ref, lw_ref, kd_ref, bd_ref, h0_ref, y_ref, hf_ref, h_scr):
    d = pl.program_id(1)
    c = pl.program_id(2)
    nc = pl.num_programs(2)
    cs = RW_CHUNK
    pairs = h_scr.shape[0]
    sgn = 1 - 2 * d

    @pl.when(c == 0)
    def _():
        h_scr[...] = h0_ref[0, 0]

    ri = lax.broadcasted_iota(I32, (cs, cs), 0)
    ci = lax.broadcasted_iota(I32, (cs, cs), 1)
    before_eq = jnp.where((ri - ci) * sgn >= 0, 1.0, 0.0).astype(BF16)
    lane = lax.broadcasted_iota(I32, (cs, LANES), 1)
    lo_half = lane < RW_HEAD_DIM
    n2 = 2 * cs
    rt = lax.broadcasted_iota(I32, (n2, n2), 0)
    ct = lax.broadcasted_iota(I32, (n2, n2), 1)
    dtok = ((rt & (cs - 1)) - (ct & (cs - 1))) * sgn
    strict = dtok > 0
    incl = dtok >= 0
    eye = rt == ct

    def stack2(x):
        return jnp.concatenate([jnp.where(lo_half, x, 0.0), jnp.where(lo_half, 0.0, x)], axis=0)

    mm = functools.partial(_mm, passes=RW_PASSES)
    mm_state = functools.partial(_mm, passes=RW_STATE_PASSES)

    h_in = [h_scr[p] for p in range(pairs)]
    ys, h_out = [], []
    for p in range(pairs):
        sl = slice(p * LANES, (p + 1) * LANES)
        logw = lw_ref[0, :, sl]
        r = r_ref[:, sl]
        v = v_ref[:, sl]
        a = nkk_ref[:, sl]
        k = kd_ref[0, :, sl]
        b = bd_ref[0, :, sl]

        lw_hi, lw_lo = _split_bf16(logw)
        cum = (jnp.dot(before_eq, lw_hi, preferred_element_type=F32)
               + jnp.dot(before_eq, lw_lo, preferred_element_type=F32))
        mid = cum[cs // 2:cs // 2 + 1]
        tot = jnp.sum(logw, axis=0, keepdims=True)
        e_in = jnp.exp(mid - cum)
        e_end = jnp.exp(tot - cum)

        a2 = stack2(a * jnp.exp(cum - logw - mid))
        r2 = stack2(r * jnp.exp(cum - mid))
        a2_abs = stack2(a * jnp.exp(cum - logw))
        r2_abs = stack2(r * jnp.exp(cum))
        b2 = stack2(b * e_in)
        k2 = stack2(k * e_in)
        v2 = stack2(v)
        bh2 = stack2(b * e_end)
        kh2 = stack2(k * e_end)

        b2t = b2.T
        k2t = k2.T
        nmat = jnp.where(strict, mm(a2, b2t), 0.0)
        mmat = jnp.where(strict, mm(a2, k2t), 0.0)
        qb = jnp.where(incl, mm(r2, b2t), 0.0)
        qk = jnp.where(incl, mm(r2, k2t), 0.0)

        tinv = jnp.where(eye, 1.0, 0.0) + nmat
        pw = nmat
        for _ in range(int(math.log2(cs)) - 1):
            pw = mm(pw, pw)
            tinv = tinv + mm(tinv, pw)

        w2 = mm(tinv, mm(mmat, v2))
        a2p = mm(tinv, a2_abs)
        y_intra = mm(qk, v2) + mm(qb, w2)
        r2p = r2_abs + mm(qb, a2p)
        bh2t = bh2.T
        gmat = jnp.where(eye, jnp.exp(tot), 0.0) + mm(bh2t, a2p)
        dmat = mm(bh2t, w2) + mm(kh2.T, v2)

        h = h_in[p]
        y2 = y_intra + mm_state(r2p, h)
        ys.append(y2[:cs] + y2[cs:])
        h_out.append(mm_state(gmat, h) + dmat)

    y_ref[0] = jnp.concatenate(ys, axis=1)
    for p in range(pairs):
        h_scr[p] = h_out[p]

    @pl.when(c == nc - 1)
    def _():
        hf_ref[0, 0] = h_scr[...]


def _rwkv_scan(r, v, nkk, lw, kd, bd, h0, nb, seq_len, row0):
    w = r.shape[1]
    pairs = w // LANES
    nc = seq_len // RW_CHUNK
    rb0 = row0 // RW_CHUNK

    def chunk(c, d):
        return c + d * (nc - 1 - 2 * c)

    shared = pl.BlockSpec((RW_CHUNK, w), lambda b, d, c: (rb0 + b * nc + chunk(c, d), 0))
    perdir = pl.BlockSpec((1, RW_CHUNK, w), lambda b, d, c: (d, rb0 + b * nc + chunk(c, d), 0))
    state = pl.BlockSpec((1, 1, pairs, LANES, LANES), lambda b, d, c: (d, b, 0, 0, 0))
    return pl.pallas_call(
        _rw_scan_kernel, grid=(nb, 2, nc),
        in_specs=[shared, shared, shared, perdir, perdir, perdir, state],
        out_specs=[pl.BlockSpec((1, RW_CHUNK, w), lambda b, d, c: (d, b * nc + chunk(c, d), 0)), state],
        out_shape=[jax.ShapeDtypeStruct((2, nb * seq_len, w), F32), jax.ShapeDtypeStruct(h0.shape, F32)],
        scratch_shapes=[pltpu.VMEM((pairs, LANES, LANES), F32)],
        compiler_params=_cparams("parallel", "parallel", "arbitrary"), name="rwkv_scan",
    )(r, v, nkk, lw, kd, bd, h0)


def _rw_post_kernel(y_ref, g_ref, bonus_ref, lnw_ref, lnb_ref, ones_ref, o_ref):
    y = y_ref[0] + y_ref[1]
    ones = ones_ref[...]
    hmean = lambda t: jnp.dot(t, ones, preferred_element_type=F32, precision=HI) * (1.0 / RW_HEAD_DIM)
    yc = y - hmean(y)
    var = hmean(yc * yc)
    yn = yc * lax.rsqrt(var + RW_GN_EPS) * lnw_ref[...] + lnb_ref[...]
    o_ref[...] = ((yn + bonus_ref[...]) * g_ref[...]).astype(o_ref.dtype)


def _rwkv_post(y, g, bonus, lnw, lnb, tm):
    _, t, w = y.shape
    row = pl.BlockSpec((tm, w), lambda i: (i, 0))
    return pl.pallas_call(
        _rw_post_kernel, grid=(t // tm,),
        in_specs=[pl.BlockSpec((2, tm, w), lambda i: (0, i, 0)), row, row, _full((1, w)), _full((1, w)),
                  _full((w, w))],
        out_specs=row, out_shape=jax.ShapeDtypeStruct((t, w), BF16),
        compiler_params=_cparams("parallel"), name="rwkv_post",
    )(y, g, bonus, lnw.reshape(1, w), lnb.reshape(1, w), _head_ones(w, RW_HEAD_DIM))


def _rwkv_mixer(u, seq_lens, nb, seq, ctx_len, tm, mu, w0, w2, a0, a2, g2, k_k, k_a, r_k, lnx_w, lnx_b):
    t_lat = nb * seq
    r, vv, nkk, g, bonus, lw, kd, bd = _rwkv_prepare(u, seq_lens, tm, mu, w0, w2, a0, a2, g2, k_k, k_a, r_k)
    zero = jnp.zeros((2, nb, r.shape[1] // LANES, LANES, LANES), F32)
    y_ctx, h_ctx = _rwkv_scan(r, vv, nkk, lw, kd, bd, zero, nb, ctx_len, t_lat)
    y_lat, _ = _rwkv_scan(r, vv, nkk, lw, kd, bd, h_ctx, nb, seq, 0)
    y = jnp.concatenate([y_lat, y_ctx], axis=1)
    return _rwkv_post(y, g, bonus, lnx_w, lnx_b, tm)


def _extract_topk(s, order, payload, count):
    big = float(2 ** 24)
    vals, pays = [], []
    for _ in range(count):
        m = jnp.max(s, axis=0, keepdims=True)
        first = jnp.min(jnp.where(s == m, order, big), axis=0, keepdims=True)
        hit = order == first
        vals.append(m)
        pays.append(first if payload is None else jnp.sum(jnp.where(hit, payload, 0.0), axis=0, keepdims=True))
        s = jnp.where(hit, -jnp.inf, s)
    return jnp.concatenate(vals, axis=0), jnp.concatenate(pays, axis=0)


def _pruned_candidates(v1, i1, v2, i2):
    k = PEER_TOPK
    tt = v1.shape[1]
    row8 = lax.broadcasted_iota(I32, (8, tt), 0).astype(F32)
    row16 = lax.broadcasted_iota(I32, (k, tt), 0).astype(F32)
    nk = float(PEER_NKEYS)
    sums, flats, eids = [], [], []

    def add(valid, s, flat, eid):
        unused = float(k * k + 16 * len(sums))
        sums.append(s if valid is None else jnp.where(valid, s, -jnp.inf))
        flats.append(flat if valid is None else jnp.where(valid, flat, flat + unused))
        eids.append(eid)

    def vary_j(i, rows, nvalid):
        r = row16 if rows == k else row8
        add(None if nvalid == rows else r < nvalid, v1[i:i + 1] + v2[:rows], r + float(i * k),
            i1[i:i + 1] * nk + i2[:rows])

    def vary_i(j, i0, lo, hi):
        r = row8 + float(i0)
        add(None if (lo == i0 and hi == i0 + 8) else (r >= lo) & (r < hi), v1[i0:i0 + 8] + v2[j:j + 1],
            r * float(k) + float(j), i1[i0:i0 + 8] * nk + i2[j:j + 1])

    vary_j(0, k, k)
    vary_j(1, 8, 8)
    vary_j(2, 8, 5)
    vary_j(3, 8, 4)
    vary_i(0, 8, 8, 16)
    vary_i(0, 0, 4, 8)
    vary_i(1, 0, 4, 8)
    vary_i(2, 0, 4, 5)
    return jnp.concatenate(sums, axis=0), jnp.concatenate(flats, axis=0), jnp.concatenate(eids, axis=0)


def _peer_topk_kernel(q_ref, keys_ref, idx_ref, gate_ref):
    tt = q_ref.shape[0]
    kpos = lax.broadcasted_iota(I32, (PEER_NKEYS, tt), 0).astype(F32)

    def head(h, carry):
        vs, ids = [], []
        for p in range(2):
            col = pl.multiple_of((2 * h + p) * LANES, LANES)
            s = _nt(keys_ref[h, p], q_ref[:, pl.ds(col, LANES)], HI)
            v_p, i_p = _extract_topk(s, kpos, None, PEER_TOPK)
            vs.append(v_p)
            ids.append(i_p)
        cand, flat, eid = _pruned_candidates(vs[0], ids[0], vs[1], ids[1])
        top_s, top_i = _extract_topk(cand, flat, eid, PEER_TOPK)
        e = jnp.exp(top_s - top_s[0:1])
        rows = pl.ds(pl.multiple_of(h * PEER_TOPK, PEER_TOPK), PEER_TOPK)
        gate_ref[rows, :] = e / jnp.sum(e, axis=0, keepdims=True)
        idx_ref[rows, :] = top_i.astype(I32)
        return carry

    lax.fori_loop(0, PEER_HEADS, head, 0)


def _peer_topk(q, keys, tt):
    t = q.shape[0]
    ne = PEER_HEADS * PEER_TOPK
    out = pl.BlockSpec((ne, tt), lambda i: (0, i))
    return pl.pallas_call(
        _peer_topk_kernel, grid=(t // tt,),
        in_specs=[pl.BlockSpec((tt, q.shape[1]), lambda i: (i, 0)), _full(keys.shape)],
        out_specs=[out, out],
        out_shape=[jax.ShapeDtypeStruct((ne, t), I32), jax.ShapeDtypeStruct((ne, t), F32)],
        compiler_params=_cparams("parallel"), name="peer_topk",
    )(q, keys)


def _sc_gather(table, idx):
    n = idx.shape[0]
    r = table.shape[1]
    workers = SC_CORES * SC_SUBCORES
    per_worker = n // workers
    nwin = per_worker // SC_WINDOW
    assert n == workers * nwin * SC_WINDOW and nwin % SC_NBUF == 0
    mesh = plsc.VectorSubcoreMesh(core_axis_name="c", subcore_axis_name="s")

    def body(table_hbm, idx_hbm, out_hbm, idx_v, *rest):
        bufs = rest[:SC_NBUF]
        gsem = rest[SC_NBUF:2 * SC_NBUF]
        osem = rest[2 * SC_NBUF:]
        base = (lax.axis_index("s") * SC_CORES + lax.axis_index("c")) * per_worker
        pltpu.sync_copy(idx_hbm.at[pl.ds(base, per_worker)], idx_v)

        def gather(w, b):
            return pltpu.make_async_copy(table_hbm.at[idx_v.at[pl.ds(w * SC_WINDOW, SC_WINDOW)]], bufs[b], gsem[b])

        def put(w, b):
            return pltpu.make_async_copy(bufs[b], out_hbm.at[pl.ds(base + w * SC_WINDOW, SC_WINDOW)], osem[b])

        for b in range(SC_NBUF):
            gather(b, b).start()

        @pl.loop(0, nwin, step=SC_NBUF)
        def _(w0):
            for b in range(SC_NBUF):
                w = w0 + b
                gather(w, b).wait()
                put(w, b).start()
                put(w, b).wait()

                @pl.when(w + SC_NBUF < nwin)
                def _():
                    gather(w + SC_NBUF, b).start()

    return pl.kernel(
        body, mesh=mesh, out_type=jax.ShapeDtypeStruct((n, r), table.dtype),
        scratch_types=[pltpu.VMEM((per_worker,), I32)] + [pltpu.VMEM((SC_WINDOW, r), table.dtype)] * SC_NBUF
        + [pltpu.SemaphoreType.DMA] * (2 * SC_NBUF),
    )(table, idx)


def _peer_apply_kernel(rows_ref, h_ref, gates_ref, x_ref, mod_ref, nf_ref, *rest, final_norm):
    o_ref = rest[-1]
    tt, ne = gates_ref.shape
    eye = (lax.broadcasted_iota(I32, (ne, ne), 0) == lax.broadcasted_iota(I32, (ne, ne), 1))
    outs = []
    for t in range(tt):
        packed = rows_ref[t * ne:(t + 1) * ne, :]
        u = pltpu.bitcast(packed << 16, F32)
        v = pltpu.bitcast(packed & jnp.uint32(0xFFFF0000), F32)
        pre = jnp.sum(u * h_ref[t:t + 1, :], axis=1, keepdims=True)
        pre = jnp.sum(jnp.where(eye, pre, 0.0), axis=0, keepdims=True)
        act = 0.5 * pre * (1.0 + lax.erf(pre * (2.0 ** -0.5)))
        w = jnp.sum(jnp.where(eye, gates_ref[t:t + 1, :] * act, 0.0), axis=1, keepdims=True)
        outs.append(jnp.sum(v * w, axis=0, keepdims=True))
    y = x_ref[...] + mod_ref[0] * jnp.concatenate(outs, axis=0)
    if final_norm:
        y = y * lax.rsqrt(jnp.mean(y * y, axis=-1, keepdims=True) + EPS) * nf_ref[...]
    o_ref[...] = y


def _peer_apply(rows, h, gates, x, acc, gate_mod, norm_f, token0, bid, final_norm):
    t, d = x.shape
    n, ne = gates.shape
    tt = PEER_TOK
    blk0 = token0 // tt
    local = lambda c: pl.BlockSpec((tt, c), lambda i: (i, 0))
    glob = pl.BlockSpec((tt, d), lambda i: (blk0 + i, 0))
    in_specs = [pl.BlockSpec((tt * ne, d), lambda i: (i, 0)), local(d), local(ne), glob,
                pl.BlockSpec((1, 1, d), lambda i: (bid(blk0 + i), 0, 0)), _full((1, d))]
    args = [rows, h, gates, x, gate_mod, norm_f.reshape(1, d)]
    aliases = {}
    if acc is not None:
        in_specs.append(pl.BlockSpec(memory_space=pl.ANY))
        args.append(acc)
        aliases = {len(args) - 1: 0}
    return pl.pallas_call(
        functools.partial(_peer_apply_kernel, final_norm=final_norm),
        grid=(n // tt,), in_specs=in_specs, out_specs=glob, out_shape=jax.ShapeDtypeStruct((t, d), F32),
        input_output_aliases=aliases,
        compiler_params=_cparams("parallel"), name="peer_apply",
    )(*args)


def _peer(x, norm2, shift, scale, gate_mod, wq, keys, uv, norm_f, bid_fn, tm, final_norm):
    per = x.shape[0] // PEER_CHUNKS
    assert x.shape[0] == per * PEER_CHUNKS and per % tm == 0
    acc = None
    for k in range(PEER_CHUNKS):
        q, h = _normmod_proj(x, norm2, shift, scale, [wq], [F32], bid_fn(tm), tm, want_h=True, row0=k * per,
                             nrows=per)
        idx_t, gates_t = _peer_topk(q, keys, LANES)
        rows = _sc_gather(uv, idx_t.T.reshape(per * idx_t.shape[0]))
        acc = _peer_apply(rows, h, gates_t.T, x, acc, gate_mod, norm_f, k * per, bid_fn(PEER_TOK), final_norm)
    return acc


def _conv_kernel(x_ref, prev_ref, next_ref, w_ref, b_ref, dtr_ref, dtb_ref, o_ref, dt_ref, *, heads):
    x = x_ref[...]
    tm = x.shape[0]
    row = lax.broadcasted_iota(I32, x.shape, 0)
    up = jnp.where(row == 0, prev_ref[0], pltpu.roll(x, 1, 0))
    dn = jnp.where(row == tm - 1, next_ref[0], pltpu.roll(x, tm - 1, 0))
    y = up * w_ref[0:1] + x * w_ref[1:2] + dn * w_ref[2:3] + b_ref[...]
    o_ref[...] = y * jax.nn.sigmoid(y)
    lane = lax.broadcasted_iota(I32, (tm, LANES), 1)
    for d in range(2):
        dt_ref[d] = jnp.where(lane < heads, _softplus(dtr_ref[d] + dtb_ref[d]), 0.0)


def _mamba_conv(xbc, dt_raw, seq_lens, tm, conv_w, conv_b, dt_bias_pad, heads):
    t, c = xbc.shape
    prev, nxt = _halo_rows(xbc, tm, seq_lens)
    row = pl.BlockSpec((tm, c), lambda i: (i, 0))
    halo = pl.BlockSpec((1, 1, c), lambda i: (i, 0, 0))
    dts = pl.BlockSpec((2, tm, LANES), lambda i: (0, i, 0))
    return pl.pallas_call(
        functools.partial(_conv_kernel, heads=heads), grid=(t // tm,),
        in_specs=[row, halo, halo, _full(conv_w.shape), _full((1, c)), dts, _full((2, 1, LANES))],
        out_specs=[row, dts],
        out_shape=[jax.ShapeDtypeStruct((t, c), F32), jax.ShapeDtypeStruct((2, t, LANES), F32)],
        compiler_params=_cparams("parallel"), name="mamba_conv",
    )(xbc, prev, nxt, conv_w, conv_b.reshape(1, c), dt_raw, dt_bias_pad)


def _ssd_kernel(*refs, reverse, inner, groups, add_prev):
    if add_prev:
        xbc_ref, dt_ref, dtt_ref, alr_ref, alc_ref, rep_ref, h0_ref, yin_ref, y_ref, hf_ref, h_scr = refs
    else:
        xbc_ref, dt_ref, dtt_ref, alr_ref, alc_ref, rep_ref, h0_ref, y_ref, hf_ref, h_scr = refs
        yin_ref = None
    c = pl.program_id(1)
    nc = pl.num_programs(1)
    cs = M_CHUNK
    gw = inner // groups
    hpg = gw // M_HEAD_DIM

    @pl.when(c == 0)
    def _():
        h_scr[...] = h0_ref[0]

    dt = dt_ref[0]
    dtt = dtt_ref[0]
    a = dt * (-jnp.exp(alr_ref[...]))
    at = dtt * (-jnp.exp(alc_ref[...]))
    ri = lax.broadcasted_iota(I32, (cs, cs), 0)
    ci = lax.broadcasted_iota(I32, (cs, cs), 1)
    incl = (ri <= ci) if reverse else (ri >= ci)
    tri = jnp.where(incl, 1.0, 0.0)
    cum = jnp.dot(tri, a, preferred_element_type=F32, precision=HI)
    cumt = _nt(at, tri, HI)
    tot = jnp.sum(a, axis=0, keepdims=True)

    rep = rep_ref[...]

    def spread(t):
        hi, lo = _split_bf16(t)
        return jnp.dot(hi, rep, preferred_element_type=F32) + jnp.dot(lo, rep, preferred_element_type=F32)

    e_cum = spread(jnp.exp(cum))
    e_end = spread(jnp.exp(tot - cum) * dt)
    e_tot = spread(jnp.broadcast_to(jnp.exp(tot), (8, LANES)))[0:1]

    lane = lax.broadcasted_iota(I32, (cs, LANES), 1)
    lo_half = lane < M_HEAD_DIM
    ys = []
    for g in range(groups):
        bg32 = xbc_ref[:, inner + g * M_STATE:inner + (g + 1) * M_STATE]
        bg = bg32.astype(BF16)
        cg = xbc_ref[:, inner + groups * M_STATE + g * M_STATE:inner + groups * M_STATE + (g + 1) * M_STATE]
        cg = cg.astype(BF16)
        cb = _nt(cg, bg)
        hprev = h_scr[g]
        xg = xbc_ref[:, g * gw:(g + 1) * gw]
        y_off = jnp.dot(cg, hprev.astype(BF16), preferred_element_type=F32) * e_cum[:, g * gw:(g + 1) * gw]
        xd = (xg * e_end[:, g * gw:(g + 1) * gw]).astype(BF16)
        h_scr[g] = e_tot[:, g * gw:(g + 1) * gw] * hprev + jnp.dot(bg32.T.astype(BF16), xd, preferred_element_type=F32)
        for j in range(hpg // 2):
            xpair = xg[:, j * LANES:(j + 1) * LANES].astype(BF16)
            halves = []
            for hh in range(2):
                h = g * hpg + 2 * j + hh
                seg = jnp.minimum(cum[:, h:h + 1] - cumt[h:h + 1, :], 0.0)
                m = jnp.where(incl, cb * jnp.exp(seg), 0.0) * dtt[h:h + 1, :]
                halves.append(jnp.dot(m.astype(BF16), xpair, preferred_element_type=F32))
            ys.append(jnp.where(lo_half, halves[0], halves[1]) + y_off[:, j * LANES:(j + 1) * LANES])
    y = jnp.concatenate(ys, axis=1)
    if add_prev:
        y = y + yin_ref[...]
    y_ref[...] = y

    @pl.when(c == nc - 1)
    def _():
        hf_ref[0] = h_scr[...]


def _ssd_pass(xbc, dt, dtt, a_log, h0, y_prev, nb, seq_len, row0, reverse, inner, groups, heads):
    nc = seq_len // M_CHUNK
    rb0 = row0 // M_CHUNK
    c_all = xbc.shape[1]
    gw = inner // groups
    alr = jnp.zeros((1, LANES), F32).at[0, :heads].set(a_log)
    alc = jnp.broadcast_to(jnp.zeros((LANES,), F32).at[:heads].set(a_log)[:, None], (LANES, LANES))
    hid = np.arange(inner) // M_HEAD_DIM
    rep = jnp.asarray((np.arange(LANES)[:, None] == hid[None, :]).astype(np.float32), BF16)
    chunk = (lambda c: nc - 1 - c) if reverse else (lambda c: c)
    add_prev = y_prev is not None
    in_specs = [pl.BlockSpec((M_CHUNK, c_all), lambda b, c: (rb0 + b * nc + chunk(c), 0)),
                pl.BlockSpec((1, M_CHUNK, LANES), lambda b, c: (0, rb0 + b * nc + chunk(c), 0)),
                pl.BlockSpec((1, LANES, M_CHUNK), lambda b, c: (0, 0, rb0 + b * nc + chunk(c))),
                _full((1, LANES)), _full((LANES, LANES)), _full((LANES, inner)),
                pl.BlockSpec((1, groups, M_STATE, gw), lambda b, c: (b, 0, 0, 0))]
    args = [xbc, dt, dtt, alr, alc, rep, h0]
    yspec = pl.BlockSpec((M_CHUNK, inner), lambda b, c: (b * nc + chunk(c), 0))
    if add_prev:
        in_specs.append(yspec)
        args.append(y_prev)
    return pl.pallas_call(
        functools.partial(_ssd_kernel, reverse=reverse, inner=inner, groups=groups, add_prev=add_prev),
        grid=(nb, nc), in_specs=in_specs,
        out_specs=[yspec, pl.BlockSpec((1, groups, M_STATE, gw), lambda b, c: (b, 0, 0, 0))],
        out_shape=[jax.ShapeDtypeStruct((nb * seq_len, inner), F32), jax.ShapeDtypeStruct(h0.shape, F32)],
        scratch_shapes=[pltpu.VMEM((groups, M_STATE, gw), F32)],
        compiler_params=_cparams("parallel", "arbitrary"), name="ssd_pass",
    )(*args)


def _mamba_gate_kernel(y_ref, x_ref, z_ref, dsk_ref, gn_ref, o_ref, *, groups):
    z = z_ref[...]
    y = (y_ref[...] + dsk_ref[...] * x_ref[...]) * (z * jax.nn.sigmoid(z))
    gw = y.shape[1] // groups
    for g in range(groups):
        yg = y[:, g * gw:(g + 1) * gw]
        yg = yg * lax.rsqrt(jnp.mean(yg * yg, axis=-1, keepdims=True) + EPS) * gn_ref[:, g * gw:(g + 1) * gw]
        o_ref[:, g * gw:(g + 1) * gw] = yg.astype(o_ref.dtype)


def _mamba_gate(y, xbc, z, d_skip_cols, gnorm, groups, tm):
    t, inner = y.shape
    row = pl.BlockSpec((tm, inner), lambda i: (i, 0))
    return pl.pallas_call(
        functools.partial(_mamba_gate_kernel, groups=groups), grid=(t // tm,),
        in_specs=[row, row, row, _full((1, inner)), _full((1, inner))],
        out_specs=row, out_shape=jax.ShapeDtypeStruct((t, inner), BF16),
        compiler_params=_cparams("parallel"), name="mamba_gate",
    )(y, xbc, z, d_skip_cols.reshape(1, inner), gnorm.reshape(1, inner))


def _even_layer(xs, mods, nb, seq, ctx_len, tm, bid_fn, norm1, norm2, win, da_lambda, da_subln, rw_mu, rw_w0,
                rw_w2, rw_a0, rw_a2, rw_g2, rw_kk, rw_ka, rw_rk, rw_lnx_w, rw_lnx_b, wout, peer_q, peer_keys,
                peer_uv, lam_init):
    d = xs.shape[1]
    t_lat = nb * seq
    da_w = d // 2
    rw_w = d - da_w
    winb = win.astype(BF16)
    ws = [winb[:, :da_w], winb[:, da_w:2 * da_w], winb[:, 2 * da_w:3 * da_w], winb[:, 3 * da_w:]]
    q, k, v, u = _normmod_proj(xs, norm1, mods[0], mods[1], ws, [F32, F32, BF16, F32], bid_fn(tm), tm)

    cos, sin = _rope_tables(seq, tm)
    lat_tiles = t_lat // tm
    tab_block = lambda i: jnp.where(i < lat_tiles, i % (seq // tm), seq // tm)
    qr, kr = _rope(q, k, cos, sin, tab_block, tm)
    lk = ctx_len + seq
    cat = lambda a: jnp.concatenate([a[t_lat:].reshape(nb, ctx_len, da_w), a[:t_lat].reshape(nb, seq, da_w)],
                                    axis=1).reshape(nb * lk, da_w)
    tq = min(256, seq)
    o_lat = _diff_attention(qr, cat(kr), cat(v), da_lambda, da_subln, lam_init, nb, seq, lk, 0, tq)
    tqc = min(256, ctx_len)
    o_ctx = _diff_attention(qr, kr[t_lat:], v[t_lat:], da_lambda, da_subln, lam_init, nb, ctx_len, ctx_len,
                            t_lat, tqc)
    o_att = jnp.concatenate([o_lat, o_ctx], axis=0)

    seq_lens = [seq] * nb + [ctx_len] * nb
    o_rw = _rwkv_mixer(u, seq_lens, nb, seq, ctx_len, tm, rw_mu, rw_w0, rw_w2, rw_a0, rw_a2, rw_g2, rw_kk, rw_ka,
                       rw_rk, rw_lnx_w, rw_lnx_b)

    woutb = wout.astype(BF16)
    xs = _proj_residual([o_att, o_rw], [woutb[:da_w], woutb[da_w:]], xs, mods[2], bid_fn(tm), tm)
    return _peer(xs, norm2, mods[3], mods[4], mods[5], peer_q.astype(BF16), peer_keys, peer_uv, norm2, bid_fn, tm,
                 False)


def _odd_layer_last(xs, mods, nb, seq, ctx_len, tm, bid_fn, norm1, norm2, win, conv_w, conv_b, dt_bias, a_log,
                    d_skip, gnorm, wout, peer_q, peer_keys, peer_uv, norm_f):
    d = xs.shape[1]
    t_lat = nb * seq
    inner = wout.shape[0]
    heads = a_log.shape[1]
    conv_dim = conv_w.shape[1]
    groups = (conv_dim - inner) // (2 * M_STATE)
    winb = win.astype(BF16)
    pad = jnp.zeros((d, LANES - heads), BF16)
    w_dt = [jnp.concatenate([winb[:, inner + conv_dim + k * heads:inner + conv_dim + (k + 1) * heads], pad], axis=1)
            for k in range(2)]
    ws = [winb[:, :inner], winb[:, inner:inner + conv_dim]] + w_dt
    z, xbc_raw, dtr_f, dtr_b = _normmod_proj(xs, norm1, mods[0], mods[1], ws, [F32] * 4, bid_fn(tm), tm)
    seq_lens = [seq] * nb + [ctx_len] * nb
    dtb = jnp.zeros((2, 1, LANES), F32).at[:, 0, :heads].set(dt_bias)
    xbc, dt = _mamba_conv(xbc_raw, jnp.stack([dtr_f, dtr_b]), seq_lens, tm, conv_w, conv_b, dtb, heads)
    dtt = jnp.swapaxes(dt, 1, 2)
    h0 = jnp.zeros((nb, groups, M_STATE, inner // groups), F32)
    ssd = functools.partial(_ssd_pass, xbc, inner=inner, groups=groups, heads=heads)
    _, hf = ssd(dt[0:1], dtt[0:1], a_log[0], h0, None, nb, ctx_len, t_lat, False)
    _, hb = ssd(dt[1:2], dtt[1:2], a_log[1], h0, None, nb, ctx_len, t_lat, True)
    y, _ = ssd(dt[0:1], dtt[0:1], a_log[0], hf, None, nb, seq, 0, False)
    y, _ = ssd(dt[1:2], dtt[1:2], a_log[1], hb, y, nb, seq, 0, True)
    x_lat = xs[:t_lat]
    gated = _mamba_gate(y, xbc, z, jnp.repeat(d_skip, M_HEAD_DIM), gnorm, groups, tm)
    x_lat = _proj_residual([gated], [wout.astype(BF16)], x_lat, mods[2], bid_fn(tm), tm)
    return _peer(x_lat, norm2, mods[3], mods[4], mods[5], peer_q.astype(BF16), peer_keys, peer_uv, norm_f, bid_fn,
                 tm, True)


def kernel(x, c, ctx, c_ctx, ada_w_0, ada_b_0, norm1_0, norm2_0, win_0, da_lambda_0, da_subln_0, rw_mu_0, rw_w0_0, rw_w2_0, rw_a0_0, rw_a2_0, rw_g2_0, rw_kk_0, rw_ka_0, rw_rk_0, rw_lnx_w_0, rw_lnx_b_0, wout_0, peer_q_0, peer_keys_0, peer_u_0, peer_v_0, ada_w_1, ada_b_1, norm1_1, norm2_1, win_1, conv_w_1, conv_b_1, dt_bias_1, a_log_1, d_skip_1, gnorm_1, wout_1, peer_q_1, peer_keys_1, peer_u_1, peer_v_1, norm_f):
    nb, seq, d = x.shape
    ctx_len = ctx.shape[1]
    tm = 256 if (seq % 256 == 0 and ctx_len % 256 == 0) else 128
    assert seq % tm == 0 and ctx_len % tm == 0 and seq % GRID_W == 0

    def bid_fn(tile):
        per = seq // tile
        return lambda i: jnp.minimum(i // per, nb)

    xs = jnp.concatenate([x.reshape(nb * seq, d), ctx.reshape(nb * ctx_len, d)], axis=0)
    cvecs = jnp.zeros((16, d), F32).at[:nb].set(c).at[nb].set(c_ctx)

    def expert_rows(u, v):
        bits = lambda a: lax.bitcast_convert_type(a.astype(BF16), jnp.uint16).astype(jnp.uint32)
        return bits(u) | (bits(v) << 16)

    mods0 = _ada_mod(cvecs, ada_w_0, ada_b_0)
    uv0 = expert_rows(peer_u_0, peer_v_0)
    xs = _even_layer(xs, mods0, nb, seq, ctx_len, tm, bid_fn, norm1_0, norm2_0, win_0, da_lambda_0, da_subln_0,
                     rw_mu_0, rw_w0_0, rw_w2_0, rw_a0_0, rw_a2_0, rw_g2_0, rw_kk_0, rw_ka_0, rw_rk_0, rw_lnx_w_0,
                     rw_lnx_b_0, wout_0, peer_q_0, peer_keys_0, uv0, 0.8 - 0.6 * math.exp(-0.3 * 0))

    mods1 = _ada_mod(cvecs, ada_w_1, ada_b_1)
    uv1 = expert_rows(peer_u_1, peer_v_1)
    out = _odd_layer_last(xs, mods1, nb, seq, ctx_len, tm, bid_fn, norm1_1, norm2_1, win_1, conv_w_1, conv_b_1,
                          dt_bias_1, a_log_1, d_skip_1, gnorm_1, wout_1, peer_q_1, peer_keys_1, uv1, norm_f)
    return out.reshape(nb, seq, d)
```

```python
import functools
import math

import jax
import jax.numpy as jnp
import numpy as np
from jax import lax
from jax.experimental import pallas as pl
from jax.experimental.pallas import tpu as pltpu
from jax.experimental.pallas import tpu_sc as plsc

F32 = jnp.float32
BF16 = jnp.bfloat16
I32 = jnp.int32
HI = lax.Precision.HIGHEST

EPS = 1e-6
N_MOD = 6
GRID_W = 64
LANES = 128
VMEM_LIMIT_BYTES = 48 * 1024 * 1024

DA_HEAD_DIM = 64
DA_V_DIM = 128
ROPE_BASE = 10000.0
ROPE_NFREQ = DA_HEAD_DIM // 4
RW_HEAD_DIM = 64
RW_GN_EPS = 64e-5
RW_CHUNK = 64
RW_PASSES = 1
RW_STATE_PASSES = 3
M_HEAD_DIM = 64
M_STATE = 128
M_CHUNK = 128
PEER_HEADS = 8
PEER_NKEYS = 128
PEER_TOPK = 16
PEER_TOK = 16
PEER_CHUNKS = 8
SC_CORES = 2
SC_SUBCORES = 16
SC_WINDOW = 16
SC_NBUF = 4


def _cparams(*sem):
    return pltpu.CompilerParams(dimension_semantics=sem, vmem_limit_bytes=VMEM_LIMIT_BYTES)


def _nt(a, b, precision=None):
    return lax.dot_general(a, b, (((1,), (1,)), ((), ())), preferred_element_type=F32, precision=precision)


def _full(shape):
    nd = len(shape)
    return pl.BlockSpec(shape, lambda *_: (0,) * nd)


def _split_bf16(x):
    hi = x.astype(BF16)
    lo = (x - hi.astype(F32)).astype(BF16)
    return hi, lo


def _ada_kernel(c_ref, w_ref, b_ref, o_ref):
    c = c_ref[...]
    s = c * jax.nn.sigmoid(c)
    o_ref[...] = jnp.dot(s, w_ref[...], preferred_element_type=F32, precision=HI) + b_ref[...]


def _ada_mod(cvecs, w, b):
    r, d = cvecs.shape
    n = w.shape[1]
    tn = 1024
    m = pl.pallas_call(
        _ada_kernel,
        grid=(n // tn,),
        in_specs=[_full((r, d)), pl.BlockSpec((d, tn), lambda j: (0, j)), pl.BlockSpec((1, tn), lambda j: (0, j))],
        out_specs=pl.BlockSpec((r, tn), lambda j: (0, j)),
        out_shape=jax.ShapeDtypeStruct((r, n), F32),
        compiler_params=_cparams("parallel"),
        name="ada_mod",
    )(cvecs, w, b.reshape(1, n))
    return [m[:, k * d:(k + 1) * d].reshape(r, 1, d) for k in range(N_MOD)]


def _normmod_kernel(x_ref, g_ref, sh_ref, sc_ref, *refs, n_w, want_h):
    x = x_ref[...]
    y = x * lax.rsqrt(jnp.mean(x * x, axis=-1, keepdims=True) + EPS) * g_ref[...]
    h = y * (1.0 + sc_ref[0]) + sh_ref[0]
    hb = h.astype(BF16)
    for w_ref, o_ref in zip(refs[:n_w], refs[n_w:2 * n_w]):
        o_ref[...] = jnp.dot(hb, w_ref[...], preferred_element_type=F32).astype(o_ref.dtype)
    if want_h:
        refs[2 * n_w][...] = h


def _normmod_proj(x, g, shift, scale, ws, out_dtypes, bid, tm, want_h=False, row0=0, nrows=None):
    d = x.shape[1]
    t = x.shape[0] if nrows is None else nrows
    blk0 = row0 // tm
    n_w = len(ws)
    in_specs = [pl.BlockSpec((tm, d), lambda i: (blk0 + i, 0)), _full((1, d)),
                pl.BlockSpec((1, 1, d), lambda i: (bid(blk0 + i), 0, 0)),
                pl.BlockSpec((1, 1, d), lambda i: (bid(blk0 + i), 0, 0))]
    in_specs += [_full(w.shape) for w in ws]
    out_specs = [pl.BlockSpec((tm, w.shape[1]), lambda i: (i, 0)) for w in ws]
    out_shape = [jax.ShapeDtypeStruct((t, w.shape[1]), dt) for w, dt in zip(ws, out_dtypes)]
    if want_h:
        out_specs.append(pl.BlockSpec((tm, d), lambda i: (i, 0)))
        out_shape.append(jax.ShapeDtypeStruct((t, d), F32))
    return pl.pallas_call(
        functools.partial(_normmod_kernel, n_w=n_w, want_h=want_h),
        grid=(t // tm,), in_specs=in_specs, out_specs=out_specs, out_shape=out_shape,
        compiler_params=_cparams("parallel"), name="normmod_proj",
    )(x, g.reshape(1, d), shift, scale, *ws)


def _proj_res_kernel(*refs, n_a):
    a_refs = refs[:n_a]
    w_refs = refs[n_a:2 * n_a]
    res_ref, gate_ref, o_ref = refs[2 * n_a:]
    acc = jnp.dot(a_refs[0][...], w_refs[0][...], preferred_element_type=F32)
    for a_ref, w_ref in zip(a_refs[1:], w_refs[1:]):
        acc += jnp.dot(a_ref[...], w_ref[...], preferred_element_type=F32)
    o_ref[...] = res_ref[...] + gate_ref[0] * acc


def _proj_residual(a_list, w_list, res, gate, bid, tm):
    t, n = res.shape
    n_a = len(a_list)
    in_specs = [pl.BlockSpec((tm, a.shape[1]), lambda i: (i, 0)) for a in a_list]
    in_specs += [_full(w.shape) for w in w_list]
    in_specs += [pl.BlockSpec((tm, n), lambda i: (i, 0)), pl.BlockSpec((1, 1, n), lambda i: (bid(i), 0, 0))]
    return pl.pallas_call(
        functools.partial(_proj_res_kernel, n_a=n_a),
        grid=(t // tm,), in_specs=in_specs, out_specs=pl.BlockSpec((tm, n), lambda i: (i, 0)),
        out_shape=jax.ShapeDtypeStruct((t, n), F32),
        compiler_params=_cparams("parallel"), name="proj_residual",
    )(*a_list, *w_list, res, gate)


def _rope_kernel(q_ref, k_ref, c_ref, s_ref, qo_ref, ko_ref):
    c = c_ref[...]
    s = s_ref[...]
    lane = lax.broadcasted_iota(I32, c.shape, 1)
    first = (lane % 32) < 16
    width = q_ref.shape[1]

    def rot(x):
        partner = jnp.where(first, pltpu.roll(x, LANES - 16, 1), pltpu.roll(x, 16, 1))
        return x * c + partner * s

    for g in range(width // LANES):
        sl = slice(g * LANES, (g + 1) * LANES)
        qo_ref[:, sl] = (rot(q_ref[:, sl]) * (DA_HEAD_DIM ** -0.5)).astype(qo_ref.dtype)
        ko_ref[:, sl] = rot(k_ref[:, sl]).astype(ko_ref.dtype)


def _rope_tables(seq_len, tm):
    rows = seq_len // GRID_W
    row = jnp.repeat(jnp.arange(rows, dtype=F32), GRID_W)
    col = (jnp.arange(seq_len) % GRID_W).astype(F32)
    inv = ROPE_BASE ** (-jnp.arange(ROPE_NFREQ, dtype=F32) / ROPE_NFREQ)
    ang_r = row[:, None] * inv
    ang_c = col[:, None] * inv
    cos64 = jnp.concatenate([jnp.cos(ang_r), jnp.cos(ang_r), jnp.cos(ang_c), jnp.cos(ang_c)], axis=1)
    sin64 = jnp.concatenate([-jnp.sin(ang_r), jnp.sin(ang_r), -jnp.sin(ang_c), jnp.sin(ang_c)], axis=1)
    cos = jnp.concatenate([jnp.tile(cos64, (1, 2)), jnp.ones((tm, LANES), F32)], axis=0)
    sin = jnp.concatenate([jnp.tile(sin64, (1, 2)), jnp.zeros((tm, LANES), F32)], axis=0)
    return cos, sin


def _rope(q, k, cos, sin, tab_block, tm):
    t, w = q.shape
    row = pl.BlockSpec((tm, w), lambda i: (i, 0))
    tab = pl.BlockSpec((tm, LANES), lambda i: (tab_block(i), 0))
    return pl.pallas_call(
        _rope_kernel, grid=(t // tm,), in_specs=[row, row, tab, tab], out_specs=[row, row],
        out_shape=[jax.ShapeDtypeStruct((t, w), BF16)] * 2,
        compiler_params=_cparams("parallel"), name="rope",
    )(q, k, cos, sin)


def _attn_kernel(lam_ref, sub_ref, q_ref, k_ref, v_ref, o_ref, *, lam_init):
    lp = lam_ref[...]
    lam = (jnp.exp(jnp.sum(lp[0:1] * lp[1:2], keepdims=True))
           - jnp.exp(jnp.sum(lp[2:3] * lp[3:4], keepdims=True)) + lam_init)
    q = q_ref[...]
    k = k_ref[...]
    v = v_ref[...]
    lane = lax.broadcasted_iota(I32, q.shape, 1)
    outs = []
    for m in range(2):
        sel = (lane < DA_HEAD_DIM) if m == 0 else (lane >= DA_HEAD_DIM)
        s = _nt(jnp.where(sel, q, jnp.zeros_like(q)), k)
        p = jnp.exp(s - jnp.max(s, axis=-1, keepdims=True))
        denom = jnp.sum(p, axis=-1, keepdims=True)
        outs.append(jnp.dot(p.astype(BF16), v, preferred_element_type=F32) / denom)
    o = outs[0] - lam * outs[1]
    o = o * lax.rsqrt(jnp.mean(o * o, axis=-1, keepdims=True) + EPS) * sub_ref[...] * (1.0 - lam_init)
    o_ref[...] = o.astype(o_ref.dtype)


def _diff_attention(q, k, v, lamp, subln, lam_init, nb, lq, lk, q_row0, tq):
    w = q.shape[1]
    heads = w // DA_V_DIM
    nq = lq // tq
    qb0 = q_row0 // tq
    return pl.pallas_call(
        functools.partial(_attn_kernel, lam_init=lam_init),
        grid=(nb, heads, nq),
        in_specs=[_full(lamp.shape), _full((1, DA_V_DIM)),
                  pl.BlockSpec((tq, DA_V_DIM), lambda b, h, i: (qb0 + b * nq + i, h)),
                  pl.BlockSpec((lk, DA_V_DIM), lambda b, h, i: (b, h)),
                  pl.BlockSpec((lk, DA_V_DIM), lambda b, h, i: (b, h))],
        out_specs=pl.BlockSpec((tq, DA_V_DIM), lambda b, h, i: (b * nq + i, h)),
        out_shape=jax.ShapeDtypeStruct((nb * lq, w), BF16),
        compiler_params=_cparams("parallel", "parallel", "arbitrary"), name="diff_attention",
    )(lamp, subln.reshape(1, DA_V_DIM), q, k, v)


def _softplus(z):
    return jnp.maximum(z, 0.0) + jnp.log(1.0 + jnp.exp(-jnp.abs(z)))


def _rw_prep_kernel(u_ref, prev_ref, next_ref, mu_ref, w0_ref, w2_ref, a0_ref, a2_ref, g2_ref, kk_ref, ka_ref,
                    rk_ref, ones_ref, r_ref, v_ref, nkk_ref, g_ref, bonus_ref, lw_ref, kd_ref, bd_ref):
    u = u_ref[...]
    tm = u.shape[0]
    width = r_ref.shape[1]
    row = lax.broadcasted_iota(I32, u.shape, 0)
    up = jnp.where(row == 0, prev_ref[0], pltpu.roll(u, 1, 0))
    dn = jnp.where(row == tm - 1, next_ref[0], pltpu.roll(u, tm - 1, 0))
    u = u + mu_ref[...] * (0.5 * (up + dn) - u)
    r = u[:, :width]
    k = u[:, width:2 * width]
    v = u[:, 2 * width:3 * width]
    o = 3 * width
    w_in = u[:, o:o + LANES]
    a_in = u[:, o + LANES:o + 2 * LANES]
    g_in = u[:, o + 2 * LANES:o + 3 * LANES]
    ones = ones_ref[...]
    hsum = lambda t: jnp.dot(t, ones, preferred_element_type=F32, precision=HI)
    g = jnp.dot(jax.nn.sigmoid(g_in), g2_ref[...], preferred_element_type=F32, precision=HI)
    kk = k * kk_ref[...]
    kk = kk / jnp.maximum(jnp.sqrt(hsum(kk * kk)), 1e-12)
    w_log = -_softplus(-(w0_ref[...] + jnp.dot(jnp.tanh(w_in), w2_ref[...], preferred_element_type=F32,
                                               precision=HI))) - 0.5
    logw = -jnp.exp(w_log)
    a = jax.nn.sigmoid(a0_ref[...] + jnp.dot(a_in, a2_ref[...], preferred_element_type=F32, precision=HI))
    ksum = jnp.zeros_like(k)
    for d in range(2):
        a_d = a[:, d * width:(d + 1) * width]
        k_d = k * (1.0 + (a_d - 1.0) * ka_ref[...])
        ksum = ksum + k_d
        lw_ref[d] = logw[:, d * width:(d + 1) * width]
        kd_ref[d] = k_d
        bd_ref[d] = kk * a_d
    r_ref[...] = r
    v_ref[...] = v
    nkk_ref[...] = -kk
    g_ref[...] = g
    bonus_ref[...] = hsum(r * ksum * rk_ref[...]) * v


def _halo_rows(x, tm, seq_lens):
    t = x.shape[0]
    nt = t // tm
    starts = np.cumsum([0] + [n for n in seq_lens])[:-1]
    ends = np.cumsum(seq_lens)
    tile_start = np.arange(nt) * tm
    has_prev = ~np.isin(tile_start, starts)
    has_next = ~np.isin(tile_start + tm, ends)
    last = x[tm - 1::tm]
    first = x[0::tm]
    zero = jnp.zeros_like(first[:1])
    prev = jnp.concatenate([zero, last[:-1]], axis=0) * jnp.asarray(has_prev, x.dtype)[:, None]
    nxt = jnp.concatenate([first[1:], zero], axis=0) * jnp.asarray(has_next, x.dtype)[:, None]
    return prev[:, None, :], nxt[:, None, :]


def _block_diag2(m):
    z = jnp.zeros_like(m[0])
    return jnp.concatenate([jnp.concatenate([m[0], z], axis=1), jnp.concatenate([z, m[1]], axis=1)], axis=0)


def _head_ones(width, hd):
    idx = np.arange(width) // hd
    return jnp.asarray((idx[:, None] == idx[None, :]).astype(np.float32))


def _rwkv_prepare(u, seq_lens, tm, mu, w0, w2, a0, a2, g2, k_k, k_a, r_k):
    t, cols = u.shape
    width = k_k.shape[0]
    prev, nxt = _halo_rows(u, tm, seq_lens)
    row = lambda c: pl.BlockSpec((tm, c), lambda i: (i, 0))
    halo = pl.BlockSpec((1, 1, cols), lambda i: (i, 0, 0))
    dir_out = pl.BlockSpec((2, tm, width), lambda i: (0, i, 0))
    consts = [mu.reshape(1, cols), w0.reshape(1, 2 * width), _block_diag2(w2), a0.reshape(1, 2 * width),
              _block_diag2(a2), g2, k_k.reshape(1, width), k_a.reshape(1, width), r_k.reshape(1, width),
              _head_ones(width, RW_HEAD_DIM)]
    f = jax.ShapeDtypeStruct((t, width), F32)
    f2 = jax.ShapeDtypeStruct((2, t, width), F32)
    return pl.pallas_call(
        _rw_prep_kernel, grid=(t // tm,),
        in_specs=[row(cols), halo, halo] + [_full(c.shape) for c in consts],
        out_specs=[row(width)] * 5 + [dir_out] * 3,
        out_shape=[f] * 5 + [f2] * 3,
        compiler_params=_cparams("parallel"), name="rwkv_prepare",
    )(u, prev, nxt, *consts)


def _mm(x, y, passes):
    if passes == 6:
        return jnp.dot(x, y, preferred_element_type=F32, precision=HI)
    dot = lambda p, q: jnp.dot(p, q, preferred_element_type=F32)
    if passes == 1:
        return dot(x.astype(BF16), y.astype(BF16))
    xh, xl = _split_bf16(x)
    yh, yl = _split_bf16(y)
    return dot(xh, yh) + (dot(xh, yl) + dot(xl, yh))


def _rw_scan_kernel(rf_ref, vf_ref, af_ref, rb_ref, vb_ref, ab_ref, lwf_ref, kdf_ref, bdf_ref, lwb_ref, kdb_ref,
                    bdb_ref, h0_ref, yf_ref, yb_ref, hf_ref, h_scr):
    c = pl.program_id(1)
    nc = pl.num_programs(1)
    cs = RW_CHUNK
    pairs = h_scr.shape[1]

    @pl.when(c == 0)
    def _():
        h_scr[...] = h0_ref[:, 0]

    ri = lax.broadcasted_iota(I32, (cs, cs), 0)
    ci = lax.broadcasted_iota(I32, (cs, cs), 1)
    lane = lax.broadcasted_iota(I32, (cs, LANES), 1)
    lo_half = lane < RW_HEAD_DIM
    n2 = 2 * cs
    rt = lax.broadcasted_iota(I32, (n2, n2), 0)
    ct = lax.broadcasted_iota(I32, (n2, n2), 1)
    eye = rt == ct

    def stack2(x):
        return jnp.concatenate([jnp.where(lo_half, x, 0.0), jnp.where(lo_half, 0.0, x)], axis=0)

    mm = functools.partial(_mm, passes=RW_PASSES)
    mm_state = functools.partial(_mm, passes=RW_STATE_PASSES)

    dirs = ((rf_ref, vf_ref, af_ref, lwf_ref, kdf_ref, bdf_ref), (rb_ref, vb_ref, ab_ref, lwb_ref, kdb_ref, bdb_ref))
    h_in = [[h_scr[d, p] for p in range(pairs)] for d in range(2)]
    ys, h_out = ([], []), ([], [])
    for d, p in [(d, p) for d in range(2) for p in range(pairs)]:
        r_ref, v_ref, nkk_ref, lw_ref, kd_ref, bd_ref = dirs[d]
        sgn = 1 - 2 * d
        before_eq = jnp.where((ri - ci) * sgn >= 0, 1.0, 0.0).astype(BF16)
        dtok = ((rt & (cs - 1)) - (ct & (cs - 1))) * sgn
        strict = dtok > 0
        incl = dtok >= 0
        sl = slice(p * LANES, (p + 1) * LANES)
        logw = lw_ref[0, :, sl]
        r = r_ref[:, sl]
        v = v_ref[:, sl]
        a = nkk_ref[:, sl]
        k = kd_ref[0, :, sl]
        b = bd_ref[0, :, sl]

        lw_hi, lw_lo = _split_bf16(logw)
        cum = (jnp.dot(before_eq, lw_hi, preferred_element_type=F32)
               + jnp.dot(before_eq, lw_lo, preferred_element_type=F32))
        mid = cum[cs // 2:cs // 2 + 1]
        tot = jnp.sum(logw, axis=0, keepdims=True)
        e_in = jnp.exp(mid - cum)
        e_end = jnp.exp(tot - cum)

        a2 = stack2(a * jnp.exp(cum - logw - mid))
        r2 = stack2(r * jnp.exp(cum - mid))
        a2_abs = stack2(a * jnp.exp(cum - logw))
        r2_abs = stack2(r * jnp.exp(cum))
        b2 = stack2(b * e_in)
        k2 = stack2(k * e_in)
        v2 = stack2(v)
        bh2 = stack2(b * e_end)
        kh2 = stack2(k * e_end)

        b2t = b2.T
        k2t = k2.T
        nmat = jnp.where(strict, mm(a2, b2t), 0.0)
        mmat = jnp.where(strict, mm(a2, k2t), 0.0)
        qb = jnp.where(incl, mm(r2, b2t), 0.0)
        qk = jnp.where(incl, mm(r2, k2t), 0.0)

        tinv = jnp.where(eye, 1.0, 0.0) + nmat
        pw = nmat
        for _ in range(int(math.log2(cs)) - 1):
            pw = mm(pw, pw)
            tinv = tinv + mm(tinv, pw)

        w2 = mm(tinv, mm(mmat, v2))
        a2p = mm(tinv, a2_abs)
        y_intra = mm(qk, v2) + mm(qb, w2)
        r2p = r2_abs + mm(qb, a2p)
        bh2t = bh2.T
        gmat = jnp.where(eye, jnp.exp(tot), 0.0) + mm(bh2t, a2p)
        dmat = mm(bh2t, w2) + mm(kh2.T, v2)

        h = h_in[d][p]
        y2 = y_intra + mm_state(r2p, h)
        ys[d].append(y2[:cs] + y2[cs:])
        h_out[d].append(mm_state(gmat, h) + dmat)

    yf_ref[...] = jnp.concatenate(ys[0], axis=1)
    yb_ref[...] = jnp.concatenate(ys[1], axis=1)
    for d in range(2):
        for p in range(pairs):
            h_scr[d, p] = h_out[d][p]

    @pl.when(c == nc - 1)
    def _():
        hf_ref[:, 0] = h_scr[...]


def _rwkv_scan(r, v, nkk, lw, kd, bd, h0, nb, seq_len, row0):
    w = r.shape[1]
    pairs = w // LANES
    nc = seq_len // RW_CHUNK
    rb0 = row0 // RW_CHUNK
    fwd = lambda b, c: rb0 + b * nc + c
    bwd = lambda b, c: rb0 + b * nc + (nc - 1 - c)
    shared = lambda pos: pl.BlockSpec((RW_CHUNK, w), lambda b, c: (pos(b, c), 0))
    perdir = lambda d, pos: pl.BlockSpec((1, RW_CHUNK, w), lambda b, c: (d, pos(b, c), 0))
    state = pl.BlockSpec((2, 1, pairs, LANES, LANES), lambda b, c: (0, b, 0, 0, 0))
    y = jax.ShapeDtypeStruct((nb * seq_len, w), F32)
    return pl.pallas_call(
        _rw_scan_kernel, grid=(nb, nc),
        in_specs=[shared(fwd)] * 3 + [shared(bwd)] * 3 + [perdir(0, fwd)] * 3 + [perdir(1, bwd)] * 3 + [state],
        out_specs=[pl.BlockSpec((RW_CHUNK, w), lambda b, c: (b * nc + c, 0)),
                   pl.BlockSpec((RW_CHUNK, w), lambda b, c: (b * nc + (nc - 1 - c), 0)), state],
        out_shape=[y, y, jax.ShapeDtypeStruct(h0.shape, F32)],
        scratch_shapes=[pltpu.VMEM((2, pairs, LANES, LANES), F32)],
        compiler_params=_cparams("parallel", "arbitrary"), name="rwkv_scan",
    )(r, v, nkk, r, v, nkk, lw, kd, bd, lw, kd, bd, h0)


def _rw_post_kernel(yf_ref, yb_ref, g_ref, bonus_ref, lnw_ref, lnb_ref, ones_ref, o_ref):
    y = yf_ref[...] + yb_ref[...]
    ones = ones_ref[...]
    hmean = lambda t: jnp.dot(t, ones, preferred_element_type=F32, precision=HI) * (1.0 / RW_HEAD_DIM)
    yc = y - hmean(y)
    var = hmean(yc * yc)
    yn = yc * lax.rsqrt(var + RW_GN_EPS) * lnw_ref[...] + lnb_ref[...]
    o_ref[...] = ((yn + bonus_ref[...]) * g_ref[...]).astype(o_ref.dtype)


def _rwkv_post(yf, yb, g, bonus, lnw, lnb, tm):
    t, w = yf.shape
    row = pl.BlockSpec((tm, w), lambda i: (i, 0))
    return pl.pallas_call(
        _rw_post_kernel, grid=(t // tm,),
        in_specs=[row, row, row, row, _full((1, w)), _full((1, w)), _full((w, w))],
        out_specs=row, out_shape=jax.ShapeDtypeStruct((t, w), BF16),
        compiler_params=_cparams("parallel"), name="rwkv_post",
    )(yf, yb, g, bonus, lnw.reshape(1, w), lnb.reshape(1, w), _head_ones(w, RW_HEAD_DIM))


def _rwkv_mixer(u, seq_lens, nb, seq, ctx_len, tm, mu, w0, w2, a0, a2, g2, k_k, k_a, r_k, lnx_w, lnx_b):
    t_lat = nb * seq
    r, vv, nkk, g, bonus, lw, kd, bd = _rwkv_prepare(u, seq_lens, tm, mu, w0, w2, a0, a2, g2, k_k, k_a, r_k)
    zero = jnp.zeros((2, nb, r.shape[1] // LANES, LANES, LANES), F32)
    yf_ctx, yb_ctx, h_ctx = _rwkv_scan(r, vv, nkk, lw, kd, bd, zero, nb, ctx_len, t_lat)
    yf_lat, yb_lat, _ = _rwkv_scan(r, vv, nkk, lw, kd, bd, h_ctx, nb, seq, 0)
    yf = jnp.concatenate([yf_lat, yf_ctx], axis=0)
    yb = jnp.concatenate([yb_lat, yb_ctx], axis=0)
    return _rwkv_post(yf, yb, g, bonus, lnx_w, lnx_b, tm)


def _extract_topk(s, order, payload, count):
    big = float(2 ** 24)
    vals, pays = [], []
    for _ in range(count):
        m = jnp.max(s, axis=0, keepdims=True)
        first = jnp.min(jnp.where(s == m, order, big), axis=0, keepdims=True)
        hit = order == first
        vals.append(m)
        pays.append(first if payload is None else jnp.sum(jnp.where(hit, payload, 0.0), axis=0, keepdims=True))
        s = jnp.where(hit, -jnp.inf, s)
    return jnp.concatenate(vals, axis=0), jnp.concatenate(pays, axis=0)


def _pruned_candidates(v1, i1, v2, i2):
    k = PEER_TOPK
    tt = v1.shape[1]
    row8 = lax.broadcasted_iota(I32, (8, tt), 0).astype(F32)
    row16 = lax.broadcasted_iota(I32, (k, tt), 0).astype(F32)
    nk = float(PEER_NKEYS)
    sums, flats, eids = [], [], []

    def add(valid, s, flat, eid):
        unused = float(k * k + 16 * len(sums))
        sums.append(s if valid is None else jnp.where(valid, s, -jnp.inf))
        flats.append(flat if valid is None else jnp.where(valid, flat, flat + unused))
        eids.append(eid)

    def vary_j(i, rows, nvalid):
        r = row16 if rows == k else row8
        add(None if nvalid == rows else r < nvalid, v1[i:i + 1] + v2[:rows], r + float(i * k),
            i1[i:i + 1] * nk + i2[:rows])

    def vary_i(j, i0, lo, hi):
        r = row8 + float(i0)
        add(None if (lo == i0 and hi == i0 + 8) else (r >= lo) & (r < hi), v1[i0:i0 + 8] + v2[j:j + 1],
            r * float(k) + float(j), i1[i0:i0 + 8] * nk + i2[j:j + 1])

    vary_j(0, k, k)
    vary_j(1, 8, 8)
    vary_j(2, 8, 5)
    vary_j(3, 8, 4)
    vary_i(0, 8, 8, 16)
    vary_i(0, 0, 4, 8)
    vary_i(1, 0, 4, 8)
    vary_i(2, 0, 4, 5)
    return jnp.concatenate(sums, axis=0), jnp.concatenate(flats, axis=0), jnp.concatenate(eids, axis=0)


def _peer_topk_kernel(q_ref, keys_ref, idx_ref, gate_ref):
    tt = q_ref.shape[0]
    kpos = lax.broadcasted_iota(I32, (PEER_NKEYS, tt), 0).astype(F32)

    def head(h, carry):
        vs, ids = [], []
        for p in range(2):
            col = pl.multiple_of((2 * h + p) * LANES, LANES)
            s = _nt(keys_ref[h, p], q_ref[:, pl.ds(col, LANES)], HI)
            v_p, i_p = _extract_topk(s, kpos, None, PEER_TOPK)
            vs.append(v_p)
            ids.append(i_p)
        cand, flat, eid = _pruned_candidates(vs[0], ids[0], vs[1], ids[1])
        top_s, top_i = _extract_topk(cand, flat, eid, PEER_TOPK)
        e = jnp.exp(top_s - top_s[0:1])
        rows = pl.ds(pl.multiple_of(h * PEER_TOPK, PEER_TOPK), PEER_TOPK)
        gate_ref[rows, :] = e / jnp.sum(e, axis=0, keepdims=True)
        idx_ref[rows, :] = top_i.astype(I32)
        return carry

    lax.fori_loop(0, PEER_HEADS, head, 0)


def _peer_topk(q, keys, tt):
    t = q.shape[0]
    ne = PEER_HEADS * PEER_TOPK
    out = pl.BlockSpec((ne, tt), lambda i: (0, i))
    return pl.pallas_call(
        _peer_topk_kernel, grid=(t // tt,),
        in_specs=[pl.BlockSpec((tt, q.shape[1]), lambda i: (i, 0)), _full(keys.shape)],
        out_specs=[out, out],
        out_shape=[jax.ShapeDtypeStruct((ne, t), I32), jax.ShapeDtypeStruct((ne, t), F32)],
        compiler_params=_cparams("parallel"), name="peer_topk",
    )(q, keys)


def _sc_gather(table, idx):
    n = idx.shape[0]
    r = table.shape[1]
    workers = SC_CORES * SC_SUBCORES
    per_worker = n // workers
    nwin = per_worker // SC_WINDOW
    assert n == workers * nwin * SC_WINDOW and nwin % SC_NBUF == 0
    mesh = plsc.VectorSubcoreMesh(core_axis_name="c", subcore_axis_name="s")

    def body(table_hbm, idx_hbm, out_hbm, idx_v, *rest):
        bufs = rest[:SC_NBUF]
        gsem = rest[SC_NBUF:2 * SC_NBUF]
        osem = rest[2 * SC_NBUF:]
        base = (lax.axis_index("s") * SC_CORES + lax.axis_index("c")) * per_worker
        pltpu.sync_copy(idx_hbm.at[pl.ds(base, per_worker)], idx_v)

        def gather(w, b):
            return pltpu.make_async_copy(table_hbm.at[idx_v.at[pl.ds(w * SC_WINDOW, SC_WINDOW)]], bufs[b], gsem[b])

        def put(w, b):
            return pltpu.make_async_copy(bufs[b], out_hbm.at[pl.ds(base + w * SC_WINDOW, SC_WINDOW)], osem[b])

        for b in range(SC_NBUF):
            gather(b, b).start()

        @pl.loop(0, nwin, step=SC_NBUF)
        def _(w0):
            for b in range(SC_NBUF):
                w = w0 + b
                gather(w, b).wait()
                put(w, b).start()
                put(w, b).wait()

                @pl.when(w + SC_NBUF < nwin)
                def _():
                    gather(w + SC_NBUF, b).start()

    return pl.kernel(
        body, mesh=mesh, out_type=jax.ShapeDtypeStruct((n, r), table.dtype),
        scratch_types=[pltpu.VMEM((per_worker,), I32)] + [pltpu.VMEM((SC_WINDOW, r), table.dtype)] * SC_NBUF
        + [pltpu.SemaphoreType.DMA] * (2 * SC_NBUF),
    )(table, idx)


def _peer_apply_kernel(rows_ref, h_ref, gates_ref, x_ref, mod_ref, nf_ref, *rest, final_norm):
    o_ref = rest[-1]
    tt, ne = gates_ref.shape
    eye = (lax.broadcasted_iota(I32, (ne, ne), 0) == lax.broadcasted_iota(I32, (ne, ne), 1))
    outs = []
    for t in range(tt):
        packed = rows_ref[t * ne:(t + 1) * ne, :]
        u = pltpu.bitcast(packed << 16, F32)
        v = pltpu.bitcast(packed & jnp.uint32(0xFFFF0000), F32)
        pre = jnp.sum(u * h_ref[t:t + 1, :], axis=1, keepdims=True)
        pre = jnp.sum(jnp.where(eye, pre, 0.0), axis=0, keepdims=True)
        act = 0.5 * pre * (1.0 + lax.erf(pre * (2.0 ** -0.5)))
        w = jnp.sum(jnp.where(eye, gates_ref[t:t + 1, :] * act, 0.0), axis=1, keepdims=True)
        outs.append(jnp.sum(v * w, axis=0, keepdims=True))
    y = x_ref[...] + mod_ref[0] * jnp.concatenate(outs, axis=0)
    if final_norm:
        y = y * lax.rsqrt(jnp.mean(y * y, axis=-1, keepdims=True) + EPS) * nf_ref[...]
    o_ref[...] = y


def _peer_apply(rows, h, gates, x, acc, gate_mod, norm_f, token0, bid, final_norm):
    t, d = x.shape
    n, ne = gates.shape
    tt = PEER_TOK
    blk0 = token0 // tt
    local = lambda c: pl.BlockSpec((tt, c), lambda i: (i, 0))
    glob = pl.BlockSpec((tt, d), lambda i: (blk0 + i, 0))
    in_specs = [pl.BlockSpec((tt * ne, d), lambda i: (i, 0)), local(d), local(ne), glob,
                pl.BlockSpec((1, 1, d), lambda i: (bid(blk0 + i), 0, 0)), _full((1, d))]
    args = [rows, h, gates, x, gate_mod, norm_f.reshape(1, d)]
    aliases = {}
    if acc is not None:
        in_specs.append(pl.BlockSpec(memory_space=pl.ANY))
        args.append(acc)
        aliases = {len(args) - 1: 0}
    return pl.pallas_call(
        functools.partial(_peer_apply_kernel, final_norm=final_norm),
        grid=(n // tt,), in_specs=in_specs, out_specs=glob, out_shape=jax.ShapeDtypeStruct((t, d), F32),
        input_output_aliases=aliases,
        compiler_params=_cparams("parallel"), name="peer_apply",
    )(*args)


def _peer(x, norm2, shift, scale, gate_mod, wq, keys, uv, norm_f, bid_fn, tm, final_norm):
    per = x.shape[0] // PEER_CHUNKS
    assert x.shape[0] == per * PEER_CHUNKS and per % tm == 0
    acc = None
    for k in range(PEER_CHUNKS):
        q, h = _normmod_proj(x, norm2, shift, scale, [wq], [F32], bid_fn(tm), tm, want_h=True, row0=k * per,
                             nrows=per)
        idx_t, gates_t = _peer_topk(q, keys, LANES)
        rows = _sc_gather(uv, idx_t.T.reshape(per * idx_t.shape[0]))
        acc = _peer_apply(rows, h, gates_t.T, x, acc, gate_mod, norm_f, k * per, bid_fn(PEER_TOK), final_norm)
    return acc


def _conv_kernel(x_ref, prev_ref, next_ref, w_ref, b_ref, dtr_ref, dtb_ref, o_ref, dt_ref, *, heads):
    x = x_ref[...]
    tm = x.shape[0]
    row = lax.broadcasted_iota(I32, x.shape, 0)
    up = jnp.where(row == 0, prev_ref[0], pltpu.roll(x, 1, 0))
    dn = jnp.where(row == tm - 1, next_ref[0], pltpu.roll(x, tm - 1, 0))
    y = up * w_ref[0:1] + x * w_ref[1:2] + dn * w_ref[2:3] + b_ref[...]
    o_ref[...] = y * jax.nn.sigmoid(y)
    lane = lax.broadcasted_iota(I32, (tm, LANES), 1)
    for d in range(2):
        dt_ref[d] = jnp.where(lane < heads, _softplus(dtr_ref[d] + dtb_ref[d]), 0.0)


def _mamba_conv(xbc, dt_raw, seq_lens, tm, conv_w, conv_b, dt_bias_pad, heads):
    t, c = xbc.shape
    prev, nxt = _halo_rows(xbc, tm, seq_lens)
    row = pl.BlockSpec((tm, c), lambda i: (i, 0))
    halo = pl.BlockSpec((1, 1, c), lambda i: (i, 0, 0))
    dts = pl.BlockSpec((2, tm, LANES), lambda i: (0, i, 0))
    return pl.pallas_call(
        functools.partial(_conv_kernel, heads=heads), grid=(t // tm,),
        in_specs=[row, halo, halo, _full(conv_w.shape), _full((1, c)), dts, _full((2, 1, LANES))],
        out_specs=[row, dts],
        out_shape=[jax.ShapeDtypeStruct((t, c), F32), jax.ShapeDtypeStruct((2, t, LANES), F32)],
        compiler_params=_cparams("parallel"), name="mamba_conv",
    )(xbc, prev, nxt, conv_w, conv_b.reshape(1, c), dt_raw, dt_bias_pad)


def _ssd_kernel(*refs, reverse, inner, groups, add_prev):
    if add_prev:
        xbc_ref, dt_ref, dtt_ref, alr_ref, alc_ref, rep_ref, h0_ref, yin_ref, y_ref, hf_ref, h_scr = refs
    else:
        xbc_ref, dt_ref, dtt_ref, alr_ref, alc_ref, rep_ref, h0_ref, y_ref, hf_ref, h_scr = refs
        yin_ref = None
    c = pl.program_id(1)
    nc = pl.num_programs(1)
    cs = M_CHUNK
    gw = inner // groups
    hpg = gw // M_HEAD_DIM

    @pl.when(c == 0)
    def _():
        h_scr[...] = h0_ref[0]

    dt = dt_ref[0]
    dtt = dtt_ref[0]
    a = dt * (-jnp.exp(alr_ref[...]))
    at = dtt * (-jnp.exp(alc_ref[...]))
    ri = lax.broadcasted_iota(I32, (cs, cs), 0)
    ci = lax.broadcasted_iota(I32, (cs, cs), 1)
    incl = (ri <= ci) if reverse else (ri >= ci)
    tri = jnp.where(incl, 1.0, 0.0)
    cum = jnp.dot(tri, a, preferred_element_type=F32, precision=HI)
    cumt = _nt(at, tri, HI)
    tot = jnp.sum(a, axis=0, keepdims=True)

    rep = rep_ref[...]

    def spread(t):
        hi, lo = _split_bf16(t)
        return jnp.dot(hi, rep, preferred_element_type=F32) + jnp.dot(lo, rep, preferred_element_type=F32)

    e_cum = spread(jnp.exp(cum))
    e_end = spread(jnp.exp(tot - cum) * dt)
    e_tot = spread(jnp.broadcast_to(jnp.exp(tot), (8, LANES)))[0:1]

    lane = lax.broadcasted_iota(I32, (cs, LANES), 1)
    lo_half = lane < M_HEAD_DIM
    ys = []
    for g in range(groups):
        bg32 = xbc_ref[:, inner + g * M_STATE:inner + (g + 1) * M_STATE]
        bg = bg32.astype(BF16)
        cg = xbc_ref[:, inner + groups * M_STATE + g * M_STATE:inner + groups * M_STATE + (g + 1) * M_STATE]
        cg = cg.astype(BF16)
        cb = _nt(cg, bg)
        hprev = h_scr[g]
        xg = xbc_ref[:, g * gw:(g + 1) * gw]
        y_off = jnp.dot(cg, hprev.astype(BF16), preferred_element_type=F32) * e_cum[:, g * gw:(g + 1) * gw]
        xd = (xg * e_end[:, g * gw:(g + 1) * gw]).astype(BF16)
        h_scr[g] = e_tot[:, g * gw:(g + 1) * gw] * hprev + jnp.dot(bg32.T.astype(BF16), xd, preferred_element_type=F32)
        for j in range(hpg // 2):
            xpair = xg[:, j * LANES:(j + 1) * LANES].astype(BF16)
            halves = []
            for hh in range(2):
                h = g * hpg + 2 * j + hh
                seg = jnp.minimum(cum[:, h:h + 1] - cumt[h:h + 1, :], 0.0)
                m = jnp.where(incl, cb * jnp.exp(seg), 0.0) * dtt[h:h + 1, :]
                halves.append(jnp.dot(m.astype(BF16), xpair, preferred_element_type=F32))
            ys.append(jnp.where(lo_half, halves[0], halves[1]) + y_off[:, j * LANES:(j + 1) * LANES])
    y = jnp.concatenate(ys, axis=1)
    if add_prev:
        y = y + yin_ref[...]
    y_ref[...] = y

    @pl.when(c == nc - 1)
    def _():
        hf_ref[0] = h_scr[...]


def _ssd_pass(xbc, dt, dtt, a_log, h0, y_prev, nb, seq_len, row0, reverse, inner, groups, heads):
    nc = seq_len // M_CHUNK
    rb0 = row0 // M_CHUNK
    c_all = xbc.shape[1]
    gw = inner // groups
    alr = jnp.zeros((1, LANES), F32).at[0, :heads].set(a_log)
    alc = jnp.broadcast_to(jnp.zeros((LANES,), F32).at[:heads].set(a_log)[:, None], (LANES, LANES))
    hid = np.arange(inner) // M_HEAD_DIM
    rep = jnp.asarray((np.arange(LANES)[:, None] == hid[None, :]).astype(np.float32), BF16)
    chunk = (lambda c: nc - 1 - c) if reverse else (lambda c: c)
    add_prev = y_prev is not None
    in_specs = [pl.BlockSpec((M_CHUNK, c_all), lambda b, c: (rb0 + b * nc + chunk(c), 0)),
                pl.BlockSpec((1, M_CHUNK, LANES), lambda b, c: (0, rb0 + b * nc + chunk(c), 0)),
                pl.BlockSpec((1, LANES, M_CHUNK), lambda b, c: (0, 0, rb0 + b * nc + chunk(c))),
                _full((1, LANES)), _full((LANES, LANES)), _full((LANES, inner)),
                pl.BlockSpec((1, groups, M_STATE, gw), lambda b, c: (b, 0, 0, 0))]
    args = [xbc, dt, dtt, alr, alc, rep, h0]
    yspec = pl.BlockSpec((M_CHUNK, inner), lambda b, c: (b * nc + chunk(c), 0))
    if add_prev:
        in_specs.append(yspec)
        args.append(y_prev)
    return pl.pallas_call(
        functools.partial(_ssd_kernel, reverse=reverse, inner=inner, groups=groups, add_prev=add_prev),
        grid=(nb, nc), in_specs=in_specs,
        out_specs=[yspec, pl.BlockSpec((1, groups, M_STATE, gw), lambda b, c: (b, 0, 0, 0))],
        out_shape=[jax.ShapeDtypeStruct((nb * seq_len, inner), F32), jax.ShapeDtypeStruct(h0.shape, F32)],
        scratch_shapes=[pltpu.VMEM((groups, M_STATE, gw), F32)],
        compiler_params=_cparams("parallel", "arbitrary"), name="ssd_pass",
    )(*args)


def _mamba_gate_kernel(y_ref, x_ref, z_ref, dsk_ref, gn_ref, o_ref, *, groups):
    z = z_ref[...]
    y = (y_ref[...] + dsk_ref[...] * x_ref[...]) * (z * jax.nn.sigmoid(z))
    gw = y.shape[1] // groups
    for g in range(groups):
        yg = y[:, g * gw:(g + 1) * gw]
        yg = yg * lax.rsqrt(jnp.mean(yg * yg, axis=-1, keepdims=True) + EPS) * gn_ref[:, g * gw:(g + 1) * gw]
        o_ref[:, g * gw:(g + 1) * gw] = yg.astype(o_ref.dtype)


def _mamba_gate(y, xbc, z, d_skip_cols, gnorm, groups, tm):
    t, inner = y.shape
    row = pl.BlockSpec((tm, inner), lambda i: (i, 0))
    return pl.pallas_call(
        functools.partial(_mamba_gate_kernel, groups=groups), grid=(t // tm,),
        in_specs=[row, row, row, _full((1, inner)), _full((1, inner))],
        out_specs=row, out_shape=jax.ShapeDtypeStruct((t, inner), BF16),
        compiler_params=_cparams("parallel"), name="mamba_gate",
    )(y, xbc, z, d_skip_cols.reshape(1, inner), gnorm.reshape(1, inner))


def _even_layer(xs, mods, nb, seq, ctx_len, tm, bid_fn, norm1, norm2, win, da_lambda, da_subln, rw_mu, rw_w0,
                rw_w2, rw_a0, rw_a2, rw_g2, rw_kk, rw_ka, rw_rk, rw_lnx_w, rw_lnx_b, wout, peer_q, peer_keys,
                peer_uv, lam_init):
    d = xs.shape[1]
    t_lat = nb * seq
    da_w = d // 2
    rw_w = d - da_w
    winb = win.astype(BF16)
    ws = [winb[:, :da_w], winb[:, da_w:2 * da_w], winb[:, 2 * da_w:3 * da_w], winb[:, 3 * da_w:]]
    q, k, v, u = _normmod_proj(xs, norm1, mods[0], mods[1], ws, [F32, F32, BF16, F32], bid_fn(tm), tm)

    cos, sin = _rope_tables(seq, tm)
    lat_tiles = t_lat // tm
    tab_block = lambda i: jnp.where(i < lat_tiles, i % (seq // tm), seq // tm)
    qr, kr = _rope(q, k, cos, sin, tab_block, tm)
    lk = ctx_len + seq
    cat = lambda a: jnp.concatenate([a[t_lat:].reshape(nb, ctx_len, da_w), a[:t_lat].reshape(nb, seq, da_w)],
                                    axis=1).reshape(nb * lk, da_w)
    tq = min(256, seq)
    o_lat = _diff_attention(qr, cat(kr), cat(v), da_lambda, da_subln, lam_init, nb, seq, lk, 0, tq)
    tqc = min(256, ctx_len)
    o_ctx = _diff_attention(qr, kr[t_lat:], v[t_lat:], da_lambda, da_subln, lam_init, nb, ctx_len, ctx_len,
                            t_lat, tqc)
    o_att = jnp.concatenate([o_lat, o_ctx], axis=0)

    seq_lens = [seq] * nb + [ctx_len] * nb
    o_rw = _rwkv_mixer(u, seq_lens, nb, seq, ctx_len, tm, rw_mu, rw_w0, rw_w2, rw_a0, rw_a2, rw_g2, rw_kk, rw_ka,
                       rw_rk, rw_lnx_w, rw_lnx_b)

    woutb = wout.astype(BF16)
    xs = _proj_residual([o_att, o_rw], [woutb[:da_w], woutb[da_w:]], xs, mods[2], bid_fn(tm), tm)
    return _peer(xs, norm2, mods[3], mods[4], mods[5], peer_q.astype(BF16), peer_keys, peer_uv, norm2, bid_fn, tm,
                 False)


def _odd_layer_last(xs, mods, nb, seq, ctx_len, tm, bid_fn, norm1, norm2, win, conv_w, conv_b, dt_bias, a_log,
                    d_skip, gnorm, wout, peer_q, peer_keys, peer_uv, norm_f):
    d = xs.shape[1]
    t_lat = nb * seq
    inner = wout.shape[0]
    heads = a_log.shape[1]
    conv_dim = conv_w.shape[1]
    groups = (conv_dim - inner) // (2 * M_STATE)
    winb = win.astype(BF16)
    pad = jnp.zeros((d, LANES - heads), BF16)
    w_dt = [jnp.concatenate([winb[:, inner + conv_dim + k * heads:inner + conv_dim + (k + 1) * heads], pad], axis=1)
            for k in range(2)]
    ws = [winb[:, :inner], winb[:, inner:inner + conv_dim]] + w_dt
    z, xbc_raw, dtr_f, dtr_b = _normmod_proj(xs, norm1, mods[0], mods[1], ws, [F32] * 4, bid_fn(tm), tm)
    seq_lens = [seq] * nb + [ctx_len] * nb
    dtb = jnp.zeros((2, 1, LANES), F32).at[:, 0, :heads].set(dt_bias)
    xbc, dt = _mamba_conv(xbc_raw, jnp.stack([dtr_f, dtr_b]), seq_lens, tm, conv_w, conv_b, dtb, heads)
    dtt = jnp.swapaxes(dt, 1, 2)
    h0 = jnp.zeros((nb, groups, M_STATE, inner // groups), F32)
    ssd = functools.partial(_ssd_pass, xbc, inner=inner, groups=groups, heads=heads)
    _, hf = ssd(dt[0:1], dtt[0:1], a_log[0], h0, None, nb, ctx_len, t_lat, False)
    _, hb = ssd(dt[1:2], dtt[1:2], a_log[1], h0, None, nb, ctx_len, t_lat, True)
    y, _ = ssd(dt[0:1], dtt[0:1], a_log[0], hf, None, nb, seq, 0, False)
    y, _ = ssd(dt[1:2], dtt[1:2], a_log[1], hb, y, nb, seq, 0, True)
    x_lat = xs[:t_lat]
    gated = _mamba_gate(y, xbc, z, jnp.repeat(d_skip, M_HEAD_DIM), gnorm, groups, tm)
    x_lat = _proj_residual([gated], [wout.astype(BF16)], x_lat, mods[2], bid_fn(tm), tm)
    return _peer(x_lat, norm2, mods[3], mods[4], mods[5], peer_q.astype(BF16), peer_keys, peer_uv, norm_f, bid_fn,
                 tm, True)


def kernel(x, c, ctx, c_ctx, ada_w_0, ada_b_0, norm1_0, norm2_0, win_0, da_lambda_0, da_subln_0, rw_mu_0, rw_w0_0, rw_w2_0, rw_a0_0, rw_a2_0, rw_g2_0, rw_kk_0, rw_ka_0, rw_rk_0, rw_lnx_w_0, rw_lnx_b_0, wout_0, peer_q_0, peer_keys_0, peer_u_0, peer_v_0, ada_w_1, ada_b_1, norm1_1, norm2_1, win_1, conv_w_1, conv_b_1, dt_bias_1, a_log_1, d_skip_1, gnorm_1, wout_1, peer_q_1, peer_keys_1, peer_u_1, peer_v_1, norm_f):
    nb, seq, d = x.shape
    ctx_len = ctx.shape[1]
    tm = 256 if (seq % 256 == 0 and ctx_len % 256 == 0) else 128
    assert seq % tm == 0 and ctx_len % tm == 0 and seq % GRID_W == 0

    def bid_fn(tile):
        per = seq // tile
        return lambda i: jnp.minimum(i // per, nb)

    xs = jnp.concatenate([x.reshape(nb * seq, d), ctx.reshape(nb * ctx_len, d)], axis=0)
    cvecs = jnp.zeros((16, d), F32).at[:nb].set(c).at[nb].set(c_ctx)

    def expert_rows(u, v):
        bits = lambda a: lax.bitcast_convert_type(a.astype(BF16), jnp.uint16).astype(jnp.uint32)
        return bits(u) | (bits(v) << 16)

    mods0 = _ada_mod(cvecs, ada_w_0, ada_b_0)
    uv0 = expert_rows(peer_u_0, peer_v_0)
    xs = _even_layer(xs, mods0, nb, seq, ctx_len, tm, bid_fn, norm1_0, norm2_0, win_0, da_lambda_0, da_subln_0,
                     rw_mu_0, rw_w0_0, rw_w2_0, rw_a0_0, rw_a2_0, rw_g2_0, rw_kk_0, rw_ka_0, rw_rk_0, rw_lnx_w_0,
                     rw_lnx_b_0, wout_0, peer_q_0, peer_keys_0, uv0, 0.8 - 0.6 * math.exp(-0.3 * 0))

    mods1 = _ada_mod(cvecs, ada_w_1, ada_b_1)
    uv1 = expert_rows(peer_u_1, peer_v_1)
    out = _odd_layer_last(xs, mods1, nb, seq, ctx_len, tm, bid_fn, norm1_1, norm2_1, win_1, conv_w_1, conv_b_1,
                          dt_bias_1, a_log_1, d_skip_1, gnorm_1, wout_1, peer_q_1, peer_keys_1, uv1, norm_f)
    return out.reshape(nb, seq, d)
```

```python
import functools
import math

import jax
import jax.numpy as jnp
import numpy as np
from jax import lax
from jax.experimental import pallas as pl
from jax.experimental.pallas import tpu as pltpu
from jax.experimental.pallas import tpu_sc as plsc

F32 = jnp.float32
BF16 = jnp.bfloat16
I32 = jnp.int32
HI = lax.Precision.HIGHEST

EPS = 1e-6
N_MOD = 6
GRID_W = 64
LANES = 128
VMEM_LIMIT_BYTES = 48 * 1024 * 1024

DA_HEAD_DIM = 64
DA_V_DIM = 128
ROPE_BASE = 10000.0
ROPE_NFREQ = DA_HEAD_DIM // 4
RW_HEAD_DIM = 64
RW_GN_EPS = 64e-5
RW_CHUNK = 64
RW_PASSES = 1
RW_STATE_PASSES = 3
M_HEAD_DIM = 64
M_STATE = 128
M_CHUNK = 128
PEER_HEADS = 8
PEER_NKEYS = 128
PEER_TOPK = 16
PEER_TOK = 8
PEER_CHUNKS = 4
BATCH_STREAMS = 2
SC_CORES = 2
SC_SUBCORES = 16
SC_WINDOW = 16
SC_NBUF = 4


def _cparams(*sem):
    return pltpu.CompilerParams(dimension_semantics=sem, vmem_limit_bytes=VMEM_LIMIT_BYTES)


def _nt(a, b, precision=None):
    return lax.dot_general(a, b, (((1,), (1,)), ((), ())), preferred_element_type=F32, precision=precision)


def _full(shape):
    nd = len(shape)
    return pl.BlockSpec(shape, lambda *_: (0,) * nd)


def _split_bf16(x):
    hi = x.astype(BF16)
    lo = (x - hi.astype(F32)).astype(BF16)
    return hi, lo


def _ada_kernel(c_ref, w_ref, b_ref, o_ref):
    c = c_ref[...]
    s = c * jax.nn.sigmoid(c)
    o_ref[...] = jnp.dot(s, w_ref[...], preferred_element_type=F32, precision=HI) + b_ref[...]


def _ada_mod(cvecs, w, b):
    r, d = cvecs.shape
    n = w.shape[1]
    tn = 1024
    m = pl.pallas_call(
        _ada_kernel,
        grid=(n // tn,),
        in_specs=[_full((r, d)), pl.BlockSpec((d, tn), lambda j: (0, j)), pl.BlockSpec((1, tn), lambda j: (0, j))],
        out_specs=pl.BlockSpec((r, tn), lambda j: (0, j)),
        out_shape=jax.ShapeDtypeStruct((r, n), F32),
        compiler_params=_cparams("parallel"),
        name="ada_mod",
    )(cvecs, w, b.reshape(1, n))
    return [m[:, k * d:(k + 1) * d].reshape(r, 1, d) for k in range(N_MOD)]


def _normmod_kernel(x_ref, g_ref, sh_ref, sc_ref, *refs, n_w, want_h):
    x = x_ref[...]
    y = x * lax.rsqrt(jnp.mean(x * x, axis=-1, keepdims=True) + EPS) * g_ref[...]
    h = y * (1.0 + sc_ref[0]) + sh_ref[0]
    hb = h.astype(BF16)
    for w_ref, o_ref in zip(refs[:n_w], refs[n_w:2 * n_w]):
        o_ref[...] = jnp.dot(hb, w_ref[...], preferred_element_type=F32).astype(o_ref.dtype)
    if want_h:
        refs[2 * n_w][...] = h


def _normmod_proj(x, g, shift, scale, ws, out_dtypes, bid, tm, want_h=False, row0=0, nrows=None):
    d = x.shape[1]
    t = x.shape[0] if nrows is None else nrows
    blk0 = row0 // tm
    n_w = len(ws)
    in_specs = [pl.BlockSpec((tm, d), lambda i: (blk0 + i, 0)), _full((1, d)),
                pl.BlockSpec((1, 1, d), lambda i: (bid(blk0 + i), 0, 0)),
                pl.BlockSpec((1, 1, d), lambda i: (bid(blk0 + i), 0, 0))]
    in_specs += [_full(w.shape) for w in ws]
    out_specs = [pl.BlockSpec((tm, w.shape[1]), lambda i: (i, 0)) for w in ws]
    out_shape = [jax.ShapeDtypeStruct((t, w.shape[1]), dt) for w, dt in zip(ws, out_dtypes)]
    if want_h:
        out_specs.append(pl.BlockSpec((tm, d), lambda i: (i, 0)))
        out_shape.append(jax.ShapeDtypeStruct((t, d), F32))
    return pl.pallas_call(
        functools.partial(_normmod_kernel, n_w=n_w, want_h=want_h),
        grid=(t // tm,), in_specs=in_specs, out_specs=out_specs, out_shape=out_shape,
        compiler_params=_cparams("parallel"), name="normmod_proj",
    )(x, g.reshape(1, d), shift, scale, *ws)


def _proj_res_kernel(*refs, n_a):
    a_refs = refs[:n_a]
    w_refs = refs[n_a:2 * n_a]
    res_ref, gate_ref, o_ref = refs[2 * n_a:]
    acc = jnp.dot(a_refs[0][...], w_refs[0][...], preferred_element_type=F32)
    for a_ref, w_ref in zip(a_refs[1:], w_refs[1:]):
        acc += jnp.dot(a_ref[...], w_ref[...], preferred_element_type=F32)
    o_ref[...] = res_ref[...] + gate_ref[0] * acc


def _proj_residual(a_list, w_list, res, gate, bid, tm):
    t, n = res.shape
    n_a = len(a_list)
    in_specs = [pl.BlockSpec((tm, a.shape[1]), lambda i: (i, 0)) for a in a_list]
    in_specs += [_full(w.shape) for w in w_list]
    in_specs += [pl.BlockSpec((tm, n), lambda i: (i, 0)), pl.BlockSpec((1, 1, n), lambda i: (bid(i), 0, 0))]
    return pl.pallas_call(
        functools.partial(_proj_res_kernel, n_a=n_a),
        grid=(t // tm,), in_specs=in_specs, out_specs=pl.BlockSpec((tm, n), lambda i: (i, 0)),
        out_shape=jax.ShapeDtypeStruct((t, n), F32),
        compiler_params=_cparams("parallel"), name="proj_residual",
    )(*a_list, *w_list, res, gate)


def _rope_kernel(q_ref, k_ref, c_ref, s_ref, qo_ref, ko_ref):
    c = c_ref[...]
    s = s_ref[...]
    lane = lax.broadcasted_iota(I32, c.shape, 1)
    first = (lane % 32) < 16
    width = q_ref.shape[1]

    def rot(x):
        partner = jnp.where(first, pltpu.roll(x, LANES - 16, 1), pltpu.roll(x, 16, 1))
        return x * c + partner * s

    for g in range(width // LANES):
        sl = slice(g * LANES, (g + 1) * LANES)
        qo_ref[:, sl] = (rot(q_ref[:, sl]) * (DA_HEAD_DIM ** -0.5)).astype(qo_ref.dtype)
        ko_ref[:, sl] = rot(k_ref[:, sl]).astype(ko_ref.dtype)


def _rope_tables(seq_len, tm):
    rows = seq_len // GRID_W
    row = jnp.repeat(jnp.arange(rows, dtype=F32), GRID_W)
    col = (jnp.arange(seq_len) % GRID_W).astype(F32)
    inv = ROPE_BASE ** (-jnp.arange(ROPE_NFREQ, dtype=F32) / ROPE_NFREQ)
    ang_r = row[:, None] * inv
    ang_c = col[:, None] * inv
    cos64 = jnp.concatenate([jnp.cos(ang_r), jnp.cos(ang_r), jnp.cos(ang_c), jnp.cos(ang_c)], axis=1)
    sin64 = jnp.concatenate([-jnp.sin(ang_r), jnp.sin(ang_r), -jnp.sin(ang_c), jnp.sin(ang_c)], axis=1)
    cos = jnp.concatenate([jnp.tile(cos64, (1, 2)), jnp.ones((tm, LANES), F32)], axis=0)
    sin = jnp.concatenate([jnp.tile(sin64, (1, 2)), jnp.zeros((tm, LANES), F32)], axis=0)
    return cos, sin


def _rope(q, k, cos, sin, tab_block, tm):
    t, w = q.shape
    row = pl.BlockSpec((tm, w), lambda i: (i, 0))
    tab = pl.BlockSpec((tm, LANES), lambda i: (tab_block(i), 0))
    return pl.pallas_call(
        _rope_kernel, grid=(t // tm,), in_specs=[row, row, tab, tab], out_specs=[row, row],
        out_shape=[jax.ShapeDtypeStruct((t, w), BF16)] * 2,
        compiler_params=_cparams("parallel"), name="rope",
    )(q, k, cos, sin)


def _attn_kernel(lam_ref, sub_ref, q_ref, k_ref, v_ref, o_ref, *, lam_init):
    lp = lam_ref[...]
    lam = (jnp.exp(jnp.sum(lp[0:1] * lp[1:2], keepdims=True))
           - jnp.exp(jnp.sum(lp[2:3] * lp[3:4], keepdims=True)) + lam_init)
    q = q_ref[...]
    k = k_ref[...]
    v = v_ref[...]
    lane = lax.broadcasted_iota(I32, q.shape, 1)
    outs = []
    for m in range(2):
        sel = (lane < DA_HEAD_DIM) if m == 0 else (lane >= DA_HEAD_DIM)
        s = _nt(jnp.where(sel, q, jnp.zeros_like(q)), k)
        p = jnp.exp(s - jnp.max(s, axis=-1, keepdims=True))
        denom = jnp.sum(p, axis=-1, keepdims=True)
        outs.append(jnp.dot(p.astype(BF16), v, preferred_element_type=F32) / denom)
    o = outs[0] - lam * outs[1]
    o = o * lax.rsqrt(jnp.mean(o * o, axis=-1, keepdims=True) + EPS) * sub_ref[...] * (1.0 - lam_init)
    o_ref[...] = o.astype(o_ref.dtype)


def _diff_attention(q, k, v, lamp, subln, lam_init, nb, lq, lk, q_row0, tq):
    w = q.shape[1]
    heads = w // DA_V_DIM
    nq = lq // tq
    qb0 = q_row0 // tq
    return pl.pallas_call(
        functools.partial(_attn_kernel, lam_init=lam_init),
        grid=(nb, heads, nq),
        in_specs=[_full(lamp.shape), _full((1, DA_V_DIM)),
                  pl.BlockSpec((tq, DA_V_DIM), lambda b, h, i: (qb0 + b * nq + i, h)),
                  pl.BlockSpec((lk, DA_V_DIM), lambda b, h, i: (b, h)),
                  pl.BlockSpec((lk, DA_V_DIM), lambda b, h, i: (b, h))],
        out_specs=pl.BlockSpec((tq, DA_V_DIM), lambda b, h, i: (b * nq + i, h)),
        out_shape=jax.ShapeDtypeStruct((nb * lq, w), BF16),
        compiler_params=_cparams("parallel", "parallel", "arbitrary"), name="diff_attention",
    )(lamp, subln.reshape(1, DA_V_DIM), q, k, v)


def _softplus(z):
    return jnp.maximum(z, 0.0) + jnp.log(1.0 + jnp.exp(-jnp.abs(z)))


def _rw_prep_kernel(u_ref, prev_ref, next_ref, mu_ref, w0_ref, w2_ref, a0_ref, a2_ref, g2_ref, kk_ref, ka_ref,
                    rk_ref, ones_ref, r_ref, v_ref, nkk_ref, g_ref, bonus_ref, lw_ref, kd_ref, bd_ref):
    u = u_ref[...]
    tm = u.shape[0]
    width = r_ref.shape[1]
    row = lax.broadcasted_iota(I32, u.shape, 0)
    up = jnp.where(row == 0, prev_ref[0], pltpu.roll(u, 1, 0))
    dn = jnp.where(row == tm - 1, next_ref[0], pltpu.roll(u, tm - 1, 0))
    u = u + mu_ref[...] * (0.5 * (up + dn) - u)
    r = u[:, :width]
    k = u[:, width:2 * width]
    v = u[:, 2 * width:3 * width]
    o = 3 * width
    w_in = u[:, o:o + LANES]
    a_in = u[:, o + LANES:o + 2 * LANES]
    g_in = u[:, o + 2 * LANES:o + 3 * LANES]
    ones = ones_ref[...]
    hsum = lambda t: jnp.dot(t, ones, preferred_element_type=F32, precision=HI)
    g = jnp.dot(jax.nn.sigmoid(g_in), g2_ref[...], preferred_element_type=F32, precision=HI)
    kk = k * kk_ref[...]
    kk = kk / jnp.maximum(jnp.sqrt(hsum(kk * kk)), 1e-12)
    w_log = -_softplus(-(w0_ref[...] + jnp.dot(jnp.tanh(w_in), w2_ref[...], preferred_element_type=F32,
                                               precision=HI))) - 0.5
    logw = -jnp.exp(w_log)
    a = jax.nn.sigmoid(a0_ref[...] + jnp.dot(a_in, a2_ref[...], preferred_element_type=F32, precision=HI))
    ksum = jnp.zeros_like(k)
    for d in range(2):
        a_d = a[:, d * width:(d + 1) * width]
        k_d = k * (1.0 + (a_d - 1.0) * ka_ref[...])
        ksum = ksum + k_d
        lw_ref[d] = logw[:, d * width:(d + 1) * width]
        kd_ref[d] = k_d
        bd_ref[d] = kk * a_d
    r_ref[...] = r
    v_ref[...] = v
    nkk_ref[...] = -kk
    g_ref[...] = g
    bonus_ref[...] = hsum(r * ksum * rk_ref[...]) * v


def _halo_rows(x, tm, seq_lens):
    t = x.shape[0]
    nt = t // tm
    starts = np.cumsum([0] + [n for n in seq_lens])[:-1]
    ends = np.cumsum(seq_lens)
    tile_start = np.arange(nt) * tm
    has_prev = ~np.isin(tile_start, starts)
    has_next = ~np.isin(tile_start + tm, ends)
    last = x[tm - 1::tm]
    first = x[0::tm]
    zero = jnp.zeros_like(first[:1])
    prev = jnp.concatenate([zero, last[:-1]], axis=0) * jnp.asarray(has_prev, x.dtype)[:, None]
    nxt = jnp.concatenate([first[1:], zero], axis=0) * jnp.asarray(has_next, x.dtype)[:, None]
    return prev[:, None, :], nxt[:, None, :]


def _block_diag2(m):
    z = jnp.zeros_like(m[0])
    return jnp.concatenate([jnp.concatenate([m[0], z], axis=1), jnp.concatenate([z, m[1]], axis=1)], axis=0)


def _head_ones(width, hd):
    idx = np.arange(width) // hd
    return jnp.asarray((idx[:, None] == idx[None, :]).astype(np.float32))


def _rwkv_prepare(u, seq_lens, tm, mu, w0, w2, a0, a2, g2, k_k, k_a, r_k):
    t, cols = u.shape
    width = k_k.shape[0]
    prev, nxt = _halo_rows(u, tm, seq_lens)
    row = lambda c: pl.BlockSpec((tm, c), lambda i: (i, 0))
    halo = pl.BlockSpec((1, 1, cols), lambda i: (i, 0, 0))
    dir_out = pl.BlockSpec((2, tm, width), lambda i: (0, i, 0))
    consts = [mu.reshape(1, cols), w0.reshape(1, 2 * width), _block_diag2(w2), a0.reshape(1, 2 * width),
              _block_diag2(a2), g2, k_k.reshape(1, width), k_a.reshape(1, width), r_k.reshape(1, width),
              _head_ones(width, RW_HEAD_DIM)]
    f = jax.ShapeDtypeStruct((t, width), F32)
    f2 = jax.ShapeDtypeStruct((2, t, width), F32)
    return pl.pallas_call(
        _rw_prep_kernel, grid=(t // tm,),
        in_specs=[row(cols), halo, halo] + [_full(c.shape) for c in consts],
        out_specs=[row(width)] * 5 + [dir_out] * 3,
        out_shape=[f] * 5 + [f2] * 3,
        compiler_params=_cparams("parallel"), name="rwkv_prepare",
    )(u, prev, nxt, *consts)


def _mm(x, y, passes):
    if passes == 6:
        return jnp.dot(x, y, preferred_element_type=F32, precision=HI)
    dot = lambda p, q: jnp.dot(p, q, preferred_element_type=F32)
    if passes == 1:
        return dot(x.astype(BF16), y.astype(BF16))
    xh, xl = _split_bf16(x)
    yh, yl = _split_bf16(y)
    return dot(xh, yh) + (dot(xh, yl) + dot(xl, yh))


def _rw_scan_kernel(r_ref, v_ref, nkk_ref, lw_ref, kd_ref, bd_ref, h0_ref, y_ref, hf_ref, h_scr):
    d = pl.program_id(1)
    c = pl.program_id(2)
    nc = pl.num_programs(2)
    cs = RW_CHUNK
    pairs = h_scr.shape[0]
    sgn = 1 - 2 * d

    @pl.when(c == 0)
    def _():
        h_scr[...] = h0_ref[0, 0]

    ri = lax.broadcasted_iota(I32, (cs, cs), 0)
    ci = lax.broadcasted_iota(I32, (cs, cs), 1)
    before_eq = jnp.where((ri - ci) * sgn >= 0, 1.0, 0.0).astype(BF16)
    lane = lax.broadcasted_iota(I32, (cs, LANES), 1)
    lo_half = lane < RW_HEAD_DIM
    n2 = 2 * cs
    rt = lax.broadcasted_iota(I32, (n2, n2), 0)
    ct = lax.broadcasted_iota(I32, (n2, n2), 1)
    dtok = ((rt & (cs - 1)) - (ct & (cs - 1))) * sgn
    strict = dtok > 0
    incl = dtok >= 0
    eye = rt == ct

    def stack2(x):
        return jnp.concatenate([jnp.where(lo_half, x, 0.0), jnp.where(lo_half, 0.0, x)], axis=0)

    mm = functools.partial(_mm, passes=RW_PASSES)
    mm_state = functools.partial(_mm, passes=RW_STATE_PASSES)

    h_in = [h_scr[p] for p in range(pairs)]
    ys, h_out = [], []
    for p in range(pairs):
        sl = slice(p * LANES, (p + 1) * LANES)
        logw = lw_ref[0, :, sl]
        r = r_ref[:, sl]
        v = v_ref[:, sl]
        a = nkk_ref[:, sl]
        k = kd_ref[0, :, sl]
        b = bd_ref[0, :, sl]

        lw_hi, lw_lo = _split_bf16(logw)
        cum = (jnp.dot(before_eq, lw_hi, preferred_element_type=F32)
               + jnp.dot(before_eq, lw_lo, preferred_element_type=F32))
        mid = cum[cs // 2:cs // 2 + 1]
        tot = jnp.sum(logw, axis=0, keepdims=True)
        e_in = jnp.exp(mid - cum)
        e_end = jnp.exp(tot - cum)

        a2 = stack2(a * jnp.exp(cum - logw - mid))
        r2 = stack2(r * jnp.exp(cum - mid))
        a2_abs = stack2(a * jnp.exp(cum - logw))
        r2_abs = stack2(r * jnp.exp(cum))
        b2 = stack2(b * e_in)
        k2 = stack2(k * e_in)
        v2 = stack2(v)
        bh2 = stack2(b * e_end)
        kh2 = stack2(k * e_end)

        b2t = b2.T
        k2t = k2.T
        nmat = jnp.where(strict, mm(a2, b2t), 0.0)
        mmat = jnp.where(strict, mm(a2, k2t), 0.0)
        qb = jnp.where(incl, mm(r2, b2t), 0.0)
        qk = jnp.where(incl, mm(r2, k2t), 0.0)

        tinv = jnp.where(eye, 1.0, 0.0) + nmat
        pw = nmat
        for _ in range(int(math.log2(cs)) - 1):
            pw = mm(pw, pw)
            tinv = tinv + mm(tinv, pw)

        w2 = mm(tinv, mm(mmat, v2))
        a2p = mm(tinv, a2_abs)
        y_intra = mm(qk, v2) + mm(qb, w2)
        r2p = r2_abs + mm(qb, a2p)
        bh2t = bh2.T
        gmat = jnp.where(eye, jnp.exp(tot), 0.0) + mm(bh2t, a2p)
        dmat = mm(bh2t, w2) + mm(kh2.T, v2)

        h = h_in[p]
        y2 = y_intra + mm_state(r2p, h)
        ys.append(y2[:cs] + y2[cs:])
        h_out.append(mm_state(gmat, h) + dmat)

    y_ref[0] = jnp.concatenate(ys, axis=1)
    for p in range(pairs):
        h_scr[p] = h_out[p]

    @pl.when(c == nc - 1)
    def _():
        hf_ref[0, 0] = h_scr[...]


def _rwkv_scan(r, v, nkk, lw, kd, bd, h0, nb, seq_len, row0):
    w = r.shape[1]
    pairs = w // LANES
    nc = seq_len // RW_CHUNK
    rb0 = row0 // RW_CHUNK

    def chunk(c, d):
        return c + d * (nc - 1 - 2 * c)

    shared = pl.BlockSpec((RW_CHUNK, w), lambda b, d, c: (rb0 + b * nc + chunk(c, d), 0))
    perdir = pl.BlockSpec((1, RW_CHUNK, w), lambda b, d, c: (d, rb0 + b * nc + chunk(c, d), 0))
    state = pl.BlockSpec((1, 1, pairs, LANES, LANES), lambda b, d, c: (d, b, 0, 0, 0))
    return pl.pallas_call(
        _rw_scan_kernel, grid=(nb, 2, nc),
        in_specs=[shared, shared, shared, perdir, perdir, perdir, state],
        out_specs=[pl.BlockSpec((1, RW_CHUNK, w), lambda b, d, c: (d, b * nc + chunk(c, d), 0)), state],
        out_shape=[jax.ShapeDtypeStruct((2, nb * seq_len, w), F32), jax.ShapeDtypeStruct(h0.shape, F32)],
        scratch_shapes=[pltpu.VMEM((pairs, LANES, LANES), F32)],
        compiler_params=_cparams("parallel", "parallel", "arbitrary"), name="rwkv_scan",
    )(r, v, nkk, lw, kd, bd, h0)


def _rw_post_kernel(y_ref, g_ref, bonus_ref, lnw_ref, lnb_ref, ones_ref, o_ref):
    y = y_ref[0] + y_ref[1]
    ones = ones_ref[...]
    hmean = lambda t: jnp.dot(t, ones, preferred_element_type=F32, precision=HI) * (1.0 / RW_HEAD_DIM)
    yc = y - hmean(y)
    var = hmean(yc * yc)
    yn = yc * lax.rsqrt(var + RW_GN_EPS) * lnw_ref[...] + lnb_ref[...]
    o_ref[...] = ((yn + bonus_ref[...]) * g_ref[...]).astype(o_ref.dtype)


def _rwkv_post(y, g, bonus, lnw, lnb, tm):
    _, t, w = y.shape
    row = pl.BlockSpec((tm, w), lambda i: (i, 0))
    return pl.pallas_call(
        _rw_post_kernel, grid=(t // tm,),
        in_specs=[pl.BlockSpec((2, tm, w), lambda i: (0, i, 0)), row, row, _full((1, w)), _full((1, w)),
                  _full((w, w))],
        out_specs=row, out_shape=jax.ShapeDtypeStruct((t, w), BF16),
        compiler_params=_cparams("parallel"), name="rwkv_post",
    )(y, g, bonus, lnw.reshape(1, w), lnb.reshape(1, w), _head_ones(w, RW_HEAD_DIM))


def _rwkv_mixer(u, seq_lens, nb, seq, ctx_len, tm, mu, w0, w2, a0, a2, g2, k_k, k_a, r_k, lnx_w, lnx_b):
    t_lat = nb * seq
    r, vv, nkk, g, bonus, lw, kd, bd = _rwkv_prepare(u, seq_lens, tm, mu, w0, w2, a0, a2, g2, k_k, k_a, r_k)
    zero = jnp.zeros((2, nb, r.shape[1] // LANES, LANES, LANES), F32)
    y_ctx, h_ctx = _rwkv_scan(r, vv, nkk, lw, kd, bd, zero, nb, ctx_len, t_lat)
    y_lat, _ = _rwkv_scan(r, vv, nkk, lw, kd, bd, h_ctx, nb, seq, 0)
    y = jnp.concatenate([y_lat, y_ctx], axis=1)
    return _rwkv_post(y, g, bonus, lnx_w, lnx_b, tm)


def _extract_topk(s, order, payload, count):
    big = float(2 ** 24)
    vals, pays = [], []
    for _ in range(count):
        m = jnp.max(s, axis=0, keepdims=True)
        first = jnp.min(jnp.where(s == m, order, big), axis=0, keepdims=True)
        hit = order == first
        vals.append(m)
        pays.append(first if payload is None else jnp.sum(jnp.where(hit, payload, 0.0), axis=0, keepdims=True))
        s = jnp.where(hit, -jnp.inf, s)
    return jnp.concatenate(vals, axis=0), jnp.concatenate(pays, axis=0)


def _pruned_candidates(v1, i1, v2, i2):
    k = PEER_TOPK
    tt = v1.shape[1]
    row8 = lax.broadcasted_iota(I32, (8, tt), 0).astype(F32)
    row16 = lax.broadcasted_iota(I32, (k, tt), 0).astype(F32)
    nk = float(PEER_NKEYS)
    sums, flats, eids = [], [], []

    def add(valid, s, flat, eid):
        unused = float(k * k + 16 * len(sums))
        sums.append(s if valid is None else jnp.where(valid, s, -jnp.inf))
        flats.append(flat if valid is None else jnp.where(valid, flat, flat + unused))
        eids.append(eid)

    def vary_j(i, rows, nvalid):
        r = row16 if rows == k else row8
        add(None if nvalid == rows else r < nvalid, v1[i:i + 1] + v2[:rows], r + float(i * k),
            i1[i:i + 1] * nk + i2[:rows])

    def vary_i(j, i0, lo, hi):
        r = row8 + float(i0)
        add(None if (lo == i0 and hi == i0 + 8) else (r >= lo) & (r < hi), v1[i0:i0 + 8] + v2[j:j + 1],
            r * float(k) + float(j), i1[i0:i0 + 8] * nk + i2[j:j + 1])

    vary_j(0, k, k)
    vary_j(1, 8, 8)
    vary_j(2, 8, 5)
    vary_j(3, 8, 4)
    vary_i(0, 8, 8, 16)
    vary_i(0, 0, 4, 8)
    vary_i(1, 0, 4, 8)
    vary_i(2, 0, 4, 5)
    return jnp.concatenate(sums, axis=0), jnp.concatenate(flats, axis=0), jnp.concatenate(eids, axis=0)


def _peer_topk_kernel(q_ref, keys_ref, idx_ref, gate_ref):
    tt = q_ref.shape[0]
    kpos = lax.broadcasted_iota(I32, (PEER_NKEYS, tt), 0).astype(F32)

    def head(h, carry):
        vs, ids = [], []
        for p in range(2):
            col = pl.multiple_of((2 * h + p) * LANES, LANES)
            s = _nt(keys_ref[h, p], q_ref[:, pl.ds(col, LANES)], HI)
            v_p, i_p = _extract_topk(s, kpos, None, PEER_TOPK)
            vs.append(v_p)
            ids.append(i_p)
        cand, flat, eid = _pruned_candidates(vs[0], ids[0], vs[1], ids[1])
        top_s, top_i = _extract_topk(cand, flat, eid, PEER_TOPK)
        e = jnp.exp(top_s - top_s[0:1])
        rows = pl.ds(pl.multiple_of(h * PEER_TOPK, PEER_TOPK), PEER_TOPK)
        gate_ref[rows, :] = e / jnp.sum(e, axis=0, keepdims=True)
        idx_ref[rows, :] = top_i.astype(I32)
        return carry

    lax.fori_loop(0, PEER_HEADS, head, 0)


def _peer_topk(q, keys, tt):
    t = q.shape[0]
    ne = PEER_HEADS * PEER_TOPK
    out = pl.BlockSpec((ne, tt), lambda i: (0, i))
    return pl.pallas_call(
        _peer_topk_kernel, grid=(t // tt,),
        in_specs=[pl.BlockSpec((tt, q.shape[1]), lambda i: (i, 0)), _full(keys.shape)],
        out_specs=[out, out],
        out_shape=[jax.ShapeDtypeStruct((ne, t), I32), jax.ShapeDtypeStruct((ne, t), F32)],
        compiler_params=_cparams("parallel"), name="peer_topk",
    )(q, keys)


def _sc_gather(table, idx):
    n = idx.shape[0]
    r = table.shape[1]
    workers = SC_CORES * SC_SUBCORES
    per_worker = n // workers
    nwin = per_worker // SC_WINDOW
    assert n == workers * nwin * SC_WINDOW and nwin % SC_NBUF == 0
    mesh = plsc.VectorSubcoreMesh(core_axis_name="c", subcore_axis_name="s")

    def body(table_hbm, idx_hbm, out_hbm, idx_v, *rest):
        bufs = rest[:SC_NBUF]
        gsem = rest[SC_NBUF:2 * SC_NBUF]
        osem = rest[2 * SC_NBUF:]
        base = (lax.axis_index("s") * SC_CORES + lax.axis_index("c")) * per_worker
        pltpu.sync_copy(idx_hbm.at[pl.ds(base, per_worker)], idx_v)

        def gather(w, b):
            return pltpu.make_async_copy(table_hbm.at[idx_v.at[pl.ds(w * SC_WINDOW, SC_WINDOW)]], bufs[b], gsem[b])

        def put(w, b):
            return pltpu.make_async_copy(bufs[b], out_hbm.at[pl.ds(base + w * SC_WINDOW, SC_WINDOW)], osem[b])

        for b in range(SC_NBUF):
            gather(b, b).start()

        @pl.loop(0, nwin, step=SC_NBUF)
        def _(w0):
            for b in range(SC_NBUF):
                w = w0 + b
                gather(w, b).wait()
                put(w, b).start()
                put(w, b).wait()

                @pl.when(w + SC_NBUF < nwin)
                def _():
                    gather(w + SC_NBUF, b).start()

    return pl.kernel(
        body, mesh=mesh, out_type=jax.ShapeDtypeStruct((n, r), table.dtype),
        scratch_types=[pltpu.VMEM((per_worker,), I32)] + [pltpu.VMEM((SC_WINDOW, r), table.dtype)] * SC_NBUF
        + [pltpu.SemaphoreType.DMA] * (2 * SC_NBUF),
    )(table, idx)


def _peer_apply_kernel(rows_ref, h_ref, gates_ref, x_ref, mod_ref, nf_ref, *rest, final_norm):
    o_ref = rest[-1]
    tt, ne = gates_ref.shape
    eye = (lax.broadcasted_iota(I32, (ne, ne), 0) == lax.broadcasted_iota(I32, (ne, ne), 1))
    outs = []
    for t in range(tt):
        packed = rows_ref[t * ne:(t + 1) * ne, :]
        u = pltpu.bitcast(packed << 16, F32)
        v = pltpu.bitcast(packed & jnp.uint32(0xFFFF0000), F32)
        pre = jnp.sum(u * h_ref[t:t + 1, :], axis=1, keepdims=True)
        pre = jnp.sum(jnp.where(eye, pre, 0.0), axis=0, keepdims=True)
        act = 0.5 * pre * (1.0 + lax.erf(pre * (2.0 ** -0.5)))
        w = jnp.sum(jnp.where(eye, gates_ref[t:t + 1, :] * act, 0.0), axis=1, keepdims=True)
        outs.append(jnp.sum(v * w, axis=0, keepdims=True))
    y = x_ref[...] + mod_ref[0] * jnp.concatenate(outs, axis=0)
    if final_norm:
        y = y * lax.rsqrt(jnp.mean(y * y, axis=-1, keepdims=True) + EPS) * nf_ref[...]
    o_ref[...] = y


def _peer_apply(rows, h, gates, x, acc, gate_mod, norm_f, token0, bid, final_norm):
    t, d = x.shape
    n, ne = gates.shape
    tt = PEER_TOK
    blk0 = token0 // tt
    local = lambda c: pl.BlockSpec((tt, c), lambda i: (i, 0))
    glob = pl.BlockSpec((tt, d), lambda i: (blk0 + i, 0))
    in_specs = [pl.BlockSpec((tt * ne, d), lambda i: (i, 0)), local(d), local(ne), glob,
                pl.BlockSpec((1, 1, d), lambda i: (bid(blk0 + i), 0, 0)), _full((1, d))]
    args = [rows, h, gates, x, gate_mod, norm_f.reshape(1, d)]
    aliases = {}
    if acc is not None:
        in_specs.append(pl.BlockSpec(memory_space=pl.ANY))
        args.append(acc)
        aliases = {len(args) - 1: 0}
    return pl.pallas_call(
        functools.partial(_peer_apply_kernel, final_norm=final_norm),
        grid=(n // tt,), in_specs=in_specs, out_specs=glob, out_shape=jax.ShapeDtypeStruct((t, d), F32),
        input_output_aliases=aliases,
        compiler_params=_cparams("parallel"), name="peer_apply",
    )(*args)


def _peer_start(x, norm2, shift, scale, wq, keys, uv, bid_fn, tm):
    per = x.shape[0] // PEER_CHUNKS
    assert x.shape[0] == per * PEER_CHUNKS and per % tm == 0
    chunks = []
    for k in range(PEER_CHUNKS):
        q, h = _normmod_proj(x, norm2, shift, scale, [wq], [F32], bid_fn(tm), tm, want_h=True, row0=k * per,
                             nrows=per)
        idx_t, gates_t = _peer_topk(q, keys, LANES)
        rows = _sc_gather(uv, idx_t.T.reshape(per * idx_t.shape[0]))
        chunks.append((rows, h, gates_t.T))
    return chunks


def _peer_finish(x, chunks, gate_mod, norm_f, bid_fn, final_norm):
    per = x.shape[0] // PEER_CHUNKS
    acc = None
    for k, (rows, h, gates) in enumerate(chunks):
        acc = _peer_apply(rows, h, gates, x, acc, gate_mod, norm_f, k * per, bid_fn(PEER_TOK), final_norm)
    return acc


def _conv_kernel(x_ref, prev_ref, next_ref, w_ref, b_ref, dtr_ref, dtb_ref, o_ref, dt_ref, *, heads):
    x = x_ref[...]
    tm = x.shape[0]
    row = lax.broadcasted_iota(I32, x.shape, 0)
    up = jnp.where(row == 0, prev_ref[0], pltpu.roll(x, 1, 0))
    dn = jnp.where(row == tm - 1, next_ref[0], pltpu.roll(x, tm - 1, 0))
    y = up * w_ref[0:1] + x * w_ref[1:2] + dn * w_ref[2:3] + b_ref[...]
    o_ref[...] = y * jax.nn.sigmoid(y)
    lane = lax.broadcasted_iota(I32, (tm, LANES), 1)
    for d in range(2):
        dt_ref[d] = jnp.where(lane < heads, _softplus(dtr_ref[d] + dtb_ref[d]), 0.0)


def _mamba_conv(xbc, dt_raw, seq_lens, tm, conv_w, conv_b, dt_bias_pad, heads):
    t, c = xbc.shape
    prev, nxt = _halo_rows(xbc, tm, seq_lens)
    row = pl.BlockSpec((tm, c), lambda i: (i, 0))
    halo = pl.BlockSpec((1, 1, c), lambda i: (i, 0, 0))
    dts = pl.BlockSpec((2, tm, LANES), lambda i: (0, i, 0))
    return pl.pallas_call(
        functools.partial(_conv_kernel, heads=heads), grid=(t // tm,),
        in_specs=[row, halo, halo, _full(conv_w.shape), _full((1, c)), dts, _full((2, 1, LANES))],
        out_specs=[row, dts],
        out_shape=[jax.ShapeDtypeStruct((t, c), F32), jax.ShapeDtypeStruct((2, t, LANES), F32)],
        compiler_params=_cparams("parallel"), name="mamba_conv",
    )(xbc, prev, nxt, conv_w, conv_b.reshape(1, c), dt_raw, dt_bias_pad)


def _ssd_kernel(*refs, reverse, inner, groups, add_prev):
    if add_prev:
        xbc_ref, dt_ref, dtt_ref, alr_ref, alc_ref, rep_ref, h0_ref, yin_ref, y_ref, hf_ref, h_scr = refs
    else:
        xbc_ref, dt_ref, dtt_ref, alr_ref, alc_ref, rep_ref, h0_ref, y_ref, hf_ref, h_scr = refs
        yin_ref = None
    c = pl.program_id(1)
    nc = pl.num_programs(1)
    cs = M_CHUNK
    gw = inner // groups
    hpg = gw // M_HEAD_DIM

    @pl.when(c == 0)
    def _():
        h_scr[...] = h0_ref[0]

    dt = dt_ref[0]
    dtt = dtt_ref[0]
    a = dt * (-jnp.exp(alr_ref[...]))
    at = dtt * (-jnp.exp(alc_ref[...]))
    ri = lax.broadcasted_iota(I32, (cs, cs), 0)
    ci = lax.broadcasted_iota(I32, (cs, cs), 1)
    incl = (ri <= ci) if reverse else (ri >= ci)
    tri = jnp.where(incl, 1.0, 0.0)
    cum = jnp.dot(tri, a, preferred_element_type=F32, precision=HI)
    cumt = _nt(at, tri, HI)
    tot = jnp.sum(a, axis=0, keepdims=True)

    rep = rep_ref[...]

    def spread(t):
        hi, lo = _split_bf16(t)
        return jnp.dot(hi, rep, preferred_element_type=F32) + jnp.dot(lo, rep, preferred_element_type=F32)

    e_cum = spread(jnp.exp(cum))
    e_end = spread(jnp.exp(tot - cum) * dt)
    e_tot = spread(jnp.broadcast_to(jnp.exp(tot), (8, LANES)))[0:1]

    lane = lax.broadcasted_iota(I32, (cs, LANES), 1)
    lo_half = lane < M_HEAD_DIM
    ys = []
    for g in range(groups):
        bg32 = xbc_ref[:, inner + g * M_STATE:inner + (g + 1) * M_STATE]
        bg = bg32.astype(BF16)
        cg = xbc_ref[:, inner + groups * M_STATE + g * M_STATE:inner + groups * M_STATE + (g + 1) * M_STATE]
        cg = cg.astype(BF16)
        cb = _nt(cg, bg)
        hprev = h_scr[g]
        xg = xbc_ref[:, g * gw:(g + 1) * gw]
        y_off = jnp.dot(cg, hprev.astype(BF16), preferred_element_type=F32) * e_cum[:, g * gw:(g + 1) * gw]
        xd = (xg * e_end[:, g * gw:(g + 1) * gw]).astype(BF16)
        h_scr[g] = e_tot[:, g * gw:(g + 1) * gw] * hprev + jnp.dot(bg32.T.astype(BF16), xd, preferred_element_type=F32)
        for j in range(hpg // 2):
            xpair = xg[:, j * LANES:(j + 1) * LANES].astype(BF16)
            halves = []
            for hh in range(2):
                h = g * hpg + 2 * j + hh
                seg = jnp.minimum(cum[:, h:h + 1] - cumt[h:h + 1, :], 0.0)
                m = jnp.where(incl, cb * jnp.exp(seg), 0.0) * dtt[h:h + 1, :]
                halves.append(jnp.dot(m.astype(BF16), xpair, preferred_element_type=F32))
            ys.append(jnp.where(lo_half, halves[0], halves[1]) + y_off[:, j * LANES:(j + 1) * LANES])
    y = jnp.concatenate(ys, axis=1)
    if add_prev:
        y = y + yin_ref[...]
    y_ref[...] = y

    @pl.when(c == nc - 1)
    def _():
        hf_ref[0] = h_scr[...]


def _ssd_pass(xbc, dt, dtt, a_log, h0, y_prev, nb, seq_len, row0, reverse, inner, groups, heads):
    nc = seq_len // M_CHUNK
    rb0 = row0 // M_CHUNK
    c_all = xbc.shape[1]
    gw = inner // groups
    alr = jnp.zeros((1, LANES), F32).at[0, :heads].set(a_log)
    alc = jnp.broadcast_to(jnp.zeros((LANES,), F32).at[:heads].set(a_log)[:, None], (LANES, LANES))
    hid = np.arange(inner) // M_HEAD_DIM
    rep = jnp.asarray((np.arange(LANES)[:, None] == hid[None, :]).astype(np.float32), BF16)
    chunk = (lambda c: nc - 1 - c) if reverse else (lambda c: c)
    add_prev = y_prev is not None
    in_specs = [pl.BlockSpec((M_CHUNK, c_all), lambda b, c: (rb0 + b * nc + chunk(c), 0)),
                pl.BlockSpec((1, M_CHUNK, LANES), lambda b, c: (0, rb0 + b * nc + chunk(c), 0)),
                pl.BlockSpec((1, LANES, M_CHUNK), lambda b, c: (0, 0, rb0 + b * nc + chunk(c))),
                _full((1, LANES)), _full((LANES, LANES)), _full((LANES, inner)),
                pl.BlockSpec((1, groups, M_STATE, gw), lambda b, c: (b, 0, 0, 0))]
    args = [xbc, dt, dtt, alr, alc, rep, h0]
    yspec = pl.BlockSpec((M_CHUNK, inner), lambda b, c: (b * nc + chunk(c), 0))
    if add_prev:
        in_specs.append(yspec)
        args.append(y_prev)
    return pl.pallas_call(
        functools.partial(_ssd_kernel, reverse=reverse, inner=inner, groups=groups, add_prev=add_prev),
        grid=(nb, nc), in_specs=in_specs,
        out_specs=[yspec, pl.BlockSpec((1, groups, M_STATE, gw), lambda b, c: (b, 0, 0, 0))],
        out_shape=[jax.ShapeDtypeStruct((nb * seq_len, inner), F32), jax.ShapeDtypeStruct(h0.shape, F32)],
        scratch_shapes=[pltpu.VMEM((groups, M_STATE, gw), F32)],
        compiler_params=_cparams("parallel", "arbitrary"), name="ssd_pass",
    )(*args)


def _mamba_gate_kernel(y_ref, x_ref, z_ref, dsk_ref, gn_ref, o_ref, *, groups):
    z = z_ref[...]
    y = (y_ref[...] + dsk_ref[...] * x_ref[...]) * (z * jax.nn.sigmoid(z))
    gw = y.shape[1] // groups
    for g in range(groups):
        yg = y[:, g * gw:(g + 1) * gw]
        yg = yg * lax.rsqrt(jnp.mean(yg * yg, axis=-1, keepdims=True) + EPS) * gn_ref[:, g * gw:(g + 1) * gw]
        o_ref[:, g * gw:(g + 1) * gw] = yg.astype(o_ref.dtype)


def _mamba_gate(y, xbc, z, d_skip_cols, gnorm, groups, tm):
    t, inner = y.shape
    row = pl.BlockSpec((tm, inner), lambda i: (i, 0))
    return pl.pallas_call(
        functools.partial(_mamba_gate_kernel, groups=groups), grid=(t // tm,),
        in_specs=[row, row, row, _full((1, inner)), _full((1, inner))],
        out_specs=row, out_shape=jax.ShapeDtypeStruct((t, inner), BF16),
        compiler_params=_cparams("parallel"), name="mamba_gate",
    )(y, xbc, z, d_skip_cols.reshape(1, inner), gnorm.reshape(1, inner))


def _even_mixer(xs, mods, nb, seq, ctx_len, tm, bid_fn, norm1, win, da_lambda, da_subln, rw_mu, rw_w0, rw_w2, rw_a0,
                rw_a2, rw_g2, rw_kk, rw_ka, rw_rk, rw_lnx_w, rw_lnx_b, wout, lam_init):
    d = xs.shape[1]
    t_lat = nb * seq
    da_w = d // 2
    rw_w = d - da_w
    winb = win.astype(BF16)
    ws = [winb[:, :da_w], winb[:, da_w:2 * da_w], winb[:, 2 * da_w:3 * da_w], winb[:, 3 * da_w:]]
    q, k, v, u = _normmod_proj(xs, norm1, mods[0], mods[1], ws, [F32, F32, BF16, F32], bid_fn(tm), tm)

    cos, sin = _rope_tables(seq, tm)
    lat_tiles = t_lat // tm
    tab_block = lambda i: jnp.where(i < lat_tiles, i % (seq // tm), seq // tm)
    qr, kr = _rope(q, k, cos, sin, tab_block, tm)
    lk = ctx_len + seq
    cat = lambda a: jnp.concatenate([a[t_lat:].reshape(nb, ctx_len, da_w), a[:t_lat].reshape(nb, seq, da_w)],
                                    axis=1).reshape(nb * lk, da_w)
    tq = min(256, seq)
    o_lat = _diff_attention(qr, cat(kr), cat(v), da_lambda, da_subln, lam_init, nb, seq, lk, 0, tq)
    tqc = min(256, ctx_len)
    o_ctx = _diff_attention(qr, kr[t_lat:], v[t_lat:], da_lambda, da_subln, lam_init, nb, ctx_len, ctx_len,
                            t_lat, tqc)
    o_att = jnp.concatenate([o_lat, o_ctx], axis=0)

    seq_lens = [seq] * nb + [ctx_len] * nb
    o_rw = _rwkv_mixer(u, seq_lens, nb, seq, ctx_len, tm, rw_mu, rw_w0, rw_w2, rw_a0, rw_a2, rw_g2, rw_kk, rw_ka,
                       rw_rk, rw_lnx_w, rw_lnx_b)

    woutb = wout.astype(BF16)
    return _proj_residual([o_att, o_rw], [woutb[:da_w], woutb[da_w:]], xs, mods[2], bid_fn(tm), tm)


def _odd_mixer_last(xs, mods, nb, seq, ctx_len, tm, bid_fn, norm1, win, conv_w, conv_b, dt_bias, a_log, d_skip, gnorm,
                    wout):
    d = xs.shape[1]
    t_lat = nb * seq
    inner = wout.shape[0]
    heads = a_log.shape[1]
    conv_dim = conv_w.shape[1]
    groups = (conv_dim - inner) // (2 * M_STATE)
    winb = win.astype(BF16)
    pad = jnp.zeros((d, LANES - heads), BF16)
    w_dt = [jnp.concatenate([winb[:, inner + conv_dim + k * heads:inner + conv_dim + (k + 1) * heads], pad], axis=1)
            for k in range(2)]
    ws = [winb[:, :inner], winb[:, inner:inner + conv_dim]] + w_dt
    z, xbc_raw, dtr_f, dtr_b = _normmod_proj(xs, norm1, mods[0], mods[1], ws, [F32] * 4, bid_fn(tm), tm)
    seq_lens = [seq] * nb + [ctx_len] * nb
    dtb = jnp.zeros((2, 1, LANES), F32).at[:, 0, :heads].set(dt_bias)
    xbc, dt = _mamba_conv(xbc_raw, jnp.stack([dtr_f, dtr_b]), seq_lens, tm, conv_w, conv_b, dtb, heads)
    dtt = jnp.swapaxes(dt, 1, 2)
    h0 = jnp.zeros((nb, groups, M_STATE, inner // groups), F32)
    ssd = functools.partial(_ssd_pass, xbc, inner=inner, groups=groups, heads=heads)
    _, hf = ssd(dt[0:1], dtt[0:1], a_log[0], h0, None, nb, ctx_len, t_lat, False)
    _, hb = ssd(dt[1:2], dtt[1:2], a_log[1], h0, None, nb, ctx_len, t_lat, True)
    y, _ = ssd(dt[0:1], dtt[0:1], a_log[0], hf, None, nb, seq, 0, False)
    y, _ = ssd(dt[1:2], dtt[1:2], a_log[1], hb, y, nb, seq, 0, True)
    x_lat = xs[:t_lat]
    gated = _mamba_gate(y, xbc, z, jnp.repeat(d_skip, M_HEAD_DIM), gnorm, groups, tm)
    return _proj_residual([gated], [wout.astype(BF16)], x_lat, mods[2], bid_fn(tm), tm)


def kernel(x, c, ctx, c_ctx, ada_w_0, ada_b_0, norm1_0, norm2_0, win_0, da_lambda_0, da_subln_0, rw_mu_0, rw_w0_0, rw_w2_0, rw_a0_0, rw_a2_0, rw_g2_0, rw_kk_0, rw_ka_0, rw_rk_0, rw_lnx_w_0, rw_lnx_b_0, wout_0, peer_q_0, peer_keys_0, peer_u_0, peer_v_0, ada_w_1, ada_b_1, norm1_1, norm2_1, win_1, conv_w_1, conv_b_1, dt_bias_1, a_log_1, d_skip_1, gnorm_1, wout_1, peer_q_1, peer_keys_1, peer_u_1, peer_v_1, norm_f):
    nb_all, seq, d = x.shape
    ctx_len = ctx.shape[1]
    tm = 256 if (seq % 256 == 0 and ctx_len % 256 == 0) else 128
    assert seq % tm == 0 and ctx_len % tm == 0 and seq % GRID_W == 0 and nb_all % BATCH_STREAMS == 0
    nb = nb_all // BATCH_STREAMS

    def bid_fn(tile):
        per = seq // tile
        return lambda i: jnp.minimum(i // per, nb)

    def expert_rows(u, v):
        bits = lambda a: lax.bitcast_convert_type(a.astype(BF16), jnp.uint16).astype(jnp.uint32)
        return bits(u) | (bits(v) << 16)

    uv0 = expert_rows(peer_u_0, peer_v_0)
    uv1 = expert_rows(peer_u_1, peer_v_1)
    wq0 = peer_q_0.astype(BF16)
    wq1 = peer_q_1.astype(BF16)
    lam_init = 0.8 - 0.6 * math.exp(-0.3 * 0)

    xs, mods0, mods1 = [], [], []
    for s in range(BATCH_STREAMS):
        sl = slice(s * nb, (s + 1) * nb)
        xs.append(jnp.concatenate([x[sl].reshape(nb * seq, d), ctx[sl].reshape(nb * ctx_len, d)], axis=0))
        cvecs = jnp.zeros((16, d), F32).at[:nb].set(c[sl]).at[nb].set(c_ctx)
        mods0.append(_ada_mod(cvecs, ada_w_0, ada_b_0))
        mods1.append(_ada_mod(cvecs, ada_w_1, ada_b_1))

    def mix0(s):
        return _even_mixer(xs[s], mods0[s], nb, seq, ctx_len, tm, bid_fn, norm1_0, win_0, da_lambda_0, da_subln_0,
                           rw_mu_0, rw_w0_0, rw_w2_0, rw_a0_0, rw_a2_0, rw_g2_0, rw_kk_0, rw_ka_0, rw_rk_0,
                           rw_lnx_w_0, rw_lnx_b_0, wout_0, lam_init)

    def mix1(s, xin):
        return _odd_mixer_last(xin, mods1[s], nb, seq, ctx_len, tm, bid_fn, norm1_1, win_1, conv_w_1, conv_b_1,
                               dt_bias_1, a_log_1, d_skip_1, gnorm_1, wout_1)

    def peer0_start(s, xin):
        return _peer_start(xin, norm2_0, mods0[s][3], mods0[s][4], wq0, peer_keys_0, uv0, bid_fn, tm)

    def peer1_start(s, xin):
        return _peer_start(xin, norm2_1, mods1[s][3], mods1[s][4], wq1, peer_keys_1, uv1, bid_fn, tm)

    assert BATCH_STREAMS == 2
    a, b = 0, 1
    xa = mix0(a)
    pa = peer0_start(a, xa)
    xb = mix0(b)
    xa = _peer_finish(xa, pa, mods0[a][5], norm2_0, bid_fn, False)
    pb = peer0_start(b, xb)
    xa = mix1(a, xa)
    xb = _peer_finish(xb, pb, mods0[b][5], norm2_0, bid_fn, False)
    pa = peer1_start(a, xa)
    xb = mix1(b, xb)
    out_a = _peer_finish(xa, pa, mods1[a][5], norm_f, bid_fn, True)
    pb = peer1_start(b, xb)
    out_b = _peer_finish(xb, pb, mods1[b][5], norm_f, bid_fn, True)
    return jnp.concatenate([out_a.reshape(nb, seq, d), out_b.reshape(nb, seq, d)], axis=0)
```

```python
import functools
import math

import jax
import jax.numpy as jnp
import numpy as np
from jax import lax
from jax.experimental import pallas as pl
from jax.experimental.pallas import tpu as pltpu
from jax.experimental.pallas import tpu_sc as plsc

F32 = jnp.float32
BF16 = jnp.bfloat16
I32 = jnp.int32
HI = lax.Precision.HIGHEST

EPS = 1e-6
N_MOD = 6
GRID_W = 64
LANES = 128
VMEM_LIMIT_BYTES = 48 * 1024 * 1024

DA_HEAD_DIM = 64
DA_V_DIM = 128
ROPE_BASE = 10000.0
ROPE_NFREQ = DA_HEAD_DIM // 4
RW_HEAD_DIM = 64
RW_GN_EPS = 64e-5
RW_CHUNK = 64
RW_PASSES = 1
RW_STATE_PASSES = 3
M_HEAD_DIM = 64
M_STATE = 128
M_CHUNK = 128
PEER_HEADS = 8
PEER_NKEYS = 128
PEER_TOPK = 16
PEER_TOK = 8
PEER_CHUNKS = 2
BATCH_STREAMS = 4
SC_CORES = 2
SC_SUBCORES = 16
SC_WINDOW = 16
SC_NBUF = 4


def _cparams(*sem):
    return pltpu.CompilerParams(dimension_semantics=sem, vmem_limit_bytes=VMEM_LIMIT_BYTES)


def _nt(a, b, precision=None):
    return lax.dot_general(a, b, (((1,), (1,)), ((), ())), preferred_element_type=F32, precision=precision)


def _full(shape):
    nd = len(shape)
    return pl.BlockSpec(shape, lambda *_: (0,) * nd)


def _split_bf16(x):
    hi = x.astype(BF16)
    lo = (x - hi.astype(F32)).astype(BF16)
    return hi, lo


def _ada_kernel(c_ref, w_ref, b_ref, o_ref):
    c = c_ref[...]
    s = c * jax.nn.sigmoid(c)
    o_ref[...] = jnp.dot(s, w_ref[...], preferred_element_type=F32, precision=HI) + b_ref[...]


def _ada_mod(cvecs, w, b):
    r, d = cvecs.shape
    n = w.shape[1]
    tn = 1024
    m = pl.pallas_call(
        _ada_kernel,
        grid=(n // tn,),
        in_specs=[_full((r, d)), pl.BlockSpec((d, tn), lambda j: (0, j)), pl.BlockSpec((1, tn), lambda j: (0, j))],
        out_specs=pl.BlockSpec((r, tn), lambda j: (0, j)),
        out_shape=jax.ShapeDtypeStruct((r, n), F32),
        compiler_params=_cparams("parallel"),
        name="ada_mod",
    )(cvecs, w, b.reshape(1, n))
    return [m[:, k * d:(k + 1) * d].reshape(r, 1, d) for k in range(N_MOD)]


def _normmod_kernel(x_ref, g_ref, sh_ref, sc_ref, *refs, n_w, want_h):
    x = x_ref[...]
    y = x * lax.rsqrt(jnp.mean(x * x, axis=-1, keepdims=True) + EPS) * g_ref[...]
    h = y * (1.0 + sc_ref[0]) + sh_ref[0]
    hb = h.astype(BF16)
    for w_ref, o_ref in zip(refs[:n_w], refs[n_w:2 * n_w]):
        o_ref[...] = jnp.dot(hb, w_ref[...], preferred_element_type=F32).astype(o_ref.dtype)
    if want_h:
        refs[2 * n_w][...] = h


def _normmod_proj(x, g, shift, scale, ws, out_dtypes, bid, tm, want_h=False, row0=0, nrows=None):
    d = x.shape[1]
    t = x.shape[0] if nrows is None else nrows
    blk0 = row0 // tm
    n_w = len(ws)
    in_specs = [pl.BlockSpec((tm, d), lambda i: (blk0 + i, 0)), _full((1, d)),
                pl.BlockSpec((1, 1, d), lambda i: (bid(blk0 + i), 0, 0)),
                pl.BlockSpec((1, 1, d), lambda i: (bid(blk0 + i), 0, 0))]
    in_specs += [_full(w.shape) for w in ws]
    out_specs = [pl.BlockSpec((tm, w.shape[1]), lambda i: (i, 0)) for w in ws]
    out_shape = [jax.ShapeDtypeStruct((t, w.shape[1]), dt) for w, dt in zip(ws, out_dtypes)]
    if want_h:
        out_specs.append(pl.BlockSpec((tm, d), lambda i: (i, 0)))
        out_shape.append(jax.ShapeDtypeStruct((t, d), F32))
    return pl.pallas_call(
        functools.partial(_normmod_kernel, n_w=n_w, want_h=want_h),
        grid=(t // tm,), in_specs=in_specs, out_specs=out_specs, out_shape=out_shape,
        compiler_params=_cparams("parallel"), name="normmod_proj",
    )(x, g.reshape(1, d), shift, scale, *ws)


def _proj_res_kernel(*refs, n_a):
    a_refs = refs[:n_a]
    w_refs = refs[n_a:2 * n_a]
    res_ref, gate_ref, o_ref = refs[2 * n_a:]
    acc = jnp.dot(a_refs[0][...], w_refs[0][...], preferred_element_type=F32)
    for a_ref, w_ref in zip(a_refs[1:], w_refs[1:]):
        acc += jnp.dot(a_ref[...], w_ref[...], preferred_element_type=F32)
    o_ref[...] = res_ref[...] + gate_ref[0] * acc


def _proj_residual(a_list, w_list, res, gate, bid, tm):
    t, n = res.shape
    n_a = len(a_list)
    in_specs = [pl.BlockSpec((tm, a.shape[1]), lambda i: (i, 0)) for a in a_list]
    in_specs += [_full(w.shape) for w in w_list]
    in_specs += [pl.BlockSpec((tm, n), lambda i: (i, 0)), pl.BlockSpec((1, 1, n), lambda i: (bid(i), 0, 0))]
    return pl.pallas_call(
        functools.partial(_proj_res_kernel, n_a=n_a),
        grid=(t // tm,), in_specs=in_specs, out_specs=pl.BlockSpec((tm, n), lambda i: (i, 0)),
        out_shape=jax.ShapeDtypeStruct((t, n), F32),
        compiler_params=_cparams("parallel"), name="proj_residual",
    )(*a_list, *w_list, res, gate)


def _rope_kernel(q_ref, k_ref, c_ref, s_ref, qo_ref, ko_ref):
    c = c_ref[...]
    s = s_ref[...]
    lane = lax.broadcasted_iota(I32, c.shape, 1)
    first = (lane % 32) < 16
    width = q_ref.shape[1]

    def rot(x):
        partner = jnp.where(first, pltpu.roll(x, LANES - 16, 1), pltpu.roll(x, 16, 1))
        return x * c + partner * s

    for g in range(width // LANES):
        sl = slice(g * LANES, (g + 1) * LANES)
        qo_ref[:, sl] = (rot(q_ref[:, sl]) * (DA_HEAD_DIM ** -0.5)).astype(qo_ref.dtype)
        ko_ref[:, sl] = rot(k_ref[:, sl]).astype(ko_ref.dtype)


def _rope_tables(seq_len, tm):
    rows = seq_len // GRID_W
    row = jnp.repeat(jnp.arange(rows, dtype=F32), GRID_W)
    col = (jnp.arange(seq_len) % GRID_W).astype(F32)
    inv = ROPE_BASE ** (-jnp.arange(ROPE_NFREQ, dtype=F32) / ROPE_NFREQ)
    ang_r = row[:, None] * inv
    ang_c = col[:, None] * inv
    cos64 = jnp.concatenate([jnp.cos(ang_r), jnp.cos(ang_r), jnp.cos(ang_c), jnp.cos(ang_c)], axis=1)
    sin64 = jnp.concatenate([-jnp.sin(ang_r), jnp.sin(ang_r), -jnp.sin(ang_c), jnp.sin(ang_c)], axis=1)
    cos = jnp.concatenate([jnp.tile(cos64, (1, 2)), jnp.ones((tm, LANES), F32)], axis=0)
    sin = jnp.concatenate([jnp.tile(sin64, (1, 2)), jnp.zeros((tm, LANES), F32)], axis=0)
    return cos, sin


def _rope(q, k, cos, sin, tab_block, tm):
    t, w = q.shape
    row = pl.BlockSpec((tm, w), lambda i: (i, 0))
    tab = pl.BlockSpec((tm, LANES), lambda i: (tab_block(i), 0))
    return pl.pallas_call(
        _rope_kernel, grid=(t // tm,), in_specs=[row, row, tab, tab], out_specs=[row, row],
        out_shape=[jax.ShapeDtypeStruct((t, w), BF16)] * 2,
        compiler_params=_cparams("parallel"), name="rope",
    )(q, k, cos, sin)


def _attn_kernel(lam_ref, sub_ref, q_ref, k_ref, v_ref, o_ref, *, lam_init):
    lp = lam_ref[...]
    lam = (jnp.exp(jnp.sum(lp[0:1] * lp[1:2], keepdims=True))
           - jnp.exp(jnp.sum(lp[2:3] * lp[3:4], keepdims=True)) + lam_init)
    q = q_ref[...]
    k = k_ref[...]
    v = v_ref[...]
    lane = lax.broadcasted_iota(I32, q.shape, 1)
    outs = []
    for m in range(2):
        sel = (lane < DA_HEAD_DIM) if m == 0 else (lane >= DA_HEAD_DIM)
        s = _nt(jnp.where(sel, q, jnp.zeros_like(q)), k)
        p = jnp.exp(s - jnp.max(s, axis=-1, keepdims=True))
        denom = jnp.sum(p, axis=-1, keepdims=True)
        outs.append(jnp.dot(p.astype(BF16), v, preferred_element_type=F32) / denom)
    o = outs[0] - lam * outs[1]
    o = o * lax.rsqrt(jnp.mean(o * o, axis=-1, keepdims=True) + EPS) * sub_ref[...] * (1.0 - lam_init)
    o_ref[...] = o.astype(o_ref.dtype)


def _diff_attention(q, k, v, lamp, subln, lam_init, nb, lq, lk, q_row0, tq):
    w = q.shape[1]
    heads = w // DA_V_DIM
    nq = lq // tq
    qb0 = q_row0 // tq
    return pl.pallas_call(
        functools.partial(_attn_kernel, lam_init=lam_init),
        grid=(nb, heads, nq),
        in_specs=[_full(lamp.shape), _full((1, DA_V_DIM)),
                  pl.BlockSpec((tq, DA_V_DIM), lambda b, h, i: (qb0 + b * nq + i, h)),
                  pl.BlockSpec((lk, DA_V_DIM), lambda b, h, i: (b, h)),
                  pl.BlockSpec((lk, DA_V_DIM), lambda b, h, i: (b, h))],
        out_specs=pl.BlockSpec((tq, DA_V_DIM), lambda b, h, i: (b * nq + i, h)),
        out_shape=jax.ShapeDtypeStruct((nb * lq, w), BF16),
        compiler_params=_cparams("parallel", "parallel", "arbitrary"), name="diff_attention",
    )(lamp, subln.reshape(1, DA_V_DIM), q, k, v)


def _softplus(z):
    return jnp.maximum(z, 0.0) + jnp.log(1.0 + jnp.exp(-jnp.abs(z)))


def _rw_prep_kernel(u_ref, prev_ref, next_ref, mu_ref, w0_ref, w2_ref, a0_ref, a2_ref, g2_ref, kk_ref, ka_ref,
                    rk_ref, ones_ref, r_ref, v_ref, nkk_ref, g_ref, bonus_ref, lw_ref, kd_ref, bd_ref):
    u = u_ref[...]
    tm = u.shape[0]
    width = r_ref.shape[1]
    row = lax.broadcasted_iota(I32, u.shape, 0)
    up = jnp.where(row == 0, prev_ref[0], pltpu.roll(u, 1, 0))
    dn = jnp.where(row == tm - 1, next_ref[0], pltpu.roll(u, tm - 1, 0))
    u = u + mu_ref[...] * (0.5 * (up + dn) - u)
    r = u[:, :width]
    k = u[:, width:2 * width]
    v = u[:, 2 * width:3 * width]
    o = 3 * width
    w_in = u[:, o:o + LANES]
    a_in = u[:, o + LANES:o + 2 * LANES]
    g_in = u[:, o + 2 * LANES:o + 3 * LANES]
    ones = ones_ref[...]
    hsum = lambda t: jnp.dot(t, ones, preferred_element_type=F32, precision=HI)
    g = jnp.dot(jax.nn.sigmoid(g_in), g2_ref[...], preferred_element_type=F32, precision=HI)
    kk = k * kk_ref[...]
    kk = kk / jnp.maximum(jnp.sqrt(hsum(kk * kk)), 1e-12)
    w_log = -_softplus(-(w0_ref[...] + jnp.dot(jnp.tanh(w_in), w2_ref[...], preferred_element_type=F32,
                                               precision=HI))) - 0.5
    logw = -jnp.exp(w_log)
    a = jax.nn.sigmoid(a0_ref[...] + jnp.dot(a_in, a2_ref[...], preferred_element_type=F32, precision=HI))
    ksum = jnp.zeros_like(k)
    for d in range(2):
        a_d = a[:, d * width:(d + 1) * width]
        k_d = k * (1.0 + (a_d - 1.0) * ka_ref[...])
        ksum = ksum + k_d
        lw_ref[d] = logw[:, d * width:(d + 1) * width]
        kd_ref[d] = k_d
        bd_ref[d] = kk * a_d
    r_ref[...] = r
    v_ref[...] = v
    nkk_ref[...] = -kk
    g_ref[...] = g
    bonus_ref[...] = hsum(r * ksum * rk_ref[...]) * v


def _halo_rows(x, tm, seq_lens):
    t = x.shape[0]
    nt = t // tm
    starts = np.cumsum([0] + [n for n in seq_lens])[:-1]
    ends = np.cumsum(seq_lens)
    tile_start = np.arange(nt) * tm
    has_prev = ~np.isin(tile_start, starts)
    has_next = ~np.isin(tile_start + tm, ends)
    last = x[tm - 1::tm]
    first = x[0::tm]
    zero = jnp.zeros_like(first[:1])
    prev = jnp.concatenate([zero, last[:-1]], axis=0) * jnp.asarray(has_prev, x.dtype)[:, None]
    nxt = jnp.concatenate([first[1:], zero], axis=0) * jnp.asarray(has_next, x.dtype)[:, None]
    return prev[:, None, :], nxt[:, None, :]


def _block_diag2(m):
    z = jnp.zeros_like(m[0])
    return jnp.concatenate([jnp.concatenate([m[0], z], axis=1), jnp.concatenate([z, m[1]], axis=1)], axis=0)


def _head_ones(width, hd):
    idx = np.arange(width) // hd
    return jnp.asarray((idx[:, None] == idx[None, :]).astype(np.float32))


def _rwkv_prepare(u, seq_lens, tm, mu, w0, w2, a0, a2, g2, k_k, k_a, r_k):
    t, cols = u.shape
    width = k_k.shape[0]
    prev, nxt = _halo_rows(u, tm, seq_lens)
    row = lambda c: pl.BlockSpec((tm, c), lambda i: (i, 0))
    halo = pl.BlockSpec((1, 1, cols), lambda i: (i, 0, 0))
    dir_out = pl.BlockSpec((2, tm, width), lambda i: (0, i, 0))
    consts = [mu.reshape(1, cols), w0.reshape(1, 2 * width), _block_diag2(w2), a0.reshape(1, 2 * width),
              _block_diag2(a2), g2, k_k.reshape(1, width), k_a.reshape(1, width), r_k.reshape(1, width),
              _head_ones(width, RW_HEAD_DIM)]
    f = jax.ShapeDtypeStruct((t, width), F32)
    f2 = jax.ShapeDtypeStruct((2, t, width), F32)
    return pl.pallas_call(
        _rw_prep_kernel, grid=(t // tm,),
        in_specs=[row(cols), halo, halo] + [_full(c.shape) for c in consts],
        out_specs=[row(width)] * 5 + [dir_out] * 3,
        out_shape=[f] * 5 + [f2] * 3,
        compiler_params=_cparams("parallel"), name="rwkv_prepare",
    )(u, prev, nxt, *consts)


def _mm(x, y, passes):
    if passes == 6:
        return jnp.dot(x, y, preferred_element_type=F32, precision=HI)
    dot = lambda p, q: jnp.dot(p, q, preferred_element_type=F32)
    if passes == 1:
        return dot(x.astype(BF16), y.astype(BF16))
    xh, xl = _split_bf16(x)
    yh, yl = _split_bf16(y)
    return dot(xh, yh) + (dot(xh, yl) + dot(xl, yh))


def _rw_scan_kernel(r_ref, v_ref, nkk_ref, lw_ref, kd_ref, bd_ref, h0_ref, y_ref, hf_ref, h_scr):
    d = pl.program_id(1)
    c = pl.program_id(2)
    nc = pl.num_programs(2)
    cs = RW_CHUNK
    pairs = h_scr.shape[0]
    sgn = 1 - 2 * d

    @pl.when(c == 0)
    def _():
        h_scr[...] = h0_ref[0, 0]

    ri = lax.broadcasted_iota(I32, (cs, cs), 0)
    ci = lax.broadcasted_iota(I32, (cs, cs), 1)
    before_eq = jnp.where((ri - ci) * sgn >= 0, 1.0, 0.0).astype(BF16)
    lane = lax.broadcasted_iota(I32, (cs, LANES), 1)
    lo_half = lane < RW_HEAD_DIM
    n2 = 2 * cs
    rt = lax.broadcasted_iota(I32, (n2, n2), 0)
    ct = lax.broadcasted_iota(I32, (n2, n2), 1)
    dtok = ((rt & (cs - 1)) - (ct & (cs - 1))) * sgn
    strict = dtok > 0
    incl = dtok >= 0
    eye = rt == ct

    def stack2(x):
        return jnp.concatenate([jnp.where(lo_half, x, 0.0), jnp.where(lo_half, 0.0, x)], axis=0)

    mm = functools.partial(_mm, passes=RW_PASSES)
    mm_state = functools.partial(_mm, passes=RW_STATE_PASSES)

    h_in = [h_scr[p] for p in range(pairs)]
    ys, h_out = [], []
    for p in range(pairs):
        sl = slice(p * LANES, (p + 1) * LANES)
        logw = lw_ref[0, :, sl]
        r = r_ref[:, sl]
        v = v_ref[:, sl]
        a = nkk_ref[:, sl]
        k = kd_ref[0, :, sl]
        b = bd_ref[0, :, sl]

        lw_hi, lw_lo = _split_bf16(logw)
        cum = (jnp.dot(before_eq, lw_hi, preferred_element_type=F32)
               + jnp.dot(before_eq, lw_lo, preferred_element_type=F32))
        mid = cum[cs // 2:cs // 2 + 1]
        tot = jnp.sum(logw, axis=0, keepdims=True)
        e_in = jnp.exp(mid - cum)
        e_end = jnp.exp(tot - cum)

        a2 = stack2(a * jnp.exp(cum - logw - mid))
        r2 = stack2(r * jnp.exp(cum - mid))
        a2_abs = stack2(a * jnp.exp(cum - logw))
        r2_abs = stack2(r * jnp.exp(cum))
        b2 = stack2(b * e_in)
        k2 = stack2(k * e_in)
        v2 = stack2(v)
        bh2 = stack2(b * e_end)
        kh2 = stack2(k * e_end)

        b2t = b2.T
        k2t = k2.T
        nmat = jnp.where(strict, mm(a2, b2t), 0.0)
        mmat = jnp.where(strict, mm(a2, k2t), 0.0)
        qb = jnp.where(incl, mm(r2, b2t), 0.0)
        qk = jnp.where(incl, mm(r2, k2t), 0.0)

        tinv = jnp.where(eye, 1.0, 0.0) + nmat
        pw = nmat
        for _ in range(int(math.log2(cs)) - 1):
            pw = mm(pw, pw)
            tinv = tinv + mm(tinv, pw)

        w2 = mm(tinv, mm(mmat, v2))
        a2p = mm(tinv, a2_abs)
        y_intra = mm(qk, v2) + mm(qb, w2)
        r2p = r2_abs + mm(qb, a2p)
        bh2t = bh2.T
        gmat = jnp.where(eye, jnp.exp(tot), 0.0) + mm(bh2t, a2p)
        dmat = mm(bh2t, w2) + mm(kh2.T, v2)

        h = h_in[p]
        y2 = y_intra + mm_state(r2p, h)
        ys.append(y2[:cs] + y2[cs:])
        h_out.append(mm_state(gmat, h) + dmat)

    y_ref[0] = jnp.concatenate(ys, axis=1)
    for p in range(pairs):
        h_scr[p] = h_out[p]

    @pl.when(c == nc - 1)
    def _():
        hf_ref[0, 0] = h_scr[...]


def _rwkv_scan(r, v, nkk, lw, kd, bd, h0, nb, seq_len, row0):
    w = r.shape[1]
    pairs = w // LANES
    nc = seq_len // RW_CHUNK
    rb0 = row0 // RW_CHUNK

    def chunk(c, d):
        return c + d * (nc - 1 - 2 * c)

    shared = pl.BlockSpec((RW_CHUNK, w), lambda b, d, c: (rb0 + b * nc + chunk(c, d), 0))
    perdir = pl.BlockSpec((1, RW_CHUNK, w), lambda b, d, c: (d, rb0 + b * nc + chunk(c, d), 0))
    state = pl.BlockSpec((1, 1, pairs, LANES, LANES), lambda b, d, c: (d, b, 0, 0, 0))
    return pl.pallas_call(
        _rw_scan_kernel, grid=(nb, 2, nc),
        in_specs=[shared, shared, shared, perdir, perdir, perdir, state],
        out_specs=[pl.BlockSpec((1, RW_CHUNK, w), lambda b, d, c: (d, b * nc + chunk(c, d), 0)), state],
        out_shape=[jax.ShapeDtypeStruct((2, nb * seq_len, w), F32), jax.ShapeDtypeStruct(h0.shape, F32)],
        scratch_shapes=[pltpu.VMEM((pairs, LANES, LANES), F32)],
        compiler_params=_cparams("parallel", "parallel", "arbitrary"), name="rwkv_scan",
    )(r, v, nkk, lw, kd, bd, h0)


def _rw_post_kernel(y_ref, g_ref, bonus_ref, lnw_ref, lnb_ref, ones_ref, o_ref):
    y = y_ref[0] + y_ref[1]
    ones = ones_ref[...]
    hmean = lambda t: jnp.dot(t, ones, preferred_element_type=F32, precision=HI) * (1.0 / RW_HEAD_DIM)
    yc = y - hmean(y)
    var = hmean(yc * yc)
    yn = yc * lax.rsqrt(var + RW_GN_EPS) * lnw_ref[...] + lnb_ref[...]
    o_ref[...] = ((yn + bonus_ref[...]) * g_ref[...]).astype(o_ref.dtype)


def _rwkv_post(y, g, bonus, lnw, lnb, tm):
    _, t, w = y.shape
    row = pl.BlockSpec((tm, w), lambda i: (i, 0))
    return pl.pallas_call(
        _rw_post_kernel, grid=(t // tm,),
        in_specs=[pl.BlockSpec((2, tm, w), lambda i: (0, i, 0)), row, row, _full((1, w)), _full((1, w)),
                  _full((w, w))],
        out_specs=row, out_shape=jax.ShapeDtypeStruct((t, w), BF16),
        compiler_params=_cparams("parallel"), name="rwkv_post",
    )(y, g, bonus, lnw.reshape(1, w), lnb.reshape(1, w), _head_ones(w, RW_HEAD_DIM))


def _rwkv_mixer(u, seq_lens, nb, seq, ctx_len, tm, mu, w0, w2, a0, a2, g2, k_k, k_a, r_k, lnx_w, lnx_b):
    t_lat = nb * seq
    r, vv, nkk, g, bonus, lw, kd, bd = _rwkv_prepare(u, seq_lens, tm, mu, w0, w2, a0, a2, g2, k_k, k_a, r_k)
    zero = jnp.zeros((2, nb, r.shape[1] // LANES, LANES, LANES), F32)
    y_ctx, h_ctx = _rwkv_scan(r, vv, nkk, lw, kd, bd, zero, nb, ctx_len, t_lat)
    y_lat, _ = _rwkv_scan(r, vv, nkk, lw, kd, bd, h_ctx, nb, seq, 0)
    y = jnp.concatenate([y_lat, y_ctx], axis=1)
    return _rwkv_post(y, g, bonus, lnx_w, lnx_b, tm)


def _extract_topk(s, order, payload, count):
    big = float(2 ** 24)
    vals, pays = [], []
    for _ in range(count):
        m = jnp.max(s, axis=0, keepdims=True)
        first = jnp.min(jnp.where(s == m, order, big), axis=0, keepdims=True)
        hit = order == first
        vals.append(m)
        pays.append(first if payload is None else jnp.sum(jnp.where(hit, payload, 0.0), axis=0, keepdims=True))
        s = jnp.where(hit, -jnp.inf, s)
    return jnp.concatenate(vals, axis=0), jnp.concatenate(pays, axis=0)


def _pruned_candidates(v1, i1, v2, i2):
    k = PEER_TOPK
    tt = v1.shape[1]
    row8 = lax.broadcasted_iota(I32, (8, tt), 0).astype(F32)
    row16 = lax.broadcasted_iota(I32, (k, tt), 0).astype(F32)
    nk = float(PEER_NKEYS)
    sums, flats, eids = [], [], []

    def add(valid, s, flat, eid):
        unused = float(k * k + 16 * len(sums))
        sums.append(s if valid is None else jnp.where(valid, s, -jnp.inf))
        flats.append(flat if valid is None else jnp.where(valid, flat, flat + unused))
        eids.append(eid)

    def vary_j(i, rows, nvalid):
        r = row16 if rows == k else row8
        add(None if nvalid == rows else r < nvalid, v1[i:i + 1] + v2[:rows], r + float(i * k),
            i1[i:i + 1] * nk + i2[:rows])

    def vary_i(j, i0, lo, hi):
        r = row8 + float(i0)
        add(None if (lo == i0 and hi == i0 + 8) else (r >= lo) & (r < hi), v1[i0:i0 + 8] + v2[j:j + 1],
            r * float(k) + float(j), i1[i0:i0 + 8] * nk + i2[j:j + 1])

    vary_j(0, k, k)
    vary_j(1, 8, 8)
    vary_j(2, 8, 5)
    vary_j(3, 8, 4)
    vary_i(0, 8, 8, 16)
    vary_i(0, 0, 4, 8)
    vary_i(1, 0, 4, 8)
    vary_i(2, 0, 4, 5)
    return jnp.concatenate(sums, axis=0), jnp.concatenate(flats, axis=0), jnp.concatenate(eids, axis=0)


def _peer_topk_kernel(q_ref, keys_ref, idx_ref, gate_ref):
    tt = q_ref.shape[0]
    kpos = lax.broadcasted_iota(I32, (PEER_NKEYS, tt), 0).astype(F32)

    def head(h, carry):
        vs, ids = [], []
        for p in range(2):
            col = pl.multiple_of((2 * h + p) * LANES, LANES)
            s = _nt(keys_ref[h, p], q_ref[:, pl.ds(col, LANES)], HI)
            v_p, i_p = _extract_topk(s, kpos, None, PEER_TOPK)
            vs.append(v_p)
            ids.append(i_p)
        cand, flat, eid = _pruned_candidates(vs[0], ids[0], vs[1], ids[1])
        top_s, top_i = _extract_topk(cand, flat, eid, PEER_TOPK)
        e = jnp.exp(top_s - top_s[0:1])
        rows = pl.ds(pl.multiple_of(h * PEER_TOPK, PEER_TOPK), PEER_TOPK)
        gate_ref[rows, :] = e / jnp.sum(e, axis=0, keepdims=True)
        idx_ref[rows, :] = top_i.astype(I32)
        return carry

    lax.fori_loop(0, PEER_HEADS, head, 0)


def _peer_topk(q, keys, tt):
    t = q.shape[0]
    ne = PEER_HEADS * PEER_TOPK
    out = pl.BlockSpec((ne, tt), lambda i: (0, i))
    return pl.pallas_call(
        _peer_topk_kernel, grid=(t // tt,),
        in_specs=[pl.BlockSpec((tt, q.shape[1]), lambda i: (i, 0)), _full(keys.shape)],
        out_specs=[out, out],
        out_shape=[jax.ShapeDtypeStruct((ne, t), I32), jax.ShapeDtypeStruct((ne, t), F32)],
        compiler_params=_cparams("parallel"), name="peer_topk",
    )(q, keys)


def _sc_gather(table, idx):
    n = idx.shape[0]
    r = table.shape[1]
    workers = SC_CORES * SC_SUBCORES
    per_worker = n // workers
    nwin = per_worker // SC_WINDOW
    assert n == workers * nwin * SC_WINDOW and nwin % SC_NBUF == 0
    mesh = plsc.VectorSubcoreMesh(core_axis_name="c", subcore_axis_name="s")

    def body(table_hbm, idx_hbm, out_hbm, idx_v, *rest):
        bufs = rest[:SC_NBUF]
        gsem = rest[SC_NBUF:2 * SC_NBUF]
        osem = rest[2 * SC_NBUF:]
        base = (lax.axis_index("s") * SC_CORES + lax.axis_index("c")) * per_worker
        pltpu.sync_copy(idx_hbm.at[pl.ds(base, per_worker)], idx_v)

        def gather(w, b):
            return pltpu.make_async_copy(table_hbm.at[idx_v.at[pl.ds(w * SC_WINDOW, SC_WINDOW)]], bufs[b], gsem[b])

        def put(w, b):
            return pltpu.make_async_copy(bufs[b], out_hbm.at[pl.ds(base + w * SC_WINDOW, SC_WINDOW)], osem[b])

        for b in range(SC_NBUF):
            gather(b, b).start()

        @pl.loop(0, nwin, step=SC_NBUF)
        def _(w0):
            for b in range(SC_NBUF):
                w = w0 + b
                gather(w, b).wait()
                put(w, b).start()
                put(w, b).wait()

                @pl.when(w + SC_NBUF < nwin)
                def _():
                    gather(w + SC_NBUF, b).start()

    return pl.kernel(
        body, mesh=mesh, out_type=jax.ShapeDtypeStruct((n, r), table.dtype),
        scratch_types=[pltpu.VMEM((per_worker,), I32)] + [pltpu.VMEM((SC_WINDOW, r), table.dtype)] * SC_NBUF
        + [pltpu.SemaphoreType.DMA] * (2 * SC_NBUF),
    )(table, idx)


def _peer_apply_kernel(rows_ref, h_ref, gates_ref, x_ref, mod_ref, nf_ref, *rest, final_norm):
    o_ref = rest[-1]
    tt, ne = gates_ref.shape
    eye = (lax.broadcasted_iota(I32, (ne, ne), 0) == lax.broadcasted_iota(I32, (ne, ne), 1))
    outs = []
    for t in range(tt):
        packed = rows_ref[t * ne:(t + 1) * ne, :]
        u = pltpu.bitcast(packed << 16, F32)
        v = pltpu.bitcast(packed & jnp.uint32(0xFFFF0000), F32)
        pre = jnp.sum(u * h_ref[t:t + 1, :], axis=1, keepdims=True)
        pre = jnp.sum(jnp.where(eye, pre, 0.0), axis=0, keepdims=True)
        act = 0.5 * pre * (1.0 + lax.erf(pre * (2.0 ** -0.5)))
        w = jnp.sum(jnp.where(eye, gates_ref[t:t + 1, :] * act, 0.0), axis=1, keepdims=True)
        outs.append(jnp.sum(v * w, axis=0, keepdims=True))
    y = x_ref[...] + mod_ref[0] * jnp.concatenate(outs, axis=0)
    if final_norm:
        y = y * lax.rsqrt(jnp.mean(y * y, axis=-1, keepdims=True) + EPS) * nf_ref[...]
    o_ref[...] = y


def _peer_apply(rows, h, gates, x, acc, gate_mod, norm_f, token0, bid, final_norm):
    t, d = x.shape
    n, ne = gates.shape
    tt = PEER_TOK
    blk0 = token0 // tt
    local = lambda c: pl.BlockSpec((tt, c), lambda i: (i, 0))
    glob = pl.BlockSpec((tt, d), lambda i: (blk0 + i, 0))
    in_specs = [pl.BlockSpec((tt * ne, d), lambda i: (i, 0)), local(d), local(ne), glob,
                pl.BlockSpec((1, 1, d), lambda i: (bid(blk0 + i), 0, 0)), _full((1, d))]
    args = [rows, h, gates, x, gate_mod, norm_f.reshape(1, d)]
    aliases = {}
    if acc is not None:
        in_specs.append(pl.BlockSpec(memory_space=pl.ANY))
        args.append(acc)
        aliases = {len(args) - 1: 0}
    return pl.pallas_call(
        functools.partial(_peer_apply_kernel, final_norm=final_norm),
        grid=(n // tt,), in_specs=in_specs, out_specs=glob, out_shape=jax.ShapeDtypeStruct((t, d), F32),
        input_output_aliases=aliases,
        compiler_params=_cparams("parallel"), name="peer_apply",
    )(*args)


def _peer_start(x, norm2, shift, scale, wq, keys, uv, bid_fn, tm):
    per = x.shape[0] // PEER_CHUNKS
    assert x.shape[0] == per * PEER_CHUNKS and per % tm == 0
    chunks = []
    for k in range(PEER_CHUNKS):
        q, h = _normmod_proj(x, norm2, shift, scale, [wq], [F32], bid_fn(tm), tm, want_h=True, row0=k * per,
                             nrows=per)
        idx_t, gates_t = _peer_topk(q, keys, LANES)
        rows = _sc_gather(uv, idx_t.T.reshape(per * idx_t.shape[0]))
        chunks.append((rows, h, gates_t.T))
    return chunks


def _peer_finish(x, chunks, gate_mod, norm_f, bid_fn, final_norm):
    per = x.shape[0] // PEER_CHUNKS
    acc = None
    for k, (rows, h, gates) in enumerate(chunks):
        acc = _peer_apply(rows, h, gates, x, acc, gate_mod, norm_f, k * per, bid_fn(PEER_TOK), final_norm)
    return acc


def _conv_kernel(x_ref, prev_ref, next_ref, w_ref, b_ref, dtr_ref, dtb_ref, o_ref, dt_ref, *, heads):
    x = x_ref[...]
    tm = x.shape[0]
    row = lax.broadcasted_iota(I32, x.shape, 0)
    up = jnp.where(row == 0, prev_ref[0], pltpu.roll(x, 1, 0))
    dn = jnp.where(row == tm - 1, next_ref[0], pltpu.roll(x, tm - 1, 0))
    y = up * w_ref[0:1] + x * w_ref[1:2] + dn * w_ref[2:3] + b_ref[...]
    o_ref[...] = y * jax.nn.sigmoid(y)
    lane = lax.broadcasted_iota(I32, (tm, LANES), 1)
    for d in range(2):
        dt_ref[d] = jnp.where(lane < heads, _softplus(dtr_ref[d] + dtb_ref[d]), 0.0)


def _mamba_conv(xbc, dt_raw, seq_lens, tm, conv_w, conv_b, dt_bias_pad, heads):
    t, c = xbc.shape
    prev, nxt = _halo_rows(xbc, tm, seq_lens)
    row = pl.BlockSpec((tm, c), lambda i: (i, 0))
    halo = pl.BlockSpec((1, 1, c), lambda i: (i, 0, 0))
    dts = pl.BlockSpec((2, tm, LANES), lambda i: (0, i, 0))
    return pl.pallas_call(
        functools.partial(_conv_kernel, heads=heads), grid=(t // tm,),
        in_specs=[row, halo, halo, _full(conv_w.shape), _full((1, c)), dts, _full((2, 1, LANES))],
        out_specs=[row, dts],
        out_shape=[jax.ShapeDtypeStruct((t, c), F32), jax.ShapeDtypeStruct((2, t, LANES), F32)],
        compiler_params=_cparams("parallel"), name="mamba_conv",
    )(xbc, prev, nxt, conv_w, conv_b.reshape(1, c), dt_raw, dt_bias_pad)


def _ssd_kernel(*refs, reverse, inner, groups, add_prev):
    if add_prev:
        xbc_ref, dt_ref, dtt_ref, alr_ref, alc_ref, rep_ref, h0_ref, yin_ref, y_ref, hf_ref, h_scr = refs
    else:
        xbc_ref, dt_ref, dtt_ref, alr_ref, alc_ref, rep_ref, h0_ref, y_ref, hf_ref, h_scr = refs
        yin_ref = None
    c = pl.program_id(1)
    nc = pl.num_programs(1)
    cs = M_CHUNK
    gw = inner // groups
    hpg = gw // M_HEAD_DIM

    @pl.when(c == 0)
    def _():
        h_scr[...] = h0_ref[0]

    dt = dt_ref[0]
    dtt = dtt_ref[0]
    a = dt * (-jnp.exp(alr_ref[...]))
    at = dtt * (-jnp.exp(alc_ref[...]))
    ri = lax.broadcasted_iota(I32, (cs, cs), 0)
    ci = lax.broadcasted_iota(I32, (cs, cs), 1)
    incl = (ri <= ci) if reverse else (ri >= ci)
    tri = jnp.where(incl, 1.0, 0.0)
    cum = jnp.dot(tri, a, preferred_element_type=F32, precision=HI)
    cumt = _nt(at, tri, HI)
    tot = jnp.sum(a, axis=0, keepdims=True)

    rep = rep_ref[...]

    def spread(t):
        hi, lo = _split_bf16(t)
        return jnp.dot(hi, rep, preferred_element_type=F32) + jnp.dot(lo, rep, preferred_element_type=F32)

    e_cum = spread(jnp.exp(cum))
    e_end = spread(jnp.exp(tot - cum) * dt)
    e_tot = spread(jnp.broadcast_to(jnp.exp(tot), (8, LANES)))[0:1]

    lane = lax.broadcasted_iota(I32, (cs, LANES), 1)
    lo_half = lane < M_HEAD_DIM
    ys = []
    for g in range(groups):
        bg32 = xbc_ref[:, inner + g * M_STATE:inner + (g + 1) * M_STATE]
        bg = bg32.astype(BF16)
        cg = xbc_ref[:, inner + groups * M_STATE + g * M_STATE:inner + groups * M_STATE + (g + 1) * M_STATE]
        cg = cg.astype(BF16)
        cb = _nt(cg, bg)
        hprev = h_scr[g]
        xg = xbc_ref[:, g * gw:(g + 1) * gw]
        y_off = jnp.dot(cg, hprev.astype(BF16), preferred_element_type=F32) * e_cum[:, g * gw:(g + 1) * gw]
        xd = (xg * e_end[:, g * gw:(g + 1) * gw]).astype(BF16)
        h_scr[g] = e_tot[:, g * gw:(g + 1) * gw] * hprev + jnp.dot(bg32.T.astype(BF16), xd, preferred_element_type=F32)
        for j in range(hpg // 2):
            xpair = xg[:, j * LANES:(j + 1) * LANES].astype(BF16)
            halves = []
            for hh in range(2):
                h = g * hpg + 2 * j + hh
                seg = jnp.minimum(cum[:, h:h + 1] - cumt[h:h + 1, :], 0.0)
                m = jnp.where(incl, cb * jnp.exp(seg), 0.0) * dtt[h:h + 1, :]
                halves.append(jnp.dot(m.astype(BF16), xpair, preferred_element_type=F32))
            ys.append(jnp.where(lo_half, halves[0], halves[1]) + y_off[:, j * LANES:(j + 1) * LANES])
    y = jnp.concatenate(ys, axis=1)
    if add_prev:
        y = y + yin_ref[...]
    y_ref[...] = y

    @pl.when(c == nc - 1)
    def _():
        hf_ref[0] = h_scr[...]


def _ssd_pass(xbc, dt, dtt, a_log, h0, y_prev, nb, seq_len, row0, reverse, inner, groups, heads):
    nc = seq_len // M_CHUNK
    rb0 = row0 // M_CHUNK
    c_all = xbc.shape[1]
    gw = inner // groups
    alr = jnp.zeros((1, LANES), F32).at[0, :heads].set(a_log)
    alc = jnp.broadcast_to(jnp.zeros((LANES,), F32).at[:heads].set(a_log)[:, None], (LANES, LANES))
    hid = np.arange(inner) // M_HEAD_DIM
    rep = jnp.asarray((np.arange(LANES)[:, None] == hid[None, :]).astype(np.float32), BF16)
    chunk = (lambda c: nc - 1 - c) if reverse else (lambda c: c)
    add_prev = y_prev is not None
    in_specs = [pl.BlockSpec((M_CHUNK, c_all), lambda b, c: (rb0 + b * nc + chunk(c), 0)),
                pl.BlockSpec((1, M_CHUNK, LANES), lambda b, c: (0, rb0 + b * nc + chunk(c), 0)),
                pl.BlockSpec((1, LANES, M_CHUNK), lambda b, c: (0, 0, rb0 + b * nc + chunk(c))),
                _full((1, LANES)), _full((LANES, LANES)), _full((LANES, inner)),
                pl.BlockSpec((1, groups, M_STATE, gw), lambda b, c: (b, 0, 0, 0))]
    args = [xbc, dt, dtt, alr, alc, rep, h0]
    yspec = pl.BlockSpec((M_CHUNK, inner), lambda b, c: (b * nc + chunk(c), 0))
    if add_prev:
        in_specs.append(yspec)
        args.append(y_prev)
    return pl.pallas_call(
        functools.partial(_ssd_kernel, reverse=reverse, inner=inner, groups=groups, add_prev=add_prev),
        grid=(nb, nc), in_specs=in_specs,
        out_specs=[yspec, pl.BlockSpec((1, groups, M_STATE, gw), lambda b, c: (b, 0, 0, 0))],
        out_shape=[jax.ShapeDtypeStruct((nb * seq_len, inner), F32), jax.ShapeDtypeStruct(h0.shape, F32)],
        scratch_shapes=[pltpu.VMEM((groups, M_STATE, gw), F32)],
        compiler_params=_cparams("parallel", "arbitrary"), name="ssd_pass",
    )(*args)


def _mamba_gate_kernel(y_ref, x_ref, z_ref, dsk_ref, gn_ref, o_ref, *, groups):
    z = z_ref[...]
    y = (y_ref[...] + dsk_ref[...] * x_ref[...]) * (z * jax.nn.sigmoid(z))
    gw = y.shape[1] // groups
    for g in range(groups):
        yg = y[:, g * gw:(g + 1) * gw]
        yg = yg * lax.rsqrt(jnp.mean(yg * yg, axis=-1, keepdims=True) + EPS) * gn_ref[:, g * gw:(g + 1) * gw]
        o_ref[:, g * gw:(g + 1) * gw] = yg.astype(o_ref.dtype)


def _mamba_gate(y, xbc, z, d_skip_cols, gnorm, groups, tm):
    t, inner = y.shape
    row = pl.BlockSpec((tm, inner), lambda i: (i, 0))
    return pl.pallas_call(
        functools.partial(_mamba_gate_kernel, groups=groups), grid=(t // tm,),
        in_specs=[row, row, row, _full((1, inner)), _full((1, inner))],
        out_specs=row, out_shape=jax.ShapeDtypeStruct((t, inner), BF16),
        compiler_params=_cparams("parallel"), name="mamba_gate",
    )(y, xbc, z, d_skip_cols.reshape(1, inner), gnorm.reshape(1, inner))


def _even_mixer(xs, mods, nb, seq, ctx_len, tm, bid_fn, norm1, win, da_lambda, da_subln, rw_mu, rw_w0, rw_w2, rw_a0,
                rw_a2, rw_g2, rw_kk, rw_ka, rw_rk, rw_lnx_w, rw_lnx_b, wout, lam_init):
    d = xs.shape[1]
    t_lat = nb * seq
    da_w = d // 2
    rw_w = d - da_w
    winb = win.astype(BF16)
    ws = [winb[:, :da_w], winb[:, da_w:2 * da_w], winb[:, 2 * da_w:3 * da_w], winb[:, 3 * da_w:]]
    q, k, v, u = _normmod_proj(xs, norm1, mods[0], mods[1], ws, [F32, F32, BF16, F32], bid_fn(tm), tm)

    cos, sin = _rope_tables(seq, tm)
    lat_tiles = t_lat // tm
    tab_block = lambda i: jnp.where(i < lat_tiles, i % (seq // tm), seq // tm)
    qr, kr = _rope(q, k, cos, sin, tab_block, tm)
    lk = ctx_len + seq
    cat = lambda a: jnp.concatenate([a[t_lat:].reshape(nb, ctx_len, da_w), a[:t_lat].reshape(nb, seq, da_w)],
                                    axis=1).reshape(nb * lk, da_w)
    tq = min(256, seq)
    o_lat = _diff_attention(qr, cat(kr), cat(v), da_lambda, da_subln, lam_init, nb, seq, lk, 0, tq)
    tqc = min(256, ctx_len)
    o_ctx = _diff_attention(qr, kr[t_lat:], v[t_lat:], da_lambda, da_subln, lam_init, nb, ctx_len, ctx_len,
                            t_lat, tqc)
    o_att = jnp.concatenate([o_lat, o_ctx], axis=0)

    seq_lens = [seq] * nb + [ctx_len] * nb
    o_rw = _rwkv_mixer(u, seq_lens, nb, seq, ctx_len, tm, rw_mu, rw_w0, rw_w2, rw_a0, rw_a2, rw_g2, rw_kk, rw_ka,
                       rw_rk, rw_lnx_w, rw_lnx_b)

    woutb = wout.astype(BF16)
    return _proj_residual([o_att, o_rw], [woutb[:da_w], woutb[da_w:]], xs, mods[2], bid_fn(tm), tm)


def _odd_mixer_last(xs, mods, nb, seq, ctx_len, tm, bid_fn, norm1, win, conv_w, conv_b, dt_bias, a_log, d_skip, gnorm,
                    wout):
    d = xs.shape[1]
    t_lat = nb * seq
    inner = wout.shape[0]
    heads = a_log.shape[1]
    conv_dim = conv_w.shape[1]
    groups = (conv_dim - inner) // (2 * M_STATE)
    winb = win.astype(BF16)
    pad = jnp.zeros((d, LANES - heads), BF16)
    w_dt = [jnp.concatenate([winb[:, inner + conv_dim + k * heads:inner + conv_dim + (k + 1) * heads], pad], axis=1)
            for k in range(2)]
    ws = [winb[:, :inner], winb[:, inner:inner + conv_dim]] + w_dt
    z, xbc_raw, dtr_f, dtr_b = _normmod_proj(xs, norm1, mods[0], mods[1], ws, [F32] * 4, bid_fn(tm), tm)
    seq_lens = [seq] * nb + [ctx_len] * nb
    dtb = jnp.zeros((2, 1, LANES), F32).at[:, 0, :heads].set(dt_bias)
    xbc, dt = _mamba_conv(xbc_raw, jnp.stack([dtr_f, dtr_b]), seq_lens, tm, conv_w, conv_b, dtb, heads)
    dtt = jnp.swapaxes(dt, 1, 2)
    h0 = jnp.zeros((nb, groups, M_STATE, inner // groups), F32)
    ssd = functools.partial(_ssd_pass, xbc, inner=inner, groups=groups, heads=heads)
    _, hf = ssd(dt[0:1], dtt[0:1], a_log[0], h0, None, nb, ctx_len, t_lat, False)
    _, hb = ssd(dt[1:2], dtt[1:2], a_log[1], h0, None, nb, ctx_len, t_lat, True)
    y, _ = ssd(dt[0:1], dtt[0:1], a_log[0], hf, None, nb, seq, 0, False)
    y, _ = ssd(dt[1:2], dtt[1:2], a_log[1], hb, y, nb, seq, 0, True)
    x_lat = xs[:t_lat]
    gated = _mamba_gate(y, xbc, z, jnp.repeat(d_skip, M_HEAD_DIM), gnorm, groups, tm)
    return _proj_residual([gated], [wout.astype(BF16)], x_lat, mods[2], bid_fn(tm), tm)


def kernel(x, c, ctx, c_ctx, ada_w_0, ada_b_0, norm1_0, norm2_0, win_0, da_lambda_0, da_subln_0, rw_mu_0, rw_w0_0, rw_w2_0, rw_a0_0, rw_a2_0, rw_g2_0, rw_kk_0, rw_ka_0, rw_rk_0, rw_lnx_w_0, rw_lnx_b_0, wout_0, peer_q_0, peer_keys_0, peer_u_0, peer_v_0, ada_w_1, ada_b_1, norm1_1, norm2_1, win_1, conv_w_1, conv_b_1, dt_bias_1, a_log_1, d_skip_1, gnorm_1, wout_1, peer_q_1, peer_keys_1, peer_u_1, peer_v_1, norm_f):
    nb_all, seq, d = x.shape
    ctx_len = ctx.shape[1]
    tm = 256 if (seq % 256 == 0 and ctx_len % 256 == 0) else 128
    assert seq % tm == 0 and ctx_len % tm == 0 and seq % GRID_W == 0 and nb_all % BATCH_STREAMS == 0
    nb = nb_all // BATCH_STREAMS

    def bid_fn(tile):
        per = seq // tile
        return lambda i: jnp.minimum(i // per, nb)

    def expert_rows(u, v):
        bits = lambda a: lax.bitcast_convert_type(a.astype(BF16), jnp.uint16).astype(jnp.uint32)
        return bits(u) | (bits(v) << 16)

    uv0 = expert_rows(peer_u_0, peer_v_0)
    uv1 = expert_rows(peer_u_1, peer_v_1)
    wq0 = peer_q_0.astype(BF16)
    wq1 = peer_q_1.astype(BF16)
    lam_init = 0.8 - 0.6 * math.exp(-0.3 * 0)

    xs, mods0, mods1 = [], [], []
    for s in range(BATCH_STREAMS):
        sl = slice(s * nb, (s + 1) * nb)
        xs.append(jnp.concatenate([x[sl].reshape(nb * seq, d), ctx[sl].reshape(nb * ctx_len, d)], axis=0))
        cvecs = jnp.zeros((16, d), F32).at[:nb].set(c[sl]).at[nb].set(c_ctx)
        mods0.append(_ada_mod(cvecs, ada_w_0, ada_b_0))
        mods1.append(_ada_mod(cvecs, ada_w_1, ada_b_1))

    def mix0(s):
        return _even_mixer(xs[s], mods0[s], nb, seq, ctx_len, tm, bid_fn, norm1_0, win_0, da_lambda_0, da_subln_0,
                           rw_mu_0, rw_w0_0, rw_w2_0, rw_a0_0, rw_a2_0, rw_g2_0, rw_kk_0, rw_ka_0, rw_rk_0,
                           rw_lnx_w_0, rw_lnx_b_0, wout_0, lam_init)

    def mix1(s, xin):
        return _odd_mixer_last(xin, mods1[s], nb, seq, ctx_len, tm, bid_fn, norm1_1, win_1, conv_w_1, conv_b_1,
                               dt_bias_1, a_log_1, d_skip_1, gnorm_1, wout_1)

    def peer0_start(s, xin):
        return _peer_start(xin, norm2_0, mods0[s][3], mods0[s][4], wq0, peer_keys_0, uv0, bid_fn, tm)

    def peer1_start(s, xin):
        return _peer_start(xin, norm2_1, mods1[s][3], mods1[s][4], wq1, peer_keys_1, uv1, bid_fn, tm)

    streams = range(BATCH_STREAMS)
    cur, pending = {}, {}
    for s in streams:
        cur[s] = mix0(s)
        pending[s] = peer0_start(s, cur[s])
    for s in streams:
        cur[s] = _peer_finish(cur[s], pending[s], mods0[s][5], norm2_0, bid_fn, False)
        cur[s] = mix1(s, cur[s])
        pending[s] = peer1_start(s, cur[s])
    outs = [_peer_finish(cur[s], pending[s], mods1[s][5], norm_f, bid_fn, True).reshape(nb, seq, d) for s in streams]
    return jnp.concatenate(outs, axis=0)
```

```python
import functools
import math

import jax
import jax.numpy as jnp
import numpy as np
from jax import lax
from jax.experimental import pallas as pl
from jax.experimental.pallas import tpu as pltpu
from jax.experimental.pallas import tpu_sc as plsc

F32 = jnp.float32
BF16 = jnp.bfloat16
I32 = jnp.int32
HI = lax.Precision.HIGHEST

EPS = 1e-6
N_MOD = 6
GRID_W = 64
LANES = 128
VMEM_LIMIT_BYTES = 48 * 1024 * 1024

DA_HEAD_DIM = 64
DA_V_DIM = 128
ROPE_BASE = 10000.0
ROPE_NFREQ = DA_HEAD_DIM // 4
RW_HEAD_DIM = 64
RW_GN_EPS = 64e-5
RW_CHUNK = 64
RW_PASSES = 1
RW_STATE_PASSES = 3
M_HEAD_DIM = 64
M_STATE = 128
M_CHUNK = 128
PEER_HEADS = 8
PEER_NKEYS = 128
PEER_TOPK = 16
PEER_TOK = 8
PEER_CHUNKS = 1
BATCH_STREAMS = 8
SC_CORES = 2
SC_SUBCORES = 16
SC_WINDOW = 16
SC_NBUF = 4


def _cparams(*sem):
    return pltpu.CompilerParams(dimension_semantics=sem, vmem_limit_bytes=VMEM_LIMIT_BYTES)


def _nt(a, b, precision=None):
    return lax.dot_general(a, b, (((1,), (1,)), ((), ())), preferred_element_type=F32, precision=precision)


def _full(shape):
    nd = len(shape)
    return pl.BlockSpec(shape, lambda *_: (0,) * nd)


def _split_bf16(x):
    hi = x.astype(BF16)
    lo = (x - hi.astype(F32)).astype(BF16)
    return hi, lo


def _ada_kernel(c_ref, w_ref, b_ref, o_ref):
    c = c_ref[...]
    s = c * jax.nn.sigmoid(c)
    o_ref[...] = jnp.dot(s, w_ref[...], preferred_element_type=F32, precision=HI) + b_ref[...]


def _ada_mod(cvecs, w, b):
    r, d = cvecs.shape
    n = w.shape[1]
    tn = 1024
    m = pl.pallas_call(
        _ada_kernel,
        grid=(n // tn,),
        in_specs=[_full((r, d)), pl.BlockSpec((d, tn), lambda j: (0, j)), pl.BlockSpec((1, tn), lambda j: (0, j))],
        out_specs=pl.BlockSpec((r, tn), lambda j: (0, j)),
        out_shape=jax.ShapeDtypeStruct((r, n), F32),
        compiler_params=_cparams("parallel"),
        name="ada_mod",
    )(cvecs, w, b.reshape(1, n))
    return [m[:, k * d:(k + 1) * d].reshape(r, 1, d) for k in range(N_MOD)]


def _normmod_kernel(x_ref, g_ref, sh_ref, sc_ref, *refs, n_w, want_h):
    x = x_ref[...]
    y = x * lax.rsqrt(jnp.mean(x * x, axis=-1, keepdims=True) + EPS) * g_ref[...]
    h = y * (1.0 + sc_ref[0]) + sh_ref[0]
    hb = h.astype(BF16)
    for w_ref, o_ref in zip(refs[:n_w], refs[n_w:2 * n_w]):
        o_ref[...] = jnp.dot(hb, w_ref[...], preferred_element_type=F32).astype(o_ref.dtype)
    if want_h:
        refs[2 * n_w][...] = h


def _normmod_proj(x, g, shift, scale, ws, out_dtypes, bid, tm, want_h=False, row0=0, nrows=None):
    d = x.shape[1]
    t = x.shape[0] if nrows is None else nrows
    blk0 = row0 // tm
    n_w = len(ws)
    in_specs = [pl.BlockSpec((tm, d), lambda i: (blk0 + i, 0)), _full((1, d)),
                pl.BlockSpec((1, 1, d), lambda i: (bid(blk0 + i), 0, 0)),
                pl.BlockSpec((1, 1, d), lambda i: (bid(blk0 + i), 0, 0))]
    in_specs += [_full(w.shape) for w in ws]
    out_specs = [pl.BlockSpec((tm, w.shape[1]), lambda i: (i, 0)) for w in ws]
    out_shape = [jax.ShapeDtypeStruct((t, w.shape[1]), dt) for w, dt in zip(ws, out_dtypes)]
    if want_h:
        out_specs.append(pl.BlockSpec((tm, d), lambda i: (i, 0)))
        out_shape.append(jax.ShapeDtypeStruct((t, d), F32))
    return pl.pallas_call(
        functools.partial(_normmod_kernel, n_w=n_w, want_h=want_h),
        grid=(t // tm,), in_specs=in_specs, out_specs=out_specs, out_shape=out_shape,
        compiler_params=_cparams("parallel"), name="normmod_proj",
    )(x, g.reshape(1, d), shift, scale, *ws)


def _proj_res_kernel(*refs, n_a):
    a_refs = refs[:n_a]
    w_refs = refs[n_a:2 * n_a]
    res_ref, gate_ref, o_ref = refs[2 * n_a:]
    acc = jnp.dot(a_refs[0][...], w_refs[0][...], preferred_element_type=F32)
    for a_ref, w_ref in zip(a_refs[1:], w_refs[1:]):
        acc += jnp.dot(a_ref[...], w_ref[...], preferred_element_type=F32)
    o_ref[...] = res_ref[...] + gate_ref[0] * acc


def _proj_residual(a_list, w_list, res, gate, bid, tm):
    t, n = res.shape
    n_a = len(a_list)
    in_specs = [pl.BlockSpec((tm, a.shape[1]), lambda i: (i, 0)) for a in a_list]
    in_specs += [_full(w.shape) for w in w_list]
    in_specs += [pl.BlockSpec((tm, n), lambda i: (i, 0)), pl.BlockSpec((1, 1, n), lambda i: (bid(i), 0, 0))]
    return pl.pallas_call(
        functools.partial(_proj_res_kernel, n_a=n_a),
        grid=(t // tm,), in_specs=in_specs, out_specs=pl.BlockSpec((tm, n), lambda i: (i, 0)),
        out_shape=jax.ShapeDtypeStruct((t, n), F32),
        compiler_params=_cparams("parallel"), name="proj_residual",
    )(*a_list, *w_list, res, gate)


def _rope_kernel(q_ref, k_ref, c_ref, s_ref, qo_ref, ko_ref):
    c = c_ref[...]
    s = s_ref[...]
    lane = lax.broadcasted_iota(I32, c.shape, 1)
    first = (lane % 32) < 16
    width = q_ref.shape[1]

    def rot(x):
        partner = jnp.where(first, pltpu.roll(x, LANES - 16, 1), pltpu.roll(x, 16, 1))
        return x * c + partner * s

    for g in range(width // LANES):
        sl = slice(g * LANES, (g + 1) * LANES)
        qo_ref[:, sl] = (rot(q_ref[:, sl]) * (DA_HEAD_DIM ** -0.5)).astype(qo_ref.dtype)
        ko_ref[:, sl] = rot(k_ref[:, sl]).astype(ko_ref.dtype)


def _rope_tables(seq_len, tm):
    rows = seq_len // GRID_W
    row = jnp.repeat(jnp.arange(rows, dtype=F32), GRID_W)
    col = (jnp.arange(seq_len) % GRID_W).astype(F32)
    inv = ROPE_BASE ** (-jnp.arange(ROPE_NFREQ, dtype=F32) / ROPE_NFREQ)
    ang_r = row[:, None] * inv
    ang_c = col[:, None] * inv
    cos64 = jnp.concatenate([jnp.cos(ang_r), jnp.cos(ang_r), jnp.cos(ang_c), jnp.cos(ang_c)], axis=1)
    sin64 = jnp.concatenate([-jnp.sin(ang_r), jnp.sin(ang_r), -jnp.sin(ang_c), jnp.sin(ang_c)], axis=1)
    cos = jnp.concatenate([jnp.tile(cos64, (1, 2)), jnp.ones((tm, LANES), F32)], axis=0)
    sin = jnp.concatenate([jnp.tile(sin64, (1, 2)), jnp.zeros((tm, LANES), F32)], axis=0)
    return cos, sin


def _rope(q, k, cos, sin, tab_block, tm):
    t, w = q.shape
    row = pl.BlockSpec((tm, w), lambda i: (i, 0))
    tab = pl.BlockSpec((tm, LANES), lambda i: (tab_block(i), 0))
    return pl.pallas_call(
        _rope_kernel, grid=(t // tm,), in_specs=[row, row, tab, tab], out_specs=[row, row],
        out_shape=[jax.ShapeDtypeStruct((t, w), BF16)] * 2,
        compiler_params=_cparams("parallel"), name="rope",
    )(q, k, cos, sin)


def _attn_kernel(lam_ref, sub_ref, q_ref, k_ref, v_ref, o_ref, *, lam_init):
    lp = lam_ref[...]
    lam = (jnp.exp(jnp.sum(lp[0:1] * lp[1:2], keepdims=True))
           - jnp.exp(jnp.sum(lp[2:3] * lp[3:4], keepdims=True)) + lam_init)
    q = q_ref[...]
    k = k_ref[...]
    v = v_ref[...]
    lane = lax.broadcasted_iota(I32, q.shape, 1)
    outs = []
    for m in range(2):
        sel = (lane < DA_HEAD_DIM) if m == 0 else (lane >= DA_HEAD_DIM)
        s = _nt(jnp.where(sel, q, jnp.zeros_like(q)), k)
        p = jnp.exp(s - jnp.max(s, axis=-1, keepdims=True))
        denom = jnp.sum(p, axis=-1, keepdims=True)
        outs.append(jnp.dot(p.astype(BF16), v, preferred_element_type=F32) / denom)
    o = outs[0] - lam * outs[1]
    o = o * lax.rsqrt(jnp.mean(o * o, axis=-1, keepdims=True) + EPS) * sub_ref[...] * (1.0 - lam_init)
    o_ref[...] = o.astype(o_ref.dtype)


def _diff_attention(q, k, v, lamp, subln, lam_init, nb, lq, lk, q_row0, tq):
    w = q.shape[1]
    heads = w // DA_V_DIM
    nq = lq // tq
    qb0 = q_row0 // tq
    return pl.pallas_call(
        functools.partial(_attn_kernel, lam_init=lam_init),
        grid=(nb, heads, nq),
        in_specs=[_full(lamp.shape), _full((1, DA_V_DIM)),
                  pl.BlockSpec((tq, DA_V_DIM), lambda b, h, i: (qb0 + b * nq + i, h)),
                  pl.BlockSpec((lk, DA_V_DIM), lambda b, h, i: (b, h)),
                  pl.BlockSpec((lk, DA_V_DIM), lambda b, h, i: (b, h))],
        out_specs=pl.BlockSpec((tq, DA_V_DIM), lambda b, h, i: (b * nq + i, h)),
        out_shape=jax.ShapeDtypeStruct((nb * lq, w), BF16),
        compiler_params=_cparams("parallel", "parallel", "arbitrary"), name="diff_attention",
    )(lamp, subln.reshape(1, DA_V_DIM), q, k, v)


def _softplus(z):
    return jnp.maximum(z, 0.0) + jnp.log(1.0 + jnp.exp(-jnp.abs(z)))


def _rw_prep_kernel(u_ref, prev_ref, next_ref, mu_ref, w0_ref, w2_ref, a0_ref, a2_ref, g2_ref, kk_ref, ka_ref,
                    rk_ref, ones_ref, r_ref, v_ref, nkk_ref, g_ref, bonus_ref, lw_ref, kd_ref, bd_ref):
    u = u_ref[...]
    tm = u.shape[0]
    width = r_ref.shape[1]
    row = lax.broadcasted_iota(I32, u.shape, 0)
    up = jnp.where(row == 0, prev_ref[0], pltpu.roll(u, 1, 0))
    dn = jnp.where(row == tm - 1, next_ref[0], pltpu.roll(u, tm - 1, 0))
    u = u + mu_ref[...] * (0.5 * (up + dn) - u)
    r = u[:, :width]
    k = u[:, width:2 * width]
    v = u[:, 2 * width:3 * width]
    o = 3 * width
    w_in = u[:, o:o + LANES]
    a_in = u[:, o + LANES:o + 2 * LANES]
    g_in = u[:, o + 2 * LANES:o + 3 * LANES]
    ones = ones_ref[...]
    hsum = lambda t: jnp.dot(t, ones, preferred_element_type=F32, precision=HI)
    g = jnp.dot(jax.nn.sigmoid(g_in), g2_ref[...], preferred_element_type=F32, precision=HI)
    kk = k * kk_ref[...]
    kk = kk / jnp.maximum(jnp.sqrt(hsum(kk * kk)), 1e-12)
    w_log = -_softplus(-(w0_ref[...] + jnp.dot(jnp.tanh(w_in), w2_ref[...], preferred_element_type=F32,
                                               precision=HI))) - 0.5
    logw = -jnp.exp(w_log)
    a = jax.nn.sigmoid(a0_ref[...] + jnp.dot(a_in, a2_ref[...], preferred_element_type=F32, precision=HI))
    ksum = jnp.zeros_like(k)
    for d in range(2):
        a_d = a[:, d * width:(d + 1) * width]
        k_d = k * (1.0 + (a_d - 1.0) * ka_ref[...])
        ksum = ksum + k_d
        lw_ref[d] = logw[:, d * width:(d + 1) * width]
        kd_ref[d] = k_d
        bd_ref[d] = kk * a_d
    r_ref[...] = r
    v_ref[...] = v
    nkk_ref[...] = -kk
    g_ref[...] = g
    bonus_ref[...] = hsum(r * ksum * rk_ref[...]) * v


def _halo_rows(x, tm, seq_lens):
    t = x.shape[0]
    nt = t // tm
    starts = np.cumsum([0] + [n for n in seq_lens])[:-1]
    ends = np.cumsum(seq_lens)
    tile_start = np.arange(nt) * tm
    has_prev = ~np.isin(tile_start, starts)
    has_next = ~np.isin(tile_start + tm, ends)
    last = x[tm - 1::tm]
    first = x[0::tm]
    zero = jnp.zeros_like(first[:1])
    prev = jnp.concatenate([zero, last[:-1]], axis=0) * jnp.asarray(has_prev, x.dtype)[:, None]
    nxt = jnp.concatenate([first[1:], zero], axis=0) * jnp.asarray(has_next, x.dtype)[:, None]
    return prev[:, None, :], nxt[:, None, :]


def _block_diag2(m):
    z = jnp.zeros_like(m[0])
    return jnp.concatenate([jnp.concatenate([m[0], z], axis=1), jnp.concatenate([z, m[1]], axis=1)], axis=0)


def _head_ones(width, hd):
    idx = np.arange(width) // hd
    return jnp.asarray((idx[:, None] == idx[None, :]).astype(np.float32))


def _rwkv_prepare(u, seq_lens, tm, mu, w0, w2, a0, a2, g2, k_k, k_a, r_k):
    t, cols = u.shape
    width = k_k.shape[0]
    prev, nxt = _halo_rows(u, tm, seq_lens)
    row = lambda c: pl.BlockSpec((tm, c), lambda i: (i, 0))
    halo = pl.BlockSpec((1, 1, cols), lambda i: (i, 0, 0))
    dir_out = pl.BlockSpec((2, tm, width), lambda i: (0, i, 0))
    consts = [mu.reshape(1, cols), w0.reshape(1, 2 * width), _block_diag2(w2), a0.reshape(1, 2 * width),
              _block_diag2(a2), g2, k_k.reshape(1, width), k_a.reshape(1, width), r_k.reshape(1, width),
              _head_ones(width, RW_HEAD_DIM)]
    f = jax.ShapeDtypeStruct((t, width), F32)
    f2 = jax.ShapeDtypeStruct((2, t, width), F32)
    return pl.pallas_call(
        _rw_prep_kernel, grid=(t // tm,),
        in_specs=[row(cols), halo, halo] + [_full(c.shape) for c in consts],
        out_specs=[row(width)] * 5 + [dir_out] * 3,
        out_shape=[f] * 5 + [f2] * 3,
        compiler_params=_cparams("parallel"), name="rwkv_prepare",
    )(u, prev, nxt, *consts)


def _mm(x, y, passes):
    if passes == 6:
        return jnp.dot(x, y, preferred_element_type=F32, precision=HI)
    dot = lambda p, q: jnp.dot(p, q, preferred_element_type=F32)
    if passes == 1:
        return dot(x.astype(BF16), y.astype(BF16))
    xh, xl = _split_bf16(x)
    yh, yl = _split_bf16(y)
    return dot(xh, yh) + (dot(xh, yl) + dot(xl, yh))


def _rw_scan_kernel(r_ref, v_ref, nkk_ref, lw_ref, kd_ref, bd_ref, h0_ref, y_ref, hf_ref, h_scr):
    d = pl.program_id(1)
    c = pl.program_id(2)
    nc = pl.num_programs(2)
    cs = RW_CHUNK
    pairs = h_scr.shape[0]
    sgn = 1 - 2 * d

    @pl.when(c == 0)
    def _():
        h_scr[...] = h0_ref[0, 0]

    ri = lax.broadcasted_iota(I32, (cs, cs), 0)
    ci = lax.broadcasted_iota(I32, (cs, cs), 1)
    before_eq = jnp.where((ri - ci) * sgn >= 0, 1.0, 0.0).astype(BF16)
    lane = lax.broadcasted_iota(I32, (cs, LANES), 1)
    lo_half = lane < RW_HEAD_DIM
    n2 = 2 * cs
    rt = lax.broadcasted_iota(I32, (n2, n2), 0)
    ct = lax.broadcasted_iota(I32, (n2, n2), 1)
    dtok = ((rt & (cs - 1)) - (ct & (cs - 1))) * sgn
    strict = dtok > 0
    incl = dtok >= 0
    eye = rt == ct

    def stack2(x):
        return jnp.concatenate([jnp.where(lo_half, x, 0.0), jnp.where(lo_half, 0.0, x)], axis=0)

    mm = functools.partial(_mm, passes=RW_PASSES)
    mm_state = functools.partial(_mm, passes=RW_STATE_PASSES)

    h_in = [h_scr[p] for p in range(pairs)]
    ys, h_out = [], []
    for p in range(pairs):
        sl = slice(p * LANES, (p + 1) * LANES)
        logw = lw_ref[0, :, sl]
        r = r_ref[:, sl]
        v = v_ref[:, sl]
        a = nkk_ref[:, sl]
        k = kd_ref[0, :, sl]
        b = bd_ref[0, :, sl]

        lw_hi, lw_lo = _split_bf16(logw)
        cum = (jnp.dot(before_eq, lw_hi, preferred_element_type=F32)
               + jnp.dot(before_eq, lw_lo, preferred_element_type=F32))
        mid = cum[cs // 2:cs // 2 + 1]
        tot = jnp.sum(logw, axis=0, keepdims=True)
        e_in = jnp.exp(mid - cum)
        e_end = jnp.exp(tot - cum)

        a2 = stack2(a * jnp.exp(cum - logw - mid))
        r2 = stack2(r * jnp.exp(cum - mid))
        a2_abs = stack2(a * jnp.exp(cum - logw))
        r2_abs = stack2(r * jnp.exp(cum))
        b2 = stack2(b * e_in)
        k2 = stack2(k * e_in)
        v2 = stack2(v)
        bh2 = stack2(b * e_end)
        kh2 = stack2(k * e_end)

        b2t = b2.T
        k2t = k2.T
        nmat = jnp.where(strict, mm(a2, b2t), 0.0)
        mmat = jnp.where(strict, mm(a2, k2t), 0.0)
        qb = jnp.where(incl, mm(r2, b2t), 0.0)
        qk = jnp.where(incl, mm(r2, k2t), 0.0)

        tinv = jnp.where(eye, 1.0, 0.0) + nmat
        pw = nmat
        for _ in range(int(math.log2(cs)) - 1):
            pw = mm(pw, pw)
            tinv = tinv + mm(tinv, pw)

        w2 = mm(tinv, mm(mmat, v2))
        a2p = mm(tinv, a2_abs)
        y_intra = mm(qk, v2) + mm(qb, w2)
        r2p = r2_abs + mm(qb, a2p)
        bh2t = bh2.T
        gmat = jnp.where(eye, jnp.exp(tot), 0.0) + mm(bh2t, a2p)
        dmat = mm(bh2t, w2) + mm(kh2.T, v2)

        h = h_in[p]
        y2 = y_intra + mm_state(r2p, h)
        ys.append(y2[:cs] + y2[cs:])
        h_out.append(mm_state(gmat, h) + dmat)

    y_ref[0] = jnp.concatenate(ys, axis=1)
    for p in range(pairs):
        h_scr[p] = h_out[p]

    @pl.when(c == nc - 1)
    def _():
        hf_ref[0, 0] = h_scr[...]


def _rwkv_scan(r, v, nkk, lw, kd, bd, h0, nb, seq_len, row0):
    w = r.shape[1]
    pairs = w // LANES
    nc = seq_len // RW_CHUNK
    rb0 = row0 // RW_CHUNK

    def chunk(c, d):
        return c + d * (nc - 1 - 2 * c)

    shared = pl.BlockSpec((RW_CHUNK, w), lambda b, d, c: (rb0 + b * nc + chunk(c, d), 0))
    perdir = pl.BlockSpec((1, RW_CHUNK, w), lambda b, d, c: (d, rb0 + b * nc + chunk(c, d), 0))
    state = pl.BlockSpec((1, 1, pairs, LANES, LANES), lambda b, d, c: (d, b, 0, 0, 0))
    return pl.pallas_call(
        _rw_scan_kernel, grid=(nb, 2, nc),
        in_specs=[shared, shared, shared, perdir, perdir, perdir, state],
        out_specs=[pl.BlockSpec((1, RW_CHUNK, w), lambda b, d, c: (d, b * nc + chunk(c, d), 0)), state],
        out_shape=[jax.ShapeDtypeStruct((2, nb * seq_len, w), F32), jax.ShapeDtypeStruct(h0.shape, F32)],
        scratch_shapes=[pltpu.VMEM((pairs, LANES, LANES), F32)],
        compiler_params=_cparams("parallel", "parallel", "arbitrary"), name="rwkv_scan",
    )(r, v, nkk, lw, kd, bd, h0)


def _rw_post_kernel(y_ref, g_ref, bonus_ref, lnw_ref, lnb_ref, ones_ref, o_ref):
    y = y_ref[0] + y_ref[1]
    ones = ones_ref[...]
    hmean = lambda t: jnp.dot(t, ones, preferred_element_type=F32, precision=HI) * (1.0 / RW_HEAD_DIM)
    yc = y - hmean(y)
    var = hmean(yc * yc)
    yn = yc * lax.rsqrt(var + RW_GN_EPS) * lnw_ref[...] + lnb_ref[...]
    o_ref[...] = ((yn + bonus_ref[...]) * g_ref[...]).astype(o_ref.dtype)


def _rwkv_post(y, g, bonus, lnw, lnb, tm):
    _, t, w = y.shape
    row = pl.BlockSpec((tm, w), lambda i: (i, 0))
    return pl.pallas_call(
        _rw_post_kernel, grid=(t // tm,),
        in_specs=[pl.BlockSpec((2, tm, w), lambda i: (0, i, 0)), row, row, _full((1, w)), _full((1, w)),
                  _full((w, w))],
        out_specs=row, out_shape=jax.ShapeDtypeStruct((t, w), BF16),
        compiler_params=_cparams("parallel"), name="rwkv_post",
    )(y, g, bonus, lnw.reshape(1, w), lnb.reshape(1, w), _head_ones(w, RW_HEAD_DIM))


def _rwkv_mixer(u, seq_lens, nb, seq, ctx_len, tm, mu, w0, w2, a0, a2, g2, k_k, k_a, r_k, lnx_w, lnx_b):
    t_lat = nb * seq
    r, vv, nkk, g, bonus, lw, kd, bd = _rwkv_prepare(u, seq_lens, tm, mu, w0, w2, a0, a2, g2, k_k, k_a, r_k)
    zero = jnp.zeros((2, nb, r.shape[1] // LANES, LANES, LANES), F32)
    y_ctx, h_ctx = _rwkv_scan(r, vv, nkk, lw, kd, bd, zero, nb, ctx_len, t_lat)
    y_lat, _ = _rwkv_scan(r, vv, nkk, lw, kd, bd, h_ctx, nb, seq, 0)
    y = jnp.concatenate([y_lat, y_ctx], axis=1)
    return _rwkv_post(y, g, bonus, lnx_w, lnx_b, tm)


def _extract_topk(s, order, payload, count):
    big = float(2 ** 24)
    vals, pays = [], []
    for _ in range(count):
        m = jnp.max(s, axis=0, keepdims=True)
        first = jnp.min(jnp.where(s == m, order, big), axis=0, keepdims=True)
        hit = order == first
        vals.append(m)
        pays.append(first if payload is None else jnp.sum(jnp.where(hit, payload, 0.0), axis=0, keepdims=True))
        s = jnp.where(hit, -jnp.inf, s)
    return jnp.concatenate(vals, axis=0), jnp.concatenate(pays, axis=0)


def _pruned_candidates(v1, i1, v2, i2):
    k = PEER_TOPK
    tt = v1.shape[1]
    row8 = lax.broadcasted_iota(I32, (8, tt), 0).astype(F32)
    row16 = lax.broadcasted_iota(I32, (k, tt), 0).astype(F32)
    nk = float(PEER_NKEYS)
    sums, flats, eids = [], [], []

    def add(valid, s, flat, eid):
        unused = float(k * k + 16 * len(sums))
        sums.append(s if valid is None else jnp.where(valid, s, -jnp.inf))
        flats.append(flat if valid is None else jnp.where(valid, flat, flat + unused))
        eids.append(eid)

    def vary_j(i, rows, nvalid):
        r = row16 if rows == k else row8
        add(None if nvalid == rows else r < nvalid, v1[i:i + 1] + v2[:rows], r + float(i * k),
            i1[i:i + 1] * nk + i2[:rows])

    def vary_i(j, i0, lo, hi):
        r = row8 + float(i0)
        add(None if (lo == i0 and hi == i0 + 8) else (r >= lo) & (r < hi), v1[i0:i0 + 8] + v2[j:j + 1],
            r * float(k) + float(j), i1[i0:i0 + 8] * nk + i2[j:j + 1])

    vary_j(0, k, k)
    vary_j(1, 8, 8)
    vary_j(2, 8, 5)
    vary_j(3, 8, 4)
    vary_i(0, 8, 8, 16)
    vary_i(0, 0, 4, 8)
    vary_i(1, 0, 4, 8)
    vary_i(2, 0, 4, 5)
    return jnp.concatenate(sums, axis=0), jnp.concatenate(flats, axis=0), jnp.concatenate(eids, axis=0)


def _peer_topk_kernel(q_ref, keys_ref, idx_ref, gate_ref):
    tt = q_ref.shape[0]
    kpos = lax.broadcasted_iota(I32, (PEER_NKEYS, tt), 0).astype(F32)

    def head(h, carry):
        vs, ids = [], []
        for p in range(2):
            col = pl.multiple_of((2 * h + p) * LANES, LANES)
            s = _nt(keys_ref[h, p], q_ref[:, pl.ds(col, LANES)], HI)
            v_p, i_p = _extract_topk(s, kpos, None, PEER_TOPK)
            vs.append(v_p)
            ids.append(i_p)
        cand, flat, eid = _pruned_candidates(vs[0], ids[0], vs[1], ids[1])
        top_s, top_i = _extract_topk(cand, flat, eid, PEER_TOPK)
        e = jnp.exp(top_s - top_s[0:1])
        rows = pl.ds(pl.multiple_of(h * PEER_TOPK, PEER_TOPK), PEER_TOPK)
        gate_ref[rows, :] = e / jnp.sum(e, axis=0, keepdims=True)
        idx_ref[rows, :] = top_i.astype(I32)
        return carry

    lax.fori_loop(0, PEER_HEADS, head, 0)


def _peer_topk(q, keys, tt):
    t = q.shape[0]
    ne = PEER_HEADS * PEER_TOPK
    out = pl.BlockSpec((ne, tt), lambda i: (0, i))
    return pl.pallas_call(
        _peer_topk_kernel, grid=(t // tt,),
        in_specs=[pl.BlockSpec((tt, q.shape[1]), lambda i: (i, 0)), _full(keys.shape)],
        out_specs=[out, out],
        out_shape=[jax.ShapeDtypeStruct((ne, t), I32), jax.ShapeDtypeStruct((ne, t), F32)],
        compiler_params=_cparams("parallel"), name="peer_topk",
    )(q, keys)


def _sc_gather(table, idx):
    n = idx.shape[0]
    r = table.shape[1]
    workers = SC_CORES * SC_SUBCORES
    per_worker = n // workers
    nwin = per_worker // SC_WINDOW
    assert n == workers * nwin * SC_WINDOW and nwin % SC_NBUF == 0
    mesh = plsc.VectorSubcoreMesh(core_axis_name="c", subcore_axis_name="s")

    def body(table_hbm, idx_hbm, out_hbm, idx_v, *rest):
        bufs = rest[:SC_NBUF]
        gsem = rest[SC_NBUF:2 * SC_NBUF]
        osem = rest[2 * SC_NBUF:]
        base = (lax.axis_index("s") * SC_CORES + lax.axis_index("c")) * per_worker
        pltpu.sync_copy(idx_hbm.at[pl.ds(base, per_worker)], idx_v)

        def gather(w, b):
            return pltpu.make_async_copy(table_hbm.at[idx_v.at[pl.ds(w * SC_WINDOW, SC_WINDOW)]], bufs[b], gsem[b])

        def put(w, b):
            return pltpu.make_async_copy(bufs[b], out_hbm.at[pl.ds(base + w * SC_WINDOW, SC_WINDOW)], osem[b])

        for b in range(SC_NBUF):
            gather(b, b).start()

        @pl.loop(0, nwin, step=SC_NBUF)
        def _(w0):
            for b in range(SC_NBUF):
                w = w0 + b
                gather(w, b).wait()
                put(w, b).start()
                put(w, b).wait()

                @pl.when(w + SC_NBUF < nwin)
                def _():
                    gather(w + SC_NBUF, b).start()

    return pl.kernel(
        body, mesh=mesh, out_type=jax.ShapeDtypeStruct((n, r), table.dtype),
        scratch_types=[pltpu.VMEM((per_worker,), I32)] + [pltpu.VMEM((SC_WINDOW, r), table.dtype)] * SC_NBUF
        + [pltpu.SemaphoreType.DMA] * (2 * SC_NBUF),
    )(table, idx)


def _peer_apply_kernel(rows_ref, h_ref, gates_ref, x_ref, mod_ref, nf_ref, *rest, final_norm):
    o_ref = rest[-1]
    tt, ne = gates_ref.shape
    eye = (lax.broadcasted_iota(I32, (ne, ne), 0) == lax.broadcasted_iota(I32, (ne, ne), 1))
    outs = []
    for t in range(tt):
        packed = rows_ref[t * ne:(t + 1) * ne, :]
        u = pltpu.bitcast(packed << 16, F32)
        v = pltpu.bitcast(packed & jnp.uint32(0xFFFF0000), F32)
        pre = jnp.sum(u * h_ref[t:t + 1, :], axis=1, keepdims=True)
        pre = jnp.sum(jnp.where(eye, pre, 0.0), axis=0, keepdims=True)
        act = 0.5 * pre * (1.0 + lax.erf(pre * (2.0 ** -0.5)))
        w = jnp.sum(jnp.where(eye, gates_ref[t:t + 1, :] * act, 0.0), axis=1, keepdims=True)
        outs.append(jnp.sum(v * w, axis=0, keepdims=True))
    y = x_ref[...] + mod_ref[0] * jnp.concatenate(outs, axis=0)
    if final_norm:
        y = y * lax.rsqrt(jnp.mean(y * y, axis=-1, keepdims=True) + EPS) * nf_ref[...]
    o_ref[...] = y


def _peer_apply(rows, h, gates, x, acc, gate_mod, norm_f, token0, bid, final_norm):
    t, d = x.shape
    n, ne = gates.shape
    tt = PEER_TOK
    blk0 = token0 // tt
    local = lambda c: pl.BlockSpec((tt, c), lambda i: (i, 0))
    glob = pl.BlockSpec((tt, d), lambda i: (blk0 + i, 0))
    in_specs = [pl.BlockSpec((tt * ne, d), lambda i: (i, 0)), local(d), local(ne), glob,
                pl.BlockSpec((1, 1, d), lambda i: (bid(blk0 + i), 0, 0)), _full((1, d))]
    args = [rows, h, gates, x, gate_mod, norm_f.reshape(1, d)]
    aliases = {}
    if acc is not None:
        in_specs.append(pl.BlockSpec(memory_space=pl.ANY))
        args.append(acc)
        aliases = {len(args) - 1: 0}
    return pl.pallas_call(
        functools.partial(_peer_apply_kernel, final_norm=final_norm),
        grid=(n // tt,), in_specs=in_specs, out_specs=glob, out_shape=jax.ShapeDtypeStruct((t, d), F32),
        input_output_aliases=aliases,
        compiler_params=_cparams("parallel"), name="peer_apply",
    )(*args)


def _peer_start(x, norm2, shift, scale, wq, keys, uv, bid_fn, tm):
    per = x.shape[0] // PEER_CHUNKS
    assert x.shape[0] == per * PEER_CHUNKS and per % tm == 0
    chunks = []
    for k in range(PEER_CHUNKS):
        q, h = _normmod_proj(x, norm2, shift, scale, [wq], [F32], bid_fn(tm), tm, want_h=True, row0=k * per,
                             nrows=per)
        idx_t, gates_t = _peer_topk(q, keys, LANES)
        rows = _sc_gather(uv, idx_t.T.reshape(per * idx_t.shape[0]))
        chunks.append((rows, h, gates_t.T))
    return chunks


def _peer_finish(x, chunks, gate_mod, norm_f, bid_fn, final_norm):
    per = x.shape[0] // PEER_CHUNKS
    acc = None
    for k, (rows, h, gates) in enumerate(chunks):
        acc = _peer_apply(rows, h, gates, x, acc, gate_mod, norm_f, k * per, bid_fn(PEER_TOK), final_norm)
    return acc


def _conv_kernel(x_ref, prev_ref, next_ref, w_ref, b_ref, dtr_ref, dtb_ref, o_ref, dt_ref, *, heads):
    x = x_ref[...]
    tm = x.shape[0]
    row = lax.broadcasted_iota(I32, x.shape, 0)
    up = jnp.where(row == 0, prev_ref[0], pltpu.roll(x, 1, 0))
    dn = jnp.where(row == tm - 1, next_ref[0], pltpu.roll(x, tm - 1, 0))
    y = up * w_ref[0:1] + x * w_ref[1:2] + dn * w_ref[2:3] + b_ref[...]
    o_ref[...] = y * jax.nn.sigmoid(y)
    lane = lax.broadcasted_iota(I32, (tm, LANES), 1)
    for d in range(2):
        dt_ref[d] = jnp.where(lane < heads, _softplus(dtr_ref[d] + dtb_ref[d]), 0.0)


def _mamba_conv(xbc, dt_raw, seq_lens, tm, conv_w, conv_b, dt_bias_pad, heads):
    t, c = xbc.shape
    prev, nxt = _halo_rows(xbc, tm, seq_lens)
    row = pl.BlockSpec((tm, c), lambda i: (i, 0))
    halo = pl.BlockSpec((1, 1, c), lambda i: (i, 0, 0))
    dts = pl.BlockSpec((2, tm, LANES), lambda i: (0, i, 0))
    return pl.pallas_call(
        functools.partial(_conv_kernel, heads=heads), grid=(t // tm,),
        in_specs=[row, halo, halo, _full(conv_w.shape), _full((1, c)), dts, _full((2, 1, LANES))],
        out_specs=[row, dts],
        out_shape=[jax.ShapeDtypeStruct((t, c), F32), jax.ShapeDtypeStruct((2, t, LANES), F32)],
        compiler_params=_cparams("parallel"), name="mamba_conv",
    )(xbc, prev, nxt, conv_w, conv_b.reshape(1, c), dt_raw, dt_bias_pad)


def _ssd_kernel(*refs, reverse, inner, groups, add_prev):
    if add_prev:
        xbc_ref, dt_ref, dtt_ref, alr_ref, alc_ref, rep_ref, h0_ref, yin_ref, y_ref, hf_ref, h_scr = refs
    else:
        xbc_ref, dt_ref, dtt_ref, alr_ref, alc_ref, rep_ref, h0_ref, y_ref, hf_ref, h_scr = refs
        yin_ref = None
    c = pl.program_id(1)
    nc = pl.num_programs(1)
    cs = M_CHUNK
    gw = inner // groups
    hpg = gw // M_HEAD_DIM

    @pl.when(c == 0)
    def _():
        h_scr[...] = h0_ref[0]

    dt = dt_ref[0]
    dtt = dtt_ref[0]
    a = dt * (-jnp.exp(alr_ref[...]))
    at = dtt * (-jnp.exp(alc_ref[...]))
    ri = lax.broadcasted_iota(I32, (cs, cs), 0)
    ci = lax.broadcasted_iota(I32, (cs, cs), 1)
    incl = (ri <= ci) if reverse else (ri >= ci)
    tri = jnp.where(incl, 1.0, 0.0)
    cum = jnp.dot(tri, a, preferred_element_type=F32, precision=HI)
    cumt = _nt(at, tri, HI)
    tot = jnp.sum(a, axis=0, keepdims=True)

    rep = rep_ref[...]

    def spread(t):
        hi, lo = _split_bf16(t)
        return jnp.dot(hi, rep, preferred_element_type=F32) + jnp.dot(lo, rep, preferred_element_type=F32)

    e_cum = spread(jnp.exp(cum))
    e_end = spread(jnp.exp(tot - cum) * dt)
    e_tot = spread(jnp.broadcast_to(jnp.exp(tot), (8, LANES)))[0:1]

    lane = lax.broadcasted_iota(I32, (cs, LANES), 1)
    lo_half = lane < M_HEAD_DIM
    ys = []
    for g in range(groups):
        bg32 = xbc_ref[:, inner + g * M_STATE:inner + (g + 1) * M_STATE]
        bg = bg32.astype(BF16)
        cg = xbc_ref[:, inner + groups * M_STATE + g * M_STATE:inner + groups * M_STATE + (g + 1) * M_STATE]
        cg = cg.astype(BF16)
        cb = _nt(cg, bg)
        hprev = h_scr[g]
        xg = xbc_ref[:, g * gw:(g + 1) * gw]
        y_off = jnp.dot(cg, hprev.astype(BF16), preferred_element_type=F32) * e_cum[:, g * gw:(g + 1) * gw]
        xd = (xg * e_end[:, g * gw:(g + 1) * gw]).astype(BF16)
        h_scr[g] = e_tot[:, g * gw:(g + 1) * gw] * hprev + jnp.dot(bg32.T.astype(BF16), xd, preferred_element_type=F32)
        for j in range(hpg // 2):
            xpair = xg[:, j * LANES:(j + 1) * LANES].astype(BF16)
            halves = []
            for hh in range(2):
                h = g * hpg + 2 * j + hh
                seg = jnp.minimum(cum[:, h:h + 1] - cumt[h:h + 1, :], 0.0)
                m = jnp.where(incl, cb * jnp.exp(seg), 0.0) * dtt[h:h + 1, :]
                halves.append(jnp.dot(m.astype(BF16), xpair, preferred_element_type=F32))
            ys.append(jnp.where(lo_half, halves[0], halves[1]) + y_off[:, j * LANES:(j + 1) * LANES])
    y = jnp.concatenate(ys, axis=1)
    if add_prev:
        y = y + yin_ref[...]
    y_ref[...] = y

    @pl.when(c == nc - 1)
    def _():
        hf_ref[0] = h_scr[...]


def _ssd_pass(xbc, dt, dtt, a_log, h0, y_prev, nb, seq_len, row0, reverse, inner, groups, heads):
    nc = seq_len // M_CHUNK
    rb0 = row0 // M_CHUNK
    c_all = xbc.shape[1]
    gw = inner // groups
    alr = jnp.zeros((1, LANES), F32).at[0, :heads].set(a_log)
    alc = jnp.broadcast_to(jnp.zeros((LANES,), F32).at[:heads].set(a_log)[:, None], (LANES, LANES))
    hid = np.arange(inner) // M_HEAD_DIM
    rep = jnp.asarray((np.arange(LANES)[:, None] == hid[None, :]).astype(np.float32), BF16)
    chunk = (lambda c: nc - 1 - c) if reverse else (lambda c: c)
    add_prev = y_prev is not None
    in_specs = [pl.BlockSpec((M_CHUNK, c_all), lambda b, c: (rb0 + b * nc + chunk(c), 0)),
                pl.BlockSpec((1, M_CHUNK, LANES), lambda b, c: (0, rb0 + b * nc + chunk(c), 0)),
                pl.BlockSpec((1, LANES, M_CHUNK), lambda b, c: (0, 0, rb0 + b * nc + chunk(c))),
                _full((1, LANES)), _full((LANES, LANES)), _full((LANES, inner)),
                pl.BlockSpec((1, groups, M_STATE, gw), lambda b, c: (b, 0, 0, 0))]
    args = [xbc, dt, dtt, alr, alc, rep, h0]
    yspec = pl.BlockSpec((M_CHUNK, inner), lambda b, c: (b * nc + chunk(c), 0))
    if add_prev:
        in_specs.append(yspec)
        args.append(y_prev)
    return pl.pallas_call(
        functools.partial(_ssd_kernel, reverse=reverse, inner=inner, groups=groups, add_prev=add_prev),
        grid=(nb, nc), in_specs=in_specs,
        out_specs=[yspec, pl.BlockSpec((1, groups, M_STATE, gw), lambda b, c: (b, 0, 0, 0))],
        out_shape=[jax.ShapeDtypeStruct((nb * seq_len, inner), F32), jax.ShapeDtypeStruct(h0.shape, F32)],
        scratch_shapes=[pltpu.VMEM((groups, M_STATE, gw), F32)],
        compiler_params=_cparams("parallel", "arbitrary"), name="ssd_pass",
    )(*args)


def _mamba_gate_kernel(y_ref, x_ref, z_ref, dsk_ref, gn_ref, o_ref, *, groups):
    z = z_ref[...]
    y = (y_ref[...] + dsk_ref[...] * x_ref[...]) * (z * jax.nn.sigmoid(z))
    gw = y.shape[1] // groups
    for g in range(groups):
        yg = y[:, g * gw:(g + 1) * gw]
        yg = yg * lax.rsqrt(jnp.mean(yg * yg, axis=-1, keepdims=True) + EPS) * gn_ref[:, g * gw:(g + 1) * gw]
        o_ref[:, g * gw:(g + 1) * gw] = yg.astype(o_ref.dtype)


def _mamba_gate(y, xbc, z, d_skip_cols, gnorm, groups, tm):
    t, inner = y.shape
    row = pl.BlockSpec((tm, inner), lambda i: (i, 0))
    return pl.pallas_call(
        functools.partial(_mamba_gate_kernel, groups=groups), grid=(t // tm,),
        in_specs=[row, row, row, _full((1, inner)), _full((1, inner))],
        out_specs=row, out_shape=jax.ShapeDtypeStruct((t, inner), BF16),
        compiler_params=_cparams("parallel"), name="mamba_gate",
    )(y, xbc, z, d_skip_cols.reshape(1, inner), gnorm.reshape(1, inner))


def _even_mixer(xs, mods, nb, seq, ctx_len, tm, bid_fn, norm1, win, da_lambda, da_subln, rw_mu, rw_w0, rw_w2, rw_a0,
                rw_a2, rw_g2, rw_kk, rw_ka, rw_rk, rw_lnx_w, rw_lnx_b, wout, lam_init):
    d = xs.shape[1]
    t_lat = nb * seq
    da_w = d // 2
    rw_w = d - da_w
    winb = win.astype(BF16)
    ws = [winb[:, :da_w], winb[:, da_w:2 * da_w], winb[:, 2 * da_w:3 * da_w], winb[:, 3 * da_w:]]
    q, k, v, u = _normmod_proj(xs, norm1, mods[0], mods[1], ws, [F32, F32, BF16, F32], bid_fn(tm), tm)

    cos, sin = _rope_tables(seq, tm)
    lat_tiles = t_lat // tm
    tab_block = lambda i: jnp.where(i < lat_tiles, i % (seq // tm), seq // tm)
    qr, kr = _rope(q, k, cos, sin, tab_block, tm)
    lk = ctx_len + seq
    cat = lambda a: jnp.concatenate([a[t_lat:].reshape(nb, ctx_len, da_w), a[:t_lat].reshape(nb, seq, da_w)],
                                    axis=1).reshape(nb * lk, da_w)
    tq = min(256, seq)
    o_lat = _diff_attention(qr, cat(kr), cat(v), da_lambda, da_subln, lam_init, nb, seq, lk, 0, tq)
    tqc = min(256, ctx_len)
    o_ctx = _diff_attention(qr, kr[t_lat:], v[t_lat:], da_lambda, da_subln, lam_init, nb, ctx_len, ctx_len,
                            t_lat, tqc)
    o_att = jnp.concatenate([o_lat, o_ctx], axis=0)

    seq_lens = [seq] * nb + [ctx_len] * nb
    o_rw = _rwkv_mixer(u, seq_lens, nb, seq, ctx_len, tm, rw_mu, rw_w0, rw_w2, rw_a0, rw_a2, rw_g2, rw_kk, rw_ka,
                       rw_rk, rw_lnx_w, rw_lnx_b)

    woutb = wout.astype(BF16)
    return _proj_residual([o_att, o_rw], [woutb[:da_w], woutb[da_w:]], xs, mods[2], bid_fn(tm), tm)


def _odd_mixer_last(xs, mods, nb, seq, ctx_len, tm, bid_fn, norm1, win, conv_w, conv_b, dt_bias, a_log, d_skip, gnorm,
                    wout):
    d = xs.shape[1]
    t_lat = nb * seq
    inner = wout.shape[0]
    heads = a_log.shape[1]
    conv_dim = conv_w.shape[1]
    groups = (conv_dim - inner) // (2 * M_STATE)
    winb = win.astype(BF16)
    pad = jnp.zeros((d, LANES - heads), BF16)
    w_dt = [jnp.concatenate([winb[:, inner + conv_dim + k * heads:inner + conv_dim + (k + 1) * heads], pad], axis=1)
            for k in range(2)]
    ws = [winb[:, :inner], winb[:, inner:inner + conv_dim]] + w_dt
    z, xbc_raw, dtr_f, dtr_b = _normmod_proj(xs, norm1, mods[0], mods[1], ws, [F32] * 4, bid_fn(tm), tm)
    seq_lens = [seq] * nb + [ctx_len] * nb
    dtb = jnp.zeros((2, 1, LANES), F32).at[:, 0, :heads].set(dt_bias)
    xbc, dt = _mamba_conv(xbc_raw, jnp.stack([dtr_f, dtr_b]), seq_lens, tm, conv_w, conv_b, dtb, heads)
    dtt = jnp.swapaxes(dt, 1, 2)
    h0 = jnp.zeros((nb, groups, M_STATE, inner // groups), F32)
    ssd = functools.partial(_ssd_pass, xbc, inner=inner, groups=groups, heads=heads)
    _, hf = ssd(dt[0:1], dtt[0:1], a_log[0], h0, None, nb, ctx_len, t_lat, False)
    _, hb = ssd(dt[1:2], dtt[1:2], a_log[1], h0, None, nb, ctx_len, t_lat, True)
    y, _ = ssd(dt[0:1], dtt[0:1], a_log[0], hf, None, nb, seq, 0, False)
    y, _ = ssd(dt[1:2], dtt[1:2], a_log[1], hb, y, nb, seq, 0, True)
    x_lat = xs[:t_lat]
    gated = _mamba_gate(y, xbc, z, jnp.repeat(d_skip, M_HEAD_DIM), gnorm, groups, tm)
    return _proj_residual([gated], [wout.astype(BF16)], x_lat, mods[2], bid_fn(tm), tm)


def kernel(x, c, ctx, c_ctx, ada_w_0, ada_b_0, norm1_0, norm2_0, win_0, da_lambda_0, da_subln_0, rw_mu_0, rw_w0_0, rw_w2_0, rw_a0_0, rw_a2_0, rw_g2_0, rw_kk_0, rw_ka_0, rw_rk_0, rw_lnx_w_0, rw_lnx_b_0, wout_0, peer_q_0, peer_keys_0, peer_u_0, peer_v_0, ada_w_1, ada_b_1, norm1_1, norm2_1, win_1, conv_w_1, conv_b_1, dt_bias_1, a_log_1, d_skip_1, gnorm_1, wout_1, peer_q_1, peer_keys_1, peer_u_1, peer_v_1, norm_f):
    nb_all, seq, d = x.shape
    ctx_len = ctx.shape[1]
    tm = 256 if (seq % 256 == 0 and ctx_len % 256 == 0) else 128
    assert seq % tm == 0 and ctx_len % tm == 0 and seq % GRID_W == 0 and nb_all % BATCH_STREAMS == 0
    nb = nb_all // BATCH_STREAMS

    def bid_fn(tile):
        per = seq // tile
        return lambda i: jnp.minimum(i // per, nb)

    def expert_rows(u, v):
        bits = lambda a: lax.bitcast_convert_type(a.astype(BF16), jnp.uint16).astype(jnp.uint32)
        return bits(u) | (bits(v) << 16)

    uv0 = expert_rows(peer_u_0, peer_v_0)
    uv1 = expert_rows(peer_u_1, peer_v_1)
    wq0 = peer_q_0.astype(BF16)
    wq1 = peer_q_1.astype(BF16)
    lam_init = 0.8 - 0.6 * math.exp(-0.3 * 0)

    xs, mods0, mods1 = [], [], []
    for s in range(BATCH_STREAMS):
        sl = slice(s * nb, (s + 1) * nb)
        xs.append(jnp.concatenate([x[sl].reshape(nb * seq, d), ctx[sl].reshape(nb * ctx_len, d)], axis=0))
        cvecs = jnp.zeros((16, d), F32).at[:nb].set(c[sl]).at[nb].set(c_ctx)
        mods0.append(_ada_mod(cvecs, ada_w_0, ada_b_0))
        mods1.append(_ada_mod(cvecs, ada_w_1, ada_b_1))

    def mix0(s):
        return _even_mixer(xs[s], mods0[s], nb, seq, ctx_len, tm, bid_fn, norm1_0, win_0, da_lambda_0, da_subln_0,
                           rw_mu_0, rw_w0_0, rw_w2_0, rw_a0_0, rw_a2_0, rw_g2_0, rw_kk_0, rw_ka_0, rw_rk_0,
                           rw_lnx_w_0, rw_lnx_b_0, wout_0, lam_init)

    def mix1(s, xin):
        return _odd_mixer_last(xin, mods1[s], nb, seq, ctx_len, tm, bid_fn, norm1_1, win_1, conv_w_1, conv_b_1,
                               dt_bias_1, a_log_1, d_skip_1, gnorm_1, wout_1)

    def peer0_start(s, xin):
        return _peer_start(xin, norm2_0, mods0[s][3], mods0[s][4], wq0, peer_keys_0, uv0, bid_fn, tm)

    def peer1_start(s, xin):
        return _peer_start(xin, norm2_1, mods1[s][3], mods1[s][4], wq1, peer_keys_1, uv1, bid_fn, tm)

    streams = range(BATCH_STREAMS)
    cur, pending = {}, {}
    for s in streams:
        cur[s] = mix0(s)
        pending[s] = peer0_start(s, cur[s])
    for s in streams:
        cur[s] = _peer_finish(cur[s], pending[s], mods0[s][5], norm2_0, bid_fn, False)
        cur[s] = mix1(s, cur[s])
        pending[s] = peer1_start(s, cur[s])
    outs = [_peer_finish(cur[s], pending[s], mods1[s][5], norm_f, bid_fn, True).reshape(nb, seq, d) for s in streams]
    return jnp.concatenate(outs, axis=0)
```

```python
import functools
import math

import jax
import jax.numpy as jnp
import numpy as np
from jax import lax
from jax.experimental import pallas as pl
from jax.experimental.pallas import tpu as pltpu
from jax.experimental.pallas import tpu_sc as plsc

F32 = jnp.float32
BF16 = jnp.bfloat16
I32 = jnp.int32
HI = lax.Precision.HIGHEST

EPS = 1e-6
N_MOD = 6
GRID_W = 64
LANES = 128
VMEM_LIMIT_BYTES = 48 * 1024 * 1024

DA_HEAD_DIM = 64
DA_V_DIM = 128
ROPE_BASE = 10000.0
ROPE_NFREQ = DA_HEAD_DIM // 4
RW_HEAD_DIM = 64
RW_GN_EPS = 64e-5
RW_CHUNK = 64
RW_PASSES = 1
RW_STATE_PASSES = 3
M_HEAD_DIM = 64
M_STATE = 128
M_CHUNK = 128
PEER_HEADS = 8
PEER_NKEYS = 128
PEER_TOPK = 16
PEER_TOK = 8
PEER_CHUNKS = 1
BATCH_STREAMS = 8
SC_CORES = 2
SC_SUBCORES = 16
SC_WINDOW = 16
SC_NBUF = 4


def _cparams(*sem):
    return pltpu.CompilerParams(dimension_semantics=sem, vmem_limit_bytes=VMEM_LIMIT_BYTES)


def _nt(a, b, precision=None):
    return lax.dot_general(a, b, (((1,), (1,)), ((), ())), preferred_element_type=F32, precision=precision)


def _full(shape):
    nd = len(shape)
    return pl.BlockSpec(shape, lambda *_: (0,) * nd)


def _split_bf16(x):
    hi = x.astype(BF16)
    lo = (x - hi.astype(F32)).astype(BF16)
    return hi, lo


def _ada_kernel(c_ref, w_ref, b_ref, o_ref):
    c = c_ref[...]
    s = c * jax.nn.sigmoid(c)
    o_ref[...] = jnp.dot(s, w_ref[...], preferred_element_type=F32, precision=HI) + b_ref[...]


def _ada_mod(cvecs, w, b):
    r, d = cvecs.shape
    n = w.shape[1]
    tn = 1024
    m = pl.pallas_call(
        _ada_kernel,
        grid=(n // tn,),
        in_specs=[_full((r, d)), pl.BlockSpec((d, tn), lambda j: (0, j)), pl.BlockSpec((1, tn), lambda j: (0, j))],
        out_specs=pl.BlockSpec((r, tn), lambda j: (0, j)),
        out_shape=jax.ShapeDtypeStruct((r, n), F32),
        compiler_params=_cparams("parallel"),
        name="ada_mod",
    )(cvecs, w, b.reshape(1, n))
    return [m[:, k * d:(k + 1) * d].reshape(r, 1, d) for k in range(N_MOD)]


def _normmod_kernel(x_ref, g_ref, sh_ref, sc_ref, *refs, n_w, want_h):
    x = x_ref[...]
    y = x * lax.rsqrt(jnp.mean(x * x, axis=-1, keepdims=True) + EPS) * g_ref[...]
    h = y * (1.0 + sc_ref[0]) + sh_ref[0]
    hb = h.astype(BF16)
    for w_ref, o_ref in zip(refs[:n_w], refs[n_w:2 * n_w]):
        o_ref[...] = jnp.dot(hb, w_ref[...], preferred_element_type=F32).astype(o_ref.dtype)
    if want_h:
        refs[2 * n_w][...] = h


def _normmod_proj(x, g, shift, scale, ws, out_dtypes, bid, tm, want_h=False, row0=0, nrows=None):
    d = x.shape[1]
    t = x.shape[0] if nrows is None else nrows
    blk0 = row0 // tm
    n_w = len(ws)
    in_specs = [pl.BlockSpec((tm, d), lambda i: (blk0 + i, 0)), _full((1, d)),
                pl.BlockSpec((1, 1, d), lambda i: (bid(blk0 + i), 0, 0)),
                pl.BlockSpec((1, 1, d), lambda i: (bid(blk0 + i), 0, 0))]
    in_specs += [_full(w.shape) for w in ws]
    out_specs = [pl.BlockSpec((tm, w.shape[1]), lambda i: (i, 0)) for w in ws]
    out_shape = [jax.ShapeDtypeStruct((t, w.shape[1]), dt) for w, dt in zip(ws, out_dtypes)]
    if want_h:
        out_specs.append(pl.BlockSpec((tm, d), lambda i: (i, 0)))
        out_shape.append(jax.ShapeDtypeStruct((t, d), F32))
    return pl.pallas_call(
        functools.partial(_normmod_kernel, n_w=n_w, want_h=want_h),
        grid=(t // tm,), in_specs=in_specs, out_specs=out_specs, out_shape=out_shape,
        compiler_params=_cparams("parallel"), name="normmod_proj",
    )(x, g.reshape(1, d), shift, scale, *ws)


def _proj_res_kernel(*refs, n_a):
    a_refs = refs[:n_a]
    w_refs = refs[n_a:2 * n_a]
    res_ref, gate_ref, o_ref = refs[2 * n_a:]
    acc = jnp.dot(a_refs[0][...], w_refs[0][...], preferred_element_type=F32)
    for a_ref, w_ref in zip(a_refs[1:], w_refs[1:]):
        acc += jnp.dot(a_ref[...], w_ref[...], preferred_element_type=F32)
    o_ref[...] = res_ref[...] + gate_ref[0] * acc


def _proj_residual(a_list, w_list, res, gate, bid, tm):
    t, n = res.shape
    n_a = len(a_list)
    in_specs = [pl.BlockSpec((tm, a.shape[1]), lambda i: (i, 0)) for a in a_list]
    in_specs += [_full(w.shape) for w in w_list]
    in_specs += [pl.BlockSpec((tm, n), lambda i: (i, 0)), pl.BlockSpec((1, 1, n), lambda i: (bid(i), 0, 0))]
    return pl.pallas_call(
        functools.partial(_proj_res_kernel, n_a=n_a),
        grid=(t // tm,), in_specs=in_specs, out_specs=pl.BlockSpec((tm, n), lambda i: (i, 0)),
        out_shape=jax.ShapeDtypeStruct((t, n), F32),
        compiler_params=_cparams("parallel"), name="proj_residual",
    )(*a_list, *w_list, res, gate)


def _rope_kernel(q_ref, k_ref, c_ref, s_ref, qo_ref, ko_ref):
    c = c_ref[...]
    s = s_ref[...]
    lane = lax.broadcasted_iota(I32, c.shape, 1)
    first = (lane % 32) < 16
    width = q_ref.shape[1]

    def rot(x):
        partner = jnp.where(first, pltpu.roll(x, LANES - 16, 1), pltpu.roll(x, 16, 1))
        return x * c + partner * s

    for g in range(width // LANES):
        sl = slice(g * LANES, (g + 1) * LANES)
        qo_ref[:, sl] = (rot(q_ref[:, sl]) * (DA_HEAD_DIM ** -0.5)).astype(qo_ref.dtype)
        ko_ref[:, sl] = rot(k_ref[:, sl]).astype(ko_ref.dtype)


def _rope_tables(seq_len, tm):
    rows = seq_len // GRID_W
    row = jnp.repeat(jnp.arange(rows, dtype=F32), GRID_W)
    col = (jnp.arange(seq_len) % GRID_W).astype(F32)
    inv = ROPE_BASE ** (-jnp.arange(ROPE_NFREQ, dtype=F32) / ROPE_NFREQ)
    ang_r = row[:, None] * inv
    ang_c = col[:, None] * inv
    cos64 = jnp.concatenate([jnp.cos(ang_r), jnp.cos(ang_r), jnp.cos(ang_c), jnp.cos(ang_c)], axis=1)
    sin64 = jnp.concatenate([-jnp.sin(ang_r), jnp.sin(ang_r), -jnp.sin(ang_c), jnp.sin(ang_c)], axis=1)
    cos = jnp.concatenate([jnp.tile(cos64, (1, 2)), jnp.ones((tm, LANES), F32)], axis=0)
    sin = jnp.concatenate([jnp.tile(sin64, (1, 2)), jnp.zeros((tm, LANES), F32)], axis=0)
    return cos, sin


def _rope(q, k, cos, sin, tab_block, tm):
    t, w = q.shape
    row = pl.BlockSpec((tm, w), lambda i: (i, 0))
    tab = pl.BlockSpec((tm, LANES), lambda i: (tab_block(i), 0))
    return pl.pallas_call(
        _rope_kernel, grid=(t // tm,), in_specs=[row, row, tab, tab], out_specs=[row, row],
        out_shape=[jax.ShapeDtypeStruct((t, w), BF16)] * 2,
        compiler_params=_cparams("parallel"), name="rope",
    )(q, k, cos, sin)


def _attn_kernel(lam_ref, sub_ref, q_ref, k_ref, v_ref, o_ref, *, lam_init):
    lp = lam_ref[...]
    lam = (jnp.exp(jnp.sum(lp[0:1] * lp[1:2], keepdims=True))
           - jnp.exp(jnp.sum(lp[2:3] * lp[3:4], keepdims=True)) + lam_init)
    q = q_ref[...]
    k = k_ref[...]
    v = v_ref[...]
    lane = lax.broadcasted_iota(I32, q.shape, 1)
    outs = []
    for m in range(2):
        sel = (lane < DA_HEAD_DIM) if m == 0 else (lane >= DA_HEAD_DIM)
        s = _nt(jnp.where(sel, q, jnp.zeros_like(q)), k)
        p = jnp.exp(s - jnp.max(s, axis=-1, keepdims=True))
        denom = jnp.sum(p, axis=-1, keepdims=True)
        outs.append(jnp.dot(p.astype(BF16), v, preferred_element_type=F32) / denom)
    o = outs[0] - lam * outs[1]
    o = o * lax.rsqrt(jnp.mean(o * o, axis=-1, keepdims=True) + EPS) * sub_ref[...] * (1.0 - lam_init)
    o_ref[...] = o.astype(o_ref.dtype)


def _diff_attention(q, k, v, lamp, subln, lam_init, nb, lq, lk, q_row0, tq):
    w = q.shape[1]
    heads = w // DA_V_DIM
    nq = lq // tq
    qb0 = q_row0 // tq
    return pl.pallas_call(
        functools.partial(_attn_kernel, lam_init=lam_init),
        grid=(nb, heads, nq),
        in_specs=[_full(lamp.shape), _full((1, DA_V_DIM)),
                  pl.BlockSpec((tq, DA_V_DIM), lambda b, h, i: (qb0 + b * nq + i, h)),
                  pl.BlockSpec((lk, DA_V_DIM), lambda b, h, i: (b, h)),
                  pl.BlockSpec((lk, DA_V_DIM), lambda b, h, i: (b, h))],
        out_specs=pl.BlockSpec((tq, DA_V_DIM), lambda b, h, i: (b * nq + i, h)),
        out_shape=jax.ShapeDtypeStruct((nb * lq, w), BF16),
        compiler_params=_cparams("parallel", "parallel", "arbitrary"), name="diff_attention",
    )(lamp, subln.reshape(1, DA_V_DIM), q, k, v)


def _softplus(z):
    return jnp.maximum(z, 0.0) + jnp.log(1.0 + jnp.exp(-jnp.abs(z)))


def _rw_prep_kernel(u_ref, prev_ref, next_ref, mu_ref, w0_ref, w2_ref, a0_ref, a2_ref, g2_ref, kk_ref, ka_ref,
                    rk_ref, ones_ref, r_ref, v_ref, nkk_ref, g_ref, bonus_ref, lw_ref, kd_ref, bd_ref):
    u = u_ref[...]
    tm = u.shape[0]
    width = r_ref.shape[1]
    row = lax.broadcasted_iota(I32, u.shape, 0)
    up = jnp.where(row == 0, prev_ref[0], pltpu.roll(u, 1, 0))
    dn = jnp.where(row == tm - 1, next_ref[0], pltpu.roll(u, tm - 1, 0))
    u = u + mu_ref[...] * (0.5 * (up + dn) - u)
    r = u[:, :width]
    k = u[:, width:2 * width]
    v = u[:, 2 * width:3 * width]
    o = 3 * width
    w_in = u[:, o:o + LANES]
    a_in = u[:, o + LANES:o + 2 * LANES]
    g_in = u[:, o + 2 * LANES:o + 3 * LANES]
    ones = ones_ref[...]
    hsum = lambda t: jnp.dot(t, ones, preferred_element_type=F32, precision=HI)
    g = jnp.dot(jax.nn.sigmoid(g_in), g2_ref[...], preferred_element_type=F32, precision=HI)
    kk = k * kk_ref[...]
    kk = kk / jnp.maximum(jnp.sqrt(hsum(kk * kk)), 1e-12)
    w_log = -_softplus(-(w0_ref[...] + jnp.dot(jnp.tanh(w_in), w2_ref[...], preferred_element_type=F32,
                                               precision=HI))) - 0.5
    logw = -jnp.exp(w_log)
    a = jax.nn.sigmoid(a0_ref[...] + jnp.dot(a_in, a2_ref[...], preferred_element_type=F32, precision=HI))
    ksum = jnp.zeros_like(k)
    for d in range(2):
        a_d = a[:, d * width:(d + 1) * width]
        k_d = k * (1.0 + (a_d - 1.0) * ka_ref[...])
        ksum = ksum + k_d
        lw_ref[d] = logw[:, d * width:(d + 1) * width]
        kd_ref[d] = k_d
        bd_ref[d] = kk * a_d
    r_ref[...] = r
    v_ref[...] = v
    nkk_ref[...] = -kk
    g_ref[...] = g
    bonus_ref[...] = hsum(r * ksum * rk_ref[...]) * v


def _halo_rows(x, tm, seq_lens):
    t = x.shape[0]
    nt = t // tm
    starts = np.cumsum([0] + [n for n in seq_lens])[:-1]
    ends = np.cumsum(seq_lens)
    tile_start = np.arange(nt) * tm
    has_prev = ~np.isin(tile_start, starts)
    has_next = ~np.isin(tile_start + tm, ends)
    last = x[tm - 1::tm]
    first = x[0::tm]
    zero = jnp.zeros_like(first[:1])
    prev = jnp.concatenate([zero, last[:-1]], axis=0) * jnp.asarray(has_prev, x.dtype)[:, None]
    nxt = jnp.concatenate([first[1:], zero], axis=0) * jnp.asarray(has_next, x.dtype)[:, None]
    return prev[:, None, :], nxt[:, None, :]


def _block_diag2(m):
    z = jnp.zeros_like(m[0])
    return jnp.concatenate([jnp.concatenate([m[0], z], axis=1), jnp.concatenate([z, m[1]], axis=1)], axis=0)


def _head_ones(width, hd):
    idx = np.arange(width) // hd
    return jnp.asarray((idx[:, None] == idx[None, :]).astype(np.float32))


def _rwkv_prepare(u, seq_lens, tm, mu, w0, w2, a0, a2, g2, k_k, k_a, r_k):
    t, cols = u.shape
    width = k_k.shape[0]
    prev, nxt = _halo_rows(u, tm, seq_lens)
    row = lambda c: pl.BlockSpec((tm, c), lambda i: (i, 0))
    halo = pl.BlockSpec((1, 1, cols), lambda i: (i, 0, 0))
    dir_out = pl.BlockSpec((2, tm, width), lambda i: (0, i, 0))
    consts = [mu.reshape(1, cols), w0.reshape(1, 2 * width), _block_diag2(w2), a0.reshape(1, 2 * width),
              _block_diag2(a2), g2, k_k.reshape(1, width), k_a.reshape(1, width), r_k.reshape(1, width),
              _head_ones(width, RW_HEAD_DIM)]
    f = jax.ShapeDtypeStruct((t, width), F32)
    f2 = jax.ShapeDtypeStruct((2, t, width), F32)
    return pl.pallas_call(
        _rw_prep_kernel, grid=(t // tm,),
        in_specs=[row(cols), halo, halo] + [_full(c.shape) for c in consts],
        out_specs=[row(width)] * 5 + [dir_out] * 3,
        out_shape=[f] * 5 + [f2] * 3,
        compiler_params=_cparams("parallel"), name="rwkv_prepare",
    )(u, prev, nxt, *consts)


def _mm(x, y, passes):
    if passes == 6:
        return jnp.dot(x, y, preferred_element_type=F32, precision=HI)
    dot = lambda p, q: jnp.dot(p, q, preferred_element_type=F32)
    if passes == 1:
        return dot(x.astype(BF16), y.astype(BF16))
    xh, xl = _split_bf16(x)
    yh, yl = _split_bf16(y)
    return dot(xh, yh) + (dot(xh, yl) + dot(xl, yh))


def _rw_scan_kernel(r_ref, v_ref, nkk_ref, lw_ref, kd_ref, bd_ref, h0_ref, y_ref, hf_ref, h_scr):
    d = pl.program_id(1)
    c = pl.program_id(2)
    nc = pl.num_programs(2)
    cs = RW_CHUNK
    pairs = h_scr.shape[0]
    sgn = 1 - 2 * d

    @pl.when(c == 0)
    def _():
        h_scr[...] = h0_ref[0, 0]

    ri = lax.broadcasted_iota(I32, (cs, cs), 0)
    ci = lax.broadcasted_iota(I32, (cs, cs), 1)
    before_eq = jnp.where((ri - ci) * sgn >= 0, 1.0, 0.0).astype(BF16)
    lane = lax.broadcasted_iota(I32, (cs, LANES), 1)
    lo_half = lane < RW_HEAD_DIM
    n2 = 2 * cs
    rt = lax.broadcasted_iota(I32, (n2, n2), 0)
    ct = lax.broadcasted_iota(I32, (n2, n2), 1)
    dtok = ((rt & (cs - 1)) - (ct & (cs - 1))) * sgn
    strict = dtok > 0
    incl = dtok >= 0
    eye = rt == ct

    def stack2(x):
        return jnp.concatenate([jnp.where(lo_half, x, 0.0), jnp.where(lo_half, 0.0, x)], axis=0)

    mm = functools.partial(_mm, passes=RW_PASSES)
    mm_state = functools.partial(_mm, passes=RW_STATE_PASSES)

    h_in = [h_scr[p] for p in range(pairs)]
    ys, h_out = [], []
    for p in range(pairs):
        sl = slice(p * LANES, (p + 1) * LANES)
        logw = lw_ref[0, :, sl]
        r = r_ref[:, sl]
        v = v_ref[:, sl]
        a = nkk_ref[:, sl]
        k = kd_ref[0, :, sl]
        b = bd_ref[0, :, sl]

        lw_hi, lw_lo = _split_bf16(logw)
        cum = (jnp.dot(before_eq, lw_hi, preferred_element_type=F32)
               + jnp.dot(before_eq, lw_lo, preferred_element_type=F32))
        mid = cum[cs // 2:cs // 2 + 1]
        tot = jnp.sum(logw, axis=0, keepdims=True)
        e_in = jnp.exp(mid - cum)
        e_end = jnp.exp(tot - cum)

        a2 = stack2(a * jnp.exp(cum - logw - mid))
        r2 = stack2(r * jnp.exp(cum - mid))
        a2_abs = stack2(a * jnp.exp(cum - logw))
        r2_abs = stack2(r * jnp.exp(cum))
        b2 = stack2(b * e_in)
        k2 = stack2(k * e_in)
        v2 = stack2(v)
        bh2 = stack2(b * e_end)
        kh2 = stack2(k * e_end)

        b2t = b2.T
        k2t = k2.T
        nmat = jnp.where(strict, mm(a2, b2t), 0.0)
        mmat = jnp.where(strict, mm(a2, k2t), 0.0)
        qb = jnp.where(incl, mm(r2, b2t), 0.0)
        qk = jnp.where(incl, mm(r2, k2t), 0.0)

        tinv = jnp.where(eye, 1.0, 0.0) + nmat
        pw = nmat
        for _ in range(int(math.log2(cs)) - 1):
            pw = mm(pw, pw)
            tinv = tinv + mm(tinv, pw)

        w2 = mm(tinv, mm(mmat, v2))
        a2p = mm(tinv, a2_abs)
        y_intra = mm(qk, v2) + mm(qb, w2)
        r2p = r2_abs + mm(qb, a2p)
        bh2t = bh2.T
        gmat = jnp.where(eye, jnp.exp(tot), 0.0) + mm(bh2t, a2p)
        dmat = mm(bh2t, w2) + mm(kh2.T, v2)

        h = h_in[p]
        y2 = y_intra + mm_state(r2p, h)
        ys.append(y2[:cs] + y2[cs:])
        h_out.append(mm_state(gmat, h) + dmat)

    y_ref[0] = jnp.concatenate(ys, axis=1)
    for p in range(pairs):
        h_scr[p] = h_out[p]

    @pl.when(c == nc - 1)
    def _():
        hf_ref[0, 0] = h_scr[...]


def _rwkv_scan(r, v, nkk, lw, kd, bd, h0, nb, seq_len, row0):
    w = r.shape[1]
    pairs = w // LANES
    nc = seq_len // RW_CHUNK
    rb0 = row0 // RW_CHUNK

    def chunk(c, d):
        return c + d * (nc - 1 - 2 * c)

    shared = pl.BlockSpec((RW_CHUNK, w), lambda b, d, c: (rb0 + b * nc + chunk(c, d), 0))
    perdir = pl.BlockSpec((1, RW_CHUNK, w), lambda b, d, c: (d, rb0 + b * nc + chunk(c, d), 0))
    state = pl.BlockSpec((1, 1, pairs, LANES, LANES), lambda b, d, c: (d, b, 0, 0, 0))
    return pl.pallas_call(
        _rw_scan_kernel, grid=(nb, 2, nc),
        in_specs=[shared, shared, shared, perdir, perdir, perdir, state],
        out_specs=[pl.BlockSpec((1, RW_CHUNK, w), lambda b, d, c: (d, b * nc + chunk(c, d), 0)), state],
        out_shape=[jax.ShapeDtypeStruct((2, nb * seq_len, w), F32), jax.ShapeDtypeStruct(h0.shape, F32)],
        scratch_shapes=[pltpu.VMEM((pairs, LANES, LANES), F32)],
        compiler_params=_cparams("parallel", "parallel", "arbitrary"), name="rwkv_scan",
    )(r, v, nkk, lw, kd, bd, h0)


def _rw_post_kernel(y_ref, g_ref, bonus_ref, lnw_ref, lnb_ref, ones_ref, o_ref):
    y = y_ref[0] + y_ref[1]
    ones = ones_ref[...]
    hmean = lambda t: jnp.dot(t, ones, preferred_element_type=F32, precision=HI) * (1.0 / RW_HEAD_DIM)
    yc = y - hmean(y)
    var = hmean(yc * yc)
    yn = yc * lax.rsqrt(var + RW_GN_EPS) * lnw_ref[...] + lnb_ref[...]
    o_ref[...] = ((yn + bonus_ref[...]) * g_ref[...]).astype(o_ref.dtype)


def _rwkv_post(y, g, bonus, lnw, lnb, tm):
    _, t, w = y.shape
    row = pl.BlockSpec((tm, w), lambda i: (i, 0))
    return pl.pallas_call(
        _rw_post_kernel, grid=(t // tm,),
        in_specs=[pl.BlockSpec((2, tm, w), lambda i: (0, i, 0)), row, row, _full((1, w)), _full((1, w)),
                  _full((w, w))],
        out_specs=row, out_shape=jax.ShapeDtypeStruct((t, w), BF16),
        compiler_params=_cparams("parallel"), name="rwkv_post",
    )(y, g, bonus, lnw.reshape(1, w), lnb.reshape(1, w), _head_ones(w, RW_HEAD_DIM))


def _rwkv_mixer(u, seq_lens, nb, seq, ctx_len, tm, mu, w0, w2, a0, a2, g2, k_k, k_a, r_k, lnx_w, lnx_b):
    t_lat = nb * seq
    r, vv, nkk, g, bonus, lw, kd, bd = _rwkv_prepare(u, seq_lens, tm, mu, w0, w2, a0, a2, g2, k_k, k_a, r_k)
    zero = jnp.zeros((2, nb, r.shape[1] // LANES, LANES, LANES), F32)
    y_ctx, h_ctx = _rwkv_scan(r, vv, nkk, lw, kd, bd, zero, nb, ctx_len, t_lat)
    y_lat, _ = _rwkv_scan(r, vv, nkk, lw, kd, bd, h_ctx, nb, seq, 0)
    y = jnp.concatenate([y_lat, y_ctx], axis=1)
    return _rwkv_post(y, g, bonus, lnx_w, lnx_b, tm)


def _extract_topk(s, order, payload, count):
    big = float(2 ** 24)
    vals, pays = [], []
    for _ in range(count):
        m = jnp.max(s, axis=0, keepdims=True)
        first = jnp.min(jnp.where(s == m, order, big), axis=0, keepdims=True)
        hit = order == first
        vals.append(m)
        pays.append(first if payload is None else jnp.sum(jnp.where(hit, payload, 0.0), axis=0, keepdims=True))
        s = jnp.where(hit, -jnp.inf, s)
    return jnp.concatenate(vals, axis=0), jnp.concatenate(pays, axis=0)


def _pruned_candidates(v1, i1, v2, i2):
    k = PEER_TOPK
    tt = v1.shape[1]
    row8 = lax.broadcasted_iota(I32, (8, tt), 0).astype(F32)
    row16 = lax.broadcasted_iota(I32, (k, tt), 0).astype(F32)
    nk = float(PEER_NKEYS)
    sums, flats, eids = [], [], []

    def add(valid, s, flat, eid):
        unused = float(k * k + 16 * len(sums))
        sums.append(s if valid is None else jnp.where(valid, s, -jnp.inf))
        flats.append(flat if valid is None else jnp.where(valid, flat, flat + unused))
        eids.append(eid)

    def vary_j(i, rows, nvalid):
        r = row16 if rows == k else row8
        add(None if nvalid == rows else r < nvalid, v1[i:i + 1] + v2[:rows], r + float(i * k),
            i1[i:i + 1] * nk + i2[:rows])

    def vary_i(j, i0, lo, hi):
        r = row8 + float(i0)
        add(None if (lo == i0 and hi == i0 + 8) else (r >= lo) & (r < hi), v1[i0:i0 + 8] + v2[j:j + 1],
            r * float(k) + float(j), i1[i0:i0 + 8] * nk + i2[j:j + 1])

    vary_j(0, k, k)
    vary_j(1, 8, 8)
    vary_j(2, 8, 5)
    vary_j(3, 8, 4)
    vary_i(0, 8, 8, 16)
    vary_i(0, 0, 4, 8)
    vary_i(1, 0, 4, 8)
    vary_i(2, 0, 4, 5)
    return jnp.concatenate(sums, axis=0), jnp.concatenate(flats, axis=0), jnp.concatenate(eids, axis=0)


def _peer_topk_kernel(q_ref, keys_ref, idx_ref, gate_ref):
    tt = q_ref.shape[0]
    kpos = lax.broadcasted_iota(I32, (PEER_NKEYS, tt), 0).astype(F32)

    def head(h, carry):
        vs, ids = [], []
        for p in range(2):
            col = pl.multiple_of((2 * h + p) * LANES, LANES)
            s = _nt(keys_ref[h, p], q_ref[:, pl.ds(col, LANES)], HI)
            v_p, i_p = _extract_topk(s, kpos, None, PEER_TOPK)
            vs.append(v_p)
            ids.append(i_p)
        cand, flat, eid = _pruned_candidates(vs[0], ids[0], vs[1], ids[1])
        top_s, top_i = _extract_topk(cand, flat, eid, PEER_TOPK)
        e = jnp.exp(top_s - top_s[0:1])
        rows = pl.ds(pl.multiple_of(h * PEER_TOPK, PEER_TOPK), PEER_TOPK)
        gate_ref[rows, :] = e / jnp.sum(e, axis=0, keepdims=True)
        idx_ref[rows, :] = top_i.astype(I32)
        return carry

    lax.fori_loop(0, PEER_HEADS, head, 0)


def _peer_topk(q, keys, tt):
    t = q.shape[0]
    ne = PEER_HEADS * PEER_TOPK
    out = pl.BlockSpec((ne, tt), lambda i: (0, i))
    return pl.pallas_call(
        _peer_topk_kernel, grid=(t // tt,),
        in_specs=[pl.BlockSpec((tt, q.shape[1]), lambda i: (i, 0)), _full(keys.shape)],
        out_specs=[out, out],
        out_shape=[jax.ShapeDtypeStruct((ne, t), I32), jax.ShapeDtypeStruct((ne, t), F32)],
        compiler_params=_cparams("parallel"), name="peer_topk",
    )(q, keys)


def _sc_gather(table, idx):
    n = idx.shape[0]
    r = table.shape[1]
    workers = SC_CORES * SC_SUBCORES
    per_worker = n // workers
    nwin = per_worker // SC_WINDOW
    assert n == workers * nwin * SC_WINDOW and nwin % SC_NBUF == 0
    mesh = plsc.VectorSubcoreMesh(core_axis_name="c", subcore_axis_name="s")

    def body(table_hbm, idx_hbm, out_hbm, idx_v, *rest):
        bufs = rest[:SC_NBUF]
        gsem = rest[SC_NBUF:2 * SC_NBUF]
        osem = rest[2 * SC_NBUF:]
        base = (lax.axis_index("s") * SC_CORES + lax.axis_index("c")) * per_worker
        pltpu.sync_copy(idx_hbm.at[pl.ds(base, per_worker)], idx_v)

        def gather(w, b):
            return pltpu.make_async_copy(table_hbm.at[idx_v.at[pl.ds(w * SC_WINDOW, SC_WINDOW)]], bufs[b], gsem[b])

        def put(w, b):
            return pltpu.make_async_copy(bufs[b], out_hbm.at[pl.ds(base + w * SC_WINDOW, SC_WINDOW)], osem[b])

        for b in range(SC_NBUF):
            gather(b, b).start()

        @pl.loop(0, nwin, step=SC_NBUF)
        def _(w0):
            for b in range(SC_NBUF):
                w = w0 + b
                gather(w, b).wait()
                put(w, b).start()
                put(w, b).wait()

                @pl.when(w + SC_NBUF < nwin)
                def _():
                    gather(w + SC_NBUF, b).start()

    return pl.kernel(
        body, mesh=mesh, out_type=jax.ShapeDtypeStruct((n, r), table.dtype),
        scratch_types=[pltpu.VMEM((per_worker,), I32)] + [pltpu.VMEM((SC_WINDOW, r), table.dtype)] * SC_NBUF
        + [pltpu.SemaphoreType.DMA] * (2 * SC_NBUF),
    )(table, idx)


def _peer_apply_kernel(rows_ref, h_ref, gates_ref, x_ref, mod_ref, nf_ref, *rest, final_norm):
    o_ref = rest[-1]
    tt, ne = gates_ref.shape
    eye = (lax.broadcasted_iota(I32, (ne, ne), 0) == lax.broadcasted_iota(I32, (ne, ne), 1))
    outs = []
    for t in range(tt):
        packed = rows_ref[t * ne:(t + 1) * ne, :]
        u = pltpu.bitcast(packed << 16, F32)
        v = pltpu.bitcast(packed & jnp.uint32(0xFFFF0000), F32)
        pre = jnp.sum(u * h_ref[t:t + 1, :], axis=1, keepdims=True)
        pre = jnp.sum(jnp.where(eye, pre, 0.0), axis=0, keepdims=True)
        act = 0.5 * pre * (1.0 + lax.erf(pre * (2.0 ** -0.5)))
        w = jnp.sum(jnp.where(eye, gates_ref[t:t + 1, :] * act, 0.0), axis=1, keepdims=True)
        outs.append(jnp.sum(v * w, axis=0, keepdims=True))
    y = x_ref[...] + mod_ref[0] * jnp.concatenate(outs, axis=0)
    if final_norm:
        y = y * lax.rsqrt(jnp.mean(y * y, axis=-1, keepdims=True) + EPS) * nf_ref[...]
    o_ref[...] = y


def _peer_apply(rows, h, gates, x, acc, gate_mod, norm_f, token0, bid, final_norm):
    t, d = x.shape
    n, ne = gates.shape
    tt = PEER_TOK
    blk0 = token0 // tt
    local = lambda c: pl.BlockSpec((tt, c), lambda i: (i, 0))
    glob = pl.BlockSpec((tt, d), lambda i: (blk0 + i, 0))
    in_specs = [pl.BlockSpec((tt * ne, d), lambda i: (i, 0)), local(d), local(ne), glob,
                pl.BlockSpec((1, 1, d), lambda i: (bid(blk0 + i), 0, 0)), _full((1, d))]
    args = [rows, h, gates, x, gate_mod, norm_f.reshape(1, d)]
    aliases = {}
    if acc is not None:
        in_specs.append(pl.BlockSpec(memory_space=pl.ANY))
        args.append(acc)
        aliases = {len(args) - 1: 0}
    return pl.pallas_call(
        functools.partial(_peer_apply_kernel, final_norm=final_norm),
        grid=(n // tt,), in_specs=in_specs, out_specs=glob, out_shape=jax.ShapeDtypeStruct((t, d), F32),
        input_output_aliases=aliases,
        compiler_params=_cparams("parallel"), name="peer_apply",
    )(*args)


def _peer_start(x, norm2, shift, scale, wq, keys, uv, bid_fn, tm):
    per = x.shape[0] // PEER_CHUNKS
    assert x.shape[0] == per * PEER_CHUNKS and per % tm == 0
    chunks = []
    for k in range(PEER_CHUNKS):
        q, h = _normmod_proj(x, norm2, shift, scale, [wq], [F32], bid_fn(tm), tm, want_h=True, row0=k * per,
                             nrows=per)
        idx_t, gates_t = _peer_topk(q, keys, LANES)
        rows = _sc_gather(uv, idx_t.T.reshape(per * idx_t.shape[0]))
        chunks.append((rows, h, gates_t.T))
    return chunks


def _peer_finish(x, chunks, gate_mod, norm_f, bid_fn, final_norm):
    per = x.shape[0] // PEER_CHUNKS
    acc = None
    for k, (rows, h, gates) in enumerate(chunks):
        acc = _peer_apply(rows, h, gates, x, acc, gate_mod, norm_f, k * per, bid_fn(PEER_TOK), final_norm)
    return acc


def _conv_kernel(x_ref, prev_ref, next_ref, w_ref, b_ref, dtr_ref, dtb_ref, o_ref, dt_ref, *, heads):
    x = x_ref[...]
    tm = x.shape[0]
    row = lax.broadcasted_iota(I32, x.shape, 0)
    up = jnp.where(row == 0, prev_ref[0], pltpu.roll(x, 1, 0))
    dn = jnp.where(row == tm - 1, next_ref[0], pltpu.roll(x, tm - 1, 0))
    y = up * w_ref[0:1] + x * w_ref[1:2] + dn * w_ref[2:3] + b_ref[...]
    o_ref[...] = y * jax.nn.sigmoid(y)
    lane = lax.broadcasted_iota(I32, (tm, LANES), 1)
    for d in range(2):
        dt_ref[d] = jnp.where(lane < heads, _softplus(dtr_ref[d] + dtb_ref[d]), 0.0)


def _mamba_conv(xbc, dt_raw, seq_lens, tm, conv_w, conv_b, dt_bias_pad, heads):
    t, c = xbc.shape
    prev, nxt = _halo_rows(xbc, tm, seq_lens)
    row = pl.BlockSpec((tm, c), lambda i: (i, 0))
    halo = pl.BlockSpec((1, 1, c), lambda i: (i, 0, 0))
    dts = pl.BlockSpec((2, tm, LANES), lambda i: (0, i, 0))
    return pl.pallas_call(
        functools.partial(_conv_kernel, heads=heads), grid=(t // tm,),
        in_specs=[row, halo, halo, _full(conv_w.shape), _full((1, c)), dts, _full((2, 1, LANES))],
        out_specs=[row, dts],
        out_shape=[jax.ShapeDtypeStruct((t, c), F32), jax.ShapeDtypeStruct((2, t, LANES), F32)],
        compiler_params=_cparams("parallel"), name="mamba_conv",
    )(xbc, prev, nxt, conv_w, conv_b.reshape(1, c), dt_raw, dt_bias_pad)


def _ssd_kernel(*refs, reverse, inner, groups, add_prev):
    if add_prev:
        xbc_ref, dt_ref, dtt_ref, alr_ref, alc_ref, rep_ref, h0_ref, yin_ref, y_ref, hf_ref, h_scr = refs
    else:
        xbc_ref, dt_ref, dtt_ref, alr_ref, alc_ref, rep_ref, h0_ref, y_ref, hf_ref, h_scr = refs
        yin_ref = None
    c = pl.program_id(1)
    nc = pl.num_programs(1)
    cs = M_CHUNK
    gw = inner // groups
    hpg = gw // M_HEAD_DIM

    @pl.when(c == 0)
    def _():
        h_scr[...] = h0_ref[0]

    dt = dt_ref[0]
    dtt = dtt_ref[0]
    a = dt * (-jnp.exp(alr_ref[...]))
    at = dtt * (-jnp.exp(alc_ref[...]))
    ri = lax.broadcasted_iota(I32, (cs, cs), 0)
    ci = lax.broadcasted_iota(I32, (cs, cs), 1)
    incl = (ri <= ci) if reverse else (ri >= ci)
    tri = jnp.where(incl, 1.0, 0.0)
    cum = jnp.dot(tri, a, preferred_element_type=F32, precision=HI)
    cumt = _nt(at, tri, HI)
    tot = jnp.sum(a, axis=0, keepdims=True)

    rep = rep_ref[...]

    def spread(t):
        hi, lo = _split_bf16(t)
        return jnp.dot(hi, rep, preferred_element_type=F32) + jnp.dot(lo, rep, preferred_element_type=F32)

    e_cum = spread(jnp.exp(cum))
    e_end = spread(jnp.exp(tot - cum) * dt)
    e_tot = spread(jnp.broadcast_to(jnp.exp(tot), (8, LANES)))[0:1]

    lane = lax.broadcasted_iota(I32, (cs, LANES), 1)
    lo_half = lane < M_HEAD_DIM
    ys = []
    for g in range(groups):
        bg32 = xbc_ref[:, inner + g * M_STATE:inner + (g + 1) * M_STATE]
        bg = bg32.astype(BF16)
        cg = xbc_ref[:, inner + groups * M_STATE + g * M_STATE:inner + groups * M_STATE + (g + 1) * M_STATE]
        cg = cg.astype(BF16)
        cb = _nt(cg, bg)
        hprev = h_scr[g]
        xg = xbc_ref[:, g * gw:(g + 1) * gw]
        y_off = jnp.dot(cg, hprev.astype(BF16), preferred_element_type=F32) * e_cum[:, g * gw:(g + 1) * gw]
        xd = (xg * e_end[:, g * gw:(g + 1) * gw]).astype(BF16)
        h_scr[g] = e_tot[:, g * gw:(g + 1) * gw] * hprev + jnp.dot(bg32.T.astype(BF16), xd, preferred_element_type=F32)
        for j in range(hpg // 2):
            xpair = xg[:, j * LANES:(j + 1) * LANES].astype(BF16)
            halves = []
            for hh in range(2):
                h = g * hpg + 2 * j + hh
                seg = jnp.minimum(cum[:, h:h + 1] - cumt[h:h + 1, :], 0.0)
                m = jnp.where(incl, cb * jnp.exp(seg), 0.0) * dtt[h:h + 1, :]
                halves.append(jnp.dot(m.astype(BF16), xpair, preferred_element_type=F32))
            ys.append(jnp.where(lo_half, halves[0], halves[1]) + y_off[:, j * LANES:(j + 1) * LANES])
    y = jnp.concatenate(ys, axis=1)
    if add_prev:
        y = y + yin_ref[...]
    y_ref[...] = y

    @pl.when(c == nc - 1)
    def _():
        hf_ref[0] = h_scr[...]


def _ssd_pass(xbc, dt, dtt, a_log, h0, y_prev, nb, seq_len, row0, reverse, inner, groups, heads):
    nc = seq_len // M_CHUNK
    rb0 = row0 // M_CHUNK
    c_all = xbc.shape[1]
    gw = inner // groups
    alr = jnp.zeros((1, LANES), F32).at[0, :heads].set(a_log)
    alc = jnp.broadcast_to(jnp.zeros((LANES,), F32).at[:heads].set(a_log)[:, None], (LANES, LANES))
    hid = np.arange(inner) // M_HEAD_DIM
    rep = jnp.asarray((np.arange(LANES)[:, None] == hid[None, :]).astype(np.float32), BF16)
    chunk = (lambda c: nc - 1 - c) if reverse else (lambda c: c)
    add_prev = y_prev is not None
    in_specs = [pl.BlockSpec((M_CHUNK, c_all), lambda b, c: (rb0 + b * nc + chunk(c), 0)),
                pl.BlockSpec((1, M_CHUNK, LANES), lambda b, c: (0, rb0 + b * nc + chunk(c), 0)),
                pl.BlockSpec((1, LANES, M_CHUNK), lambda b, c: (0, 0, rb0 + b * nc + chunk(c))),
                _full((1, LANES)), _full((LANES, LANES)), _full((LANES, inner)),
                pl.BlockSpec((1, groups, M_STATE, gw), lambda b, c: (b, 0, 0, 0))]
    args = [xbc, dt, dtt, alr, alc, rep, h0]
    yspec = pl.BlockSpec((M_CHUNK, inner), lambda b, c: (b * nc + chunk(c), 0))
    if add_prev:
        in_specs.append(yspec)
        args.append(y_prev)
    return pl.pallas_call(
        functools.partial(_ssd_kernel, reverse=reverse, inner=inner, groups=groups, add_prev=add_prev),
        grid=(nb, nc), in_specs=in_specs,
        out_specs=[yspec, pl.BlockSpec((1, groups, M_STATE, gw), lambda b, c: (b, 0, 0, 0))],
        out_shape=[jax.ShapeDtypeStruct((nb * seq_len, inner), F32), jax.ShapeDtypeStruct(h0.shape, F32)],
        scratch_shapes=[pltpu.VMEM((groups, M_STATE, gw), F32)],
        compiler_params=_cparams("parallel", "arbitrary"), name="ssd_pass",
    )(*args)


def _mamba_gate_kernel(y_ref, x_ref, z_ref, dsk_ref, gn_ref, o_ref, *, groups):
    z = z_ref[...]
    y = (y_ref[...] + dsk_ref[...] * x_ref[...]) * (z * jax.nn.sigmoid(z))
    gw = y.shape[1] // groups
    for g in range(groups):
        yg = y[:, g * gw:(g + 1) * gw]
        yg = yg * lax.rsqrt(jnp.mean(yg * yg, axis=-1, keepdims=True) + EPS) * gn_ref[:, g * gw:(g + 1) * gw]
        o_ref[:, g * gw:(g + 1) * gw] = yg.astype(o_ref.dtype)


def _mamba_gate(y, xbc, z, d_skip_cols, gnorm, groups, tm):
    t, inner = y.shape
    row = pl.BlockSpec((tm, inner), lambda i: (i, 0))
    return pl.pallas_call(
        functools.partial(_mamba_gate_kernel, groups=groups), grid=(t // tm,),
        in_specs=[row, row, row, _full((1, inner)), _full((1, inner))],
        out_specs=row, out_shape=jax.ShapeDtypeStruct((t, inner), BF16),
        compiler_params=_cparams("parallel"), name="mamba_gate",
    )(y, xbc, z, d_skip_cols.reshape(1, inner), gnorm.reshape(1, inner))


def _even_mixer(xs, mods, nb, seq, ctx_len, tm, bid_fn, norm1, win, da_lambda, da_subln, rw_mu, rw_w0, rw_w2, rw_a0,
                rw_a2, rw_g2, rw_kk, rw_ka, rw_rk, rw_lnx_w, rw_lnx_b, wout, lam_init):
    d = xs.shape[1]
    t_lat = nb * seq
    da_w = d // 2
    rw_w = d - da_w
    winb = win.astype(BF16)
    ws = [winb[:, :da_w], winb[:, da_w:2 * da_w], winb[:, 2 * da_w:3 * da_w], winb[:, 3 * da_w:]]
    q, k, v, u = _normmod_proj(xs, norm1, mods[0], mods[1], ws, [F32, F32, BF16, F32], bid_fn(tm), tm)

    cos, sin = _rope_tables(seq, tm)
    lat_tiles = t_lat // tm
    tab_block = lambda i: jnp.where(i < lat_tiles, i % (seq // tm), seq // tm)
    qr, kr = _rope(q, k, cos, sin, tab_block, tm)
    lk = ctx_len + seq
    cat = lambda a: jnp.concatenate([a[t_lat:].reshape(nb, ctx_len, da_w), a[:t_lat].reshape(nb, seq, da_w)],
                                    axis=1).reshape(nb * lk, da_w)
    tq = min(256, seq)
    o_lat = _diff_attention(qr, cat(kr), cat(v), da_lambda, da_subln, lam_init, nb, seq, lk, 0, tq)
    tqc = min(256, ctx_len)
    o_ctx = _diff_attention(qr, kr[t_lat:], v[t_lat:], da_lambda, da_subln, lam_init, nb, ctx_len, ctx_len,
                            t_lat, tqc)
    o_att = jnp.concatenate([o_lat, o_ctx], axis=0)

    seq_lens = [seq] * nb + [ctx_len] * nb
    o_rw = _rwkv_mixer(u, seq_lens, nb, seq, ctx_len, tm, rw_mu, rw_w0, rw_w2, rw_a0, rw_a2, rw_g2, rw_kk, rw_ka,
                       rw_rk, rw_lnx_w, rw_lnx_b)

    woutb = wout.astype(BF16)
    return _proj_residual([o_att, o_rw], [woutb[:da_w], woutb[da_w:]], xs, mods[2], bid_fn(tm), tm)


def _odd_mixer_last(xs, mods, nb, seq, ctx_len, tm, bid_fn, norm1, win, conv_w, conv_b, dt_bias, a_log, d_skip, gnorm,
                    wout):
    d = xs.shape[1]
    t_lat = nb * seq
    inner = wout.shape[0]
    heads = a_log.shape[1]
    conv_dim = conv_w.shape[1]
    groups = (conv_dim - inner) // (2 * M_STATE)
    winb = win.astype(BF16)
    pad = jnp.zeros((d, LANES - heads), BF16)
    w_dt = [jnp.concatenate([winb[:, inner + conv_dim + k * heads:inner + conv_dim + (k + 1) * heads], pad], axis=1)
            for k in range(2)]
    ws = [winb[:, :inner], winb[:, inner:inner + conv_dim]] + w_dt
    z, xbc_raw, dtr_f, dtr_b = _normmod_proj(xs, norm1, mods[0], mods[1], ws, [F32] * 4, bid_fn(tm), tm)
    seq_lens = [seq] * nb + [ctx_len] * nb
    dtb = jnp.zeros((2, 1, LANES), F32).at[:, 0, :heads].set(dt_bias)
    xbc, dt = _mamba_conv(xbc_raw, jnp.stack([dtr_f, dtr_b]), seq_lens, tm, conv_w, conv_b, dtb, heads)
    dtt = jnp.swapaxes(dt, 1, 2)
    h0 = jnp.zeros((nb, groups, M_STATE, inner // groups), F32)
    ssd = functools.partial(_ssd_pass, xbc, inner=inner, groups=groups, heads=heads)
    _, hf = ssd(dt[0:1], dtt[0:1], a_log[0], h0, None, nb, ctx_len, t_lat, False)
    _, hb = ssd(dt[1:2], dtt[1:2], a_log[1], h0, None, nb, ctx_len, t_lat, True)
    y, _ = ssd(dt[0:1], dtt[0:1], a_log[0], hf, None, nb, seq, 0, False)
    y, _ = ssd(dt[1:2], dtt[1:2], a_log[1], hb, y, nb, seq, 0, True)
    x_lat = xs[:t_lat]
    gated = _mamba_gate(y, xbc, z, jnp.repeat(d_skip, M_HEAD_DIM), gnorm, groups, tm)
    return _proj_residual([gated], [wout.astype(BF16)], x_lat, mods[2], bid_fn(tm), tm)


def kernel(x, c, ctx, c_ctx, ada_w_0, ada_b_0, norm1_0, norm2_0, win_0, da_lambda_0, da_subln_0, rw_mu_0, rw_w0_0, rw_w2_0, rw_a0_0, rw_a2_0, rw_g2_0, rw_kk_0, rw_ka_0, rw_rk_0, rw_lnx_w_0, rw_lnx_b_0, wout_0, peer_q_0, peer_keys_0, peer_u_0, peer_v_0, ada_w_1, ada_b_1, norm1_1, norm2_1, win_1, conv_w_1, conv_b_1, dt_bias_1, a_log_1, d_skip_1, gnorm_1, wout_1, peer_q_1, peer_keys_1, peer_u_1, peer_v_1, norm_f):
    nb_all, seq, d = x.shape
    ctx_len = ctx.shape[1]
    tm = 256 if (seq % 256 == 0 and ctx_len % 256 == 0) else 128
    assert seq % tm == 0 and ctx_len % tm == 0 and seq % GRID_W == 0 and nb_all % BATCH_STREAMS == 0
    nb = nb_all // BATCH_STREAMS

    def bid_fn(tile):
        per = seq // tile
        return lambda i: jnp.minimum(i // per, nb)

    def expert_rows(u, v):
        bits = lambda a: lax.bitcast_convert_type(a.astype(BF16), jnp.uint16).astype(jnp.uint32)
        return bits(u) | (bits(v) << 16)

    uv0 = expert_rows(peer_u_0, peer_v_0)
    uv1 = expert_rows(peer_u_1, peer_v_1)
    wq0 = peer_q_0.astype(BF16)
    wq1 = peer_q_1.astype(BF16)
    lam_init = 0.8 - 0.6 * math.exp(-0.3 * 0)

    xs, mods0, mods1 = [], [], []
    for s in range(BATCH_STREAMS):
        sl = slice(s * nb, (s + 1) * nb)
        xs.append(jnp.concatenate([x[sl].reshape(nb * seq, d), ctx[sl].reshape(nb * ctx_len, d)], axis=0))
        cvecs = jnp.zeros((16, d), F32).at[:nb].set(c[sl]).at[nb].set(c_ctx)
        mods0.append(_ada_mod(cvecs, ada_w_0, ada_b_0))
        mods1.append(_ada_mod(cvecs, ada_w_1, ada_b_1))

    def mix0(s):
        return _even_mixer(xs[s], mods0[s], nb, seq, ctx_len, tm, bid_fn, norm1_0, win_0, da_lambda_0, da_subln_0,
                           rw_mu_0, rw_w0_0, rw_w2_0, rw_a0_0, rw_a2_0, rw_g2_0, rw_kk_0, rw_ka_0, rw_rk_0,
                           rw_lnx_w_0, rw_lnx_b_0, wout_0, lam_init)

    def mix1(s, xin):
        return _odd_mixer_last(xin, mods1[s], nb, seq, ctx_len, tm, bid_fn, norm1_1, win_1, conv_w_1, conv_b_1,
                               dt_bias_1, a_log_1, d_skip_1, gnorm_1, wout_1)

    def peer0_start(s, xin):
        return _peer_start(xin, norm2_0, mods0[s][3], mods0[s][4], wq0, peer_keys_0, uv0, bid_fn, tm)

    def peer1_start(s, xin):
        return _peer_start(xin, norm2_1, mods1[s][3], mods1[s][4], wq1, peer_keys_1, uv1, bid_fn, tm)

    streams = range(BATCH_STREAMS)
    cur, pending = {}, {}
    for s in streams:
        cur[s] = mix0(s)
        pending[s] = peer0_start(s, cur[s])
    for s in streams:
        cur[s] = _peer_finish(cur[s], pending[s], mods0[s][5], norm2_0, bid_fn, False)
    for s in streams:
        cur[s] = mix1(s, cur[s])
        pending[s] = peer1_start(s, cur[s])
    outs = [_peer_finish(cur[s], pending[s], mods1[s][5], norm_f, bid_fn, True).reshape(nb, seq, d) for s in streams]
    return jnp.concatenate(outs, axis=0)
```
